```python
import math
import jax
import jax.numpy as jnp
from jax import lax
import numpy as np

D_MODEL = 1024
BATCH = 4
SEQ = 4096
DEPTH = 4

N_A_LAYERS = DEPTH // 2
N_B_LAYERS = DEPTH - N_A_LAYERS
N_DENSE = (DEPTH + 1) // 2
N_MOE = DEPTH // 2
SSM_WIDTH = D_MODEL
SSM_GROUP = 16
SSM_GROUPS = SSM_WIDTH // SSM_GROUP
SSM_STATE = 64
SSM_CHUNK = 128
DT_MIN = 1e-3
DT_MAX = 1e-1
N_HEADS = 8
QK_NOPE_DIM = 128
QK_ROPE_DIM = 64
V_HEAD_DIM = 128
QK_HEAD_DIM = QK_NOPE_DIM + QK_ROPE_DIM
Q_LORA_RANK = 384
KV_LORA_RANK = 256
ROPE_THETA = 10000.0
Q_BLOCK = 128
SOFTMAX_SCALE = QK_HEAD_DIM ** -0.5
D_FF = 2816
N_EXPERTS = 8
TOP_K = 2
EXPERT_FF = 3584
MOE_BLOCK = 128
NORM_EPS = 1e-6

kernel_name = 'hybrid_s5_mla_moe_yoco'


def _rmsnorm(x, g):
    xf = x.astype(jnp.float32)
    xf = xf * lax.rsqrt(jnp.mean(xf * xf, axis=-1, keepdims=True) + NORM_EPS)
    return (xf * g.astype(jnp.float32)).astype(x.dtype)


def _modulate(x, shift, scale):
    return x * (1.0 + scale[:, None, :]) + shift[:, None, :]


def _apply_rope(x, cos, sin):
    half = x.shape[-1] // 2
    xf = x.astype(jnp.float32)
    x1, x2 = xf[..., :half], xf[..., half:]
    return jnp.concatenate([x1 * cos - x2 * sin, x1 * sin + x2 * cos], axis=-1).astype(x.dtype)


def _ssm_combine(earlier, later):
    a1r, a1i, b1r, b1i = earlier
    a2r, a2i, b2r, b2i = later
    return (a2r * a1r - a2i * a1i,
            a2r * a1i + a2i * a1r,
            a2r * b1r - a2i * b1i + b2r,
            a2r * b1i + a2i * b1r + b2i)


def _s5_mixer(h, w_in, log_step, a_re, a_im, b_re, b_im, c_re, c_im, d_skip, w_out):
    bsz, seq, _ = h.shape
    f32 = jnp.float32
    u = jnp.matmul(h, w_in, preferred_element_type=f32)
    a_re, a_im = a_re.astype(f32), a_im.astype(f32)
    delta = jnp.exp(log_step.astype(f32))[:, None]
    mag = jnp.exp(a_re * delta)
    ang = a_im * delta
    lb_re, lb_im = mag * jnp.cos(ang), mag * jnp.sin(ang)
    den = a_re * a_re + a_im * a_im
    nr, ni = lb_re - 1.0, lb_im
    coef_re = (nr * a_re + ni * a_im) / den
    coef_im = (ni * a_re - nr * a_im) / den
    b_re, b_im = b_re.astype(f32), b_im.astype(f32)
    bb_re = coef_re[..., None] * b_re - coef_im[..., None] * b_im
    bb_im = coef_re[..., None] * b_im + coef_im[..., None] * b_re
    c_re, c_im = c_re.astype(f32), c_im.astype(f32)
    n_chunks = seq // SSM_CHUNK
    u_chunks = u.reshape(bsz, n_chunks, SSM_CHUNK, SSM_GROUPS, SSM_GROUP).transpose(1, 0, 2, 3, 4)

    def chunk_step(carry, u_c):
        h_re, h_im = carry
        bu_re = jnp.einsum('btgc,gpc->btgp', u_c, bb_re)
        bu_im = jnp.einsum('btgc,gpc->btgp', u_c, bb_im)
        ar = jnp.broadcast_to(lb_re, bu_re.shape)
        ai = jnp.broadcast_to(lb_im, bu_re.shape)
        acum_re, acum_im, s_re, s_im = lax.associative_scan(_ssm_combine, (ar, ai, bu_re, bu_im), axis=1)
        st_re = acum_re * h_re[:, None] - acum_im * h_im[:, None] + s_re
        st_im = acum_re * h_im[:, None] + acum_im * h_re[:, None] + s_im
        y = jnp.einsum('btgp,gcp->btgc', st_re, c_re) - jnp.einsum('btgp,gcp->btgc', st_im, c_im)
        return (st_re[:, -1], st_im[:, -1]), y

    init = (jnp.zeros((bsz, SSM_GROUPS, SSM_STATE), f32), jnp.zeros((bsz, SSM_GROUPS, SSM_STATE), f32))
    _, y = lax.scan(chunk_step, init, u_chunks)
    y = y.transpose(1, 0, 2, 3, 4).reshape(bsz, seq, SSM_WIDTH)
    y = jax.nn.gelu(y + d_skip.astype(f32) * u).astype(h.dtype)
    z = y @ w_out
    return z[..., :D_MODEL] * jax.nn.sigmoid(z[..., D_MODEL:])


def _mla_shared_kv(s, c_act, kv_ada_w, kv_ada_b, kv_norm, w_dkv, kv_lora_norm, w_ukv, cos, sin):
    bsz, seq, _ = s.shape
    shift, scale = jnp.split(c_act @ kv_ada_w + kv_ada_b, 2, axis=-1)
    hs = _modulate(_rmsnorm(s, kv_norm), shift, scale)
    ckr = hs @ w_dkv
    ckv = _rmsnorm(ckr[..., :KV_LORA_RANK], kv_lora_norm)
    k_rope = _apply_rope(ckr[..., KV_LORA_RANK:][:, :, None, :], cos, sin)
    kv = (ckv @ w_ukv).reshape(bsz, seq, N_HEADS, QK_NOPE_DIM + V_HEAD_DIM)
    k = jnp.concatenate([kv[..., :QK_NOPE_DIM], jnp.broadcast_to(k_rope, (bsz, seq, N_HEADS, QK_ROPE_DIM))], axis=-1)
    v = kv[..., QK_NOPE_DIM:]
    return k, v


def _causal_block_attention(q, k, v):
    bsz, seq, nh, dq = q.shape
    nqb = seq // Q_BLOCK
    q_blocks = q.reshape(bsz, nqb, Q_BLOCK, nh, dq).transpose(1, 0, 2, 3, 4)
    k_idx = jnp.arange(seq)

    def one_block(args):
        q_blk, blk = args
        s = jnp.einsum('bqhd,bkhd->bhqk', q_blk, k, preferred_element_type=jnp.float32) * SOFTMAX_SCALE
        q_idx = blk * Q_BLOCK + jnp.arange(Q_BLOCK)
        s = jnp.where((k_idx[None, :] <= q_idx[:, None])[None, None], s, -jnp.inf)
        p = jax.nn.softmax(s, axis=-1).astype(v.dtype)
        return jnp.einsum('bhqk,bkhv->bqhv', p, v)

    o = lax.map(one_block, (q_blocks, jnp.arange(nqb)))
    return o.transpose(1, 0, 2, 3, 4).reshape(bsz, seq, nh * v.shape[-1])


def _mla_attend(h, k, v, w_dq, q_norm, w_uq, w_o, cos, sin):
    bsz, seq, _ = h.shape
    q = (_rmsnorm(h @ w_dq, q_norm) @ w_uq).reshape(bsz, seq, N_HEADS, QK_HEAD_DIM)
    q = jnp.concatenate([q[..., :QK_NOPE_DIM], _apply_rope(q[..., QK_NOPE_DIM:], cos, sin)], axis=-1)
    return _causal_block_attention(q, k, v) @ w_o


def _swiglu(h, w_gu, w_down):
    gu = h @ w_gu
    return (jax.nn.silu(gu[..., :D_FF]) * gu[..., D_FF:]) @ w_down


def _moe_swiglu(h, router_w, w_gu, w_down):
    bsz, seq, d = h.shape
    hf = h.reshape(bsz * seq, d)
    n = hf.shape[0]
    nk = n * TOP_K
    logits = jnp.matmul(hf, router_w, preferred_element_type=jnp.float32)
    top_logit, top_e = lax.top_k(logits, TOP_K)
    gates = jax.nn.softmax(top_logit, axis=-1).reshape(nk)
    e_flat = top_e.reshape(nk)
    tok_flat = jnp.arange(nk, dtype=jnp.int32) // TOP_K
    order = jnp.argsort(e_flat)
    e_sorted = e_flat[order]
    tok_sorted = tok_flat[order]
    counts = jnp.bincount(e_flat, length=N_EXPERTS)
    padded = (counts + MOE_BLOCK - 1) // MOE_BLOCK * MOE_BLOCK
    start = jnp.cumsum(counts) - counts
    pend = jnp.cumsum(padded)
    pstart = pend - padded
    dest = pstart[e_sorted] + jnp.arange(nk, dtype=jnp.int32) - start[e_sorted]
    n_blk = -(-nk // MOE_BLOCK) + N_EXPERTS
    buf = jnp.zeros((n_blk * MOE_BLOCK, d), hf.dtype).at[dest].set(hf[tok_sorted])
    blk_expert = jnp.minimum(jnp.searchsorted(pend, jnp.arange(n_blk, dtype=jnp.int32) * MOE_BLOCK, side='right'), N_EXPERTS - 1)

    def expert_block(args):
        xb, e = args
        gu = xb @ w_gu[e]
        return (jax.nn.silu(gu[:, :EXPERT_FF]) * gu[:, EXPERT_FF:]) @ w_down[e]

    ybuf = lax.map(expert_block, (buf.reshape(n_blk, MOE_BLOCK, d), blk_expert)).reshape(n_blk * MOE_BLOCK, d)
    y = ybuf[dest] * gates[order][:, None].astype(hf.dtype)
    out = jnp.zeros_like(hf).at[tok_sorted].add(y)
    return out.reshape(bsz, seq, d)


def setup_inputs(seed: int = 0) -> dict:
    key = jax.random.key(seed)
    ks = jax.random.split(key, 34)
    f32 = jnp.float32

    def nrm(i, shape, scale):
        return scale * jax.random.normal(ks[i], shape, f32)

    def gain(i, shape):
        return 1.0 + 0.05 * jax.random.normal(ks[i], shape, f32)

    D = D_MODEL
    G, P, GC = SSM_GROUPS, SSM_STATE, SSM_GROUP
    offs = jax.random.randint(ks[2], (BATCH, 1), 0, 2048, jnp.int32)
    positions = offs + jnp.arange(SEQ, dtype=jnp.int32)[None, :]
    a_im_init = math.pi * jnp.arange(P, dtype=f32)
    return {
        'x': nrm(0, (BATCH, SEQ, D), 1.0),
        'c': nrm(1, (BATCH, D), 1.0),
        'positions': positions,
        'ada_w': nrm(3, (DEPTH, D, 6 * D), 0.5 * D ** -0.5),
        'ada_b': nrm(4, (DEPTH, 6 * D), 0.02),
        'norm_mix_pre': gain(5, (DEPTH, D)),
        'norm_mix_post': gain(6, (DEPTH, D)),
        'norm_ffn_pre': gain(7, (DEPTH, D)),
        'norm_ffn_post': gain(8, (DEPTH, D)),
        'ssm_w_in': nrm(9, (N_A_LAYERS, D, SSM_WIDTH), D ** -0.5),
        'ssm_log_step': jax.random.uniform(ks[10], (N_A_LAYERS, G), f32, math.log(DT_MIN), math.log(DT_MAX)),
        'ssm_a_re': -0.5 + nrm(11, (N_A_LAYERS, G, P), 0.01),
        'ssm_a_im': a_im_init + nrm(12, (N_A_LAYERS, G, P), 0.01),
        'ssm_b_re': nrm(13, (N_A_LAYERS, G, P, GC), (2 * GC) ** -0.5),
        'ssm_b_im': nrm(14, (N_A_LAYERS, G, P, GC), (2 * GC) ** -0.5),
        'ssm_c_re': nrm(15, (N_A_LAYERS, G, GC, P), (2 * P) ** -0.5),
        'ssm_c_im': nrm(16, (N_A_LAYERS, G, GC, P), (2 * P) ** -0.5),
        'ssm_d': nrm(17, (N_A_LAYERS, SSM_WIDTH), 1.0),
        'ssm_w_out': nrm(18, (N_A_LAYERS, SSM_WIDTH, 2 * D), SSM_WIDTH ** -0.5),
        'kv_ada_w': nrm(19, (D, 2 * D), 0.5 * D ** -0.5),
        'kv_ada_b': nrm(20, (2 * D,), 0.02),
        'kv_norm': gain(21, (D,)),
        'mla_w_dkv': nrm(22, (D, KV_LORA_RANK + QK_ROPE_DIM), D ** -0.5),
        'mla_kv_norm': gain(23, (KV_LORA_RANK,)),
        'mla_w_ukv': nrm(24, (KV_LORA_RANK, N_HEADS * (QK_NOPE_DIM + V_HEAD_DIM)), KV_LORA_RANK ** -0.5),
        'mla_w_dq': nrm(25, (N_B_LAYERS, D, Q_LORA_RANK), D ** -0.5),
        'mla_q_norm': gain(26, (N_B_LAYERS, Q_LORA_RANK)),
        'mla_w_uq': nrm(27, (N_B_LAYERS, Q_LORA_RANK, N_HEADS * QK_HEAD_DIM), Q_LORA_RANK ** -0.5),
        'mla_w_o': nrm(28, (N_B_LAYERS, N_HEADS * V_HEAD_DIM, D), (N_HEADS * V_HEAD_DIM) ** -0.5),
        'ffn_w_gu': nrm(29, (N_DENSE, D, 2 * D_FF), D ** -0.5),
        'ffn_w_down': nrm(30, (N_DENSE, D_FF, D), D_FF ** -0.5),
        'moe_router': nrm(31, (N_MOE, D, N_EXPERTS), D ** -0.5),
        'moe_w_gu': nrm(32, (N_MOE, N_EXPERTS, D, 2 * EXPERT_FF), D ** -0.5),
        'moe_w_down': nrm(33, (N_MOE, N_EXPERTS, EXPERT_FF, D), EXPERT_FF ** -0.5),
    }


def reference(x, c, positions, ada_w, ada_b, norm_mix_pre, norm_mix_post, norm_ffn_pre, norm_ffn_post,
              ssm_w_in, ssm_log_step, ssm_a_re, ssm_a_im, ssm_b_re, ssm_b_im, ssm_c_re, ssm_c_im, ssm_d, ssm_w_out,
              kv_ada_w, kv_ada_b, kv_norm, mla_w_dkv, mla_kv_norm, mla_w_ukv,
              mla_w_dq, mla_q_norm, mla_w_uq, mla_w_o,
              ffn_w_gu, ffn_w_down, moe_router, moe_w_gu, moe_w_down):
    c_act = jax.nn.silu(c)
    ada = jnp.einsum('bd,lde->lbe', c_act, ada_w) + ada_b[:, None, :]
    inv_freq = ROPE_THETA ** (-jnp.arange(0, QK_ROPE_DIM, 2, dtype=jnp.float32) / QK_ROPE_DIM)
    ang = positions.astype(jnp.float32)[..., None] * inv_freq
    cos = jnp.cos(ang)[:, :, None, :]
    sin = jnp.sin(ang)[:, :, None, :]
    k_shared = None
    v_shared = None
    for i in range(DEPTH):
        sh_m, sc_m, g_m, sh_f, sc_f, g_f = jnp.split(ada[i], 6, axis=-1)
        h = _modulate(_rmsnorm(x, norm_mix_pre[i]), sh_m, sc_m)
        if i < N_A_LAYERS:
            y = _s5_mixer(h, ssm_w_in[i], ssm_log_step[i], ssm_a_re[i], ssm_a_im[i], ssm_b_re[i], ssm_b_im[i],
                          ssm_c_re[i], ssm_c_im[i], ssm_d[i], ssm_w_out[i])
        else:
            if i == N_A_LAYERS:
                k_shared, v_shared = _mla_shared_kv(x, c_act, kv_ada_w, kv_ada_b, kv_norm, mla_w_dkv,
                                                    mla_kv_norm, mla_w_ukv, cos, sin)
            j = i - N_A_LAYERS
            y = _mla_attend(h, k_shared, v_shared, mla_w_dq[j], mla_q_norm[j], mla_w_uq[j], mla_w_o[j], cos, sin)
        x = x + g_m[:, None, :] * _rmsnorm(y, norm_mix_post[i])
        h = _modulate(_rmsnorm(x, norm_ffn_pre[i]), sh_f, sc_f)
        if i % 2 == 0:
            y = _swiglu(h, ffn_w_gu[i // 2], ffn_w_down[i // 2])
        else:
            y = _moe_swiglu(h, moe_router[i // 2], moe_w_gu[i // 2], moe_w_down[i // 2])
        x = x + g_f[:, None, :] * _rmsnorm(y, norm_ffn_post[i])
    return x
```

```python
import functools
import math

import jax
import jax.numpy as jnp
from jax import lax
from jax.experimental import pallas as pl
from jax.experimental.pallas import tpu as pltpu

F32 = jnp.float32
BF16 = jnp.bfloat16

NORM_EPS = 1e-6
LANES = 128
SSM_GROUP = 16
SSM_STATE = 64
N_HEADS = 8
QK_NOPE_DIM = 128
QK_ROPE_DIM = 64
V_HEAD_DIM = 128
KV_LORA_RANK = 256
ROPE_THETA = 10000.0
N_EXPERTS = 8
SOFTMAX_SCALE = (QK_NOPE_DIM + QK_ROPE_DIM) ** -0.5

ROW_TILE = 512
SSM_CHUNK = 128
SSM_CH_BLOCK = 128
ATTN_TILE = 512
ROUTE_TILE = 256
EXPERT_TILE = 256
GATHER_TILE = 256
FF_CHUNK = 256


def _params(*sem):
    return pltpu.CompilerParams(dimension_semantics=sem)


def _rms(x, g):
    return x * lax.rsqrt(jnp.mean(x * x, axis=-1, keepdims=True) + NORM_EPS) * g


def _modnorm(x, g, shift, scale):
    return _rms(x, g) * (1.0 + scale) + shift


def _row_spec(tm, d):
    return pl.BlockSpec((tm, d), lambda i: (i, 0))


def _const_spec(shape):
    return pl.BlockSpec(shape, lambda i: tuple(0 for _ in shape))


def _batch_spec(d, tiles_per_batch):
    return pl.BlockSpec((None, 1, d), lambda i: (i // tiles_per_batch, 0, 0))


def _ada_kernel(c_ref, w_ref, b_ref, o_ref):
    c = c_ref[...]
    ca = c * jax.nn.sigmoid(c)
    o_ref[...] = jnp.dot(ca, w_ref[...], preferred_element_type=F32) + b_ref[...]


def _ada_proj(c_pad, w, b, tn):
    nl, d, e = w.shape
    return pl.pallas_call(
        _ada_kernel,
        grid=(nl, e // tn),
        in_specs=[
            pl.BlockSpec((8, d), lambda l, j: (0, 0)),
            pl.BlockSpec((None, d, tn), lambda l, j: (l, 0, j)),
            pl.BlockSpec((None, 1, tn), lambda l, j: (l, 0, j)),
        ],
        out_specs=pl.BlockSpec((None, 8, tn), lambda l, j: (l, 0, j)),
        out_shape=jax.ShapeDtypeStruct((nl, 8, e), F32),
        compiler_params=_params("parallel", "parallel"),
        name="ada_proj",
    )(c_pad, w, b)


def _modmm_kernel(x_ref, g_ref, sh_ref, sc_ref, w_ref, o_ref):
    h = _modnorm(x_ref[...], g_ref[...], sh_ref[...], sc_ref[...]).astype(BF16)
    o_ref[...] = jnp.dot(h, w_ref[...], preferred_element_type=F32).astype(o_ref.dtype)


def _modmm(x, g, shift, scale, w, seq, out_dtype):
    n, d = x.shape
    e = w.shape[1]
    tm = ROW_TILE
    tpb = seq // tm
    return pl.pallas_call(
        _modmm_kernel,
        grid=(n // tm,),
        in_specs=[_row_spec(tm, d), _const_spec((1, d)), _batch_spec(d, tpb), _batch_spec(d, tpb),
                  _const_spec((d, e))],
        out_specs=_row_spec(tm, e),
        out_shape=jax.ShapeDtypeStruct((n, e), out_dtype),
        compiler_params=_params("parallel"),
        name="modnorm_matmul",
    )(x, g, shift, scale, w)


def _ffn_up_kernel(x_ref, g_ref, sh_ref, sc_ref, wg_ref, wu_ref, o_ref):
    h = _modnorm(x_ref[...], g_ref[...], sh_ref[...], sc_ref[...]).astype(BF16)
    f = o_ref.shape[1]
    for c in range(0, f, FF_CHUNK):
        gt = jnp.dot(h, wg_ref[:, c:c + FF_CHUNK], preferred_element_type=F32)
        up = jnp.dot(h, wu_ref[:, c:c + FF_CHUNK], preferred_element_type=F32)
        o_ref[:, c:c + FF_CHUNK] = (gt * jax.nn.sigmoid(gt) * up).astype(o_ref.dtype)


def _ffn_up(x, g, shift, scale, wg, wu, seq):
    n, d = x.shape
    f = wg.shape[1]
    tm = ROW_TILE
    tpb = seq // tm
    return pl.pallas_call(
        _ffn_up_kernel,
        grid=(n // tm,),
        in_specs=[_row_spec(tm, d), _const_spec((1, d)), _batch_spec(d, tpb), _batch_spec(d, tpb),
                  _const_spec((d, f)), _const_spec((d, f))],
        out_specs=_row_spec(tm, f),
        out_shape=jax.ShapeDtypeStruct((n, f), BF16),
        compiler_params=_params("parallel"),
        name="ffn_up",
    )(x, g, shift, scale, wg, wu)


def _mm_post_kernel(a_ref, w_ref, x_ref, gp_ref, gate_ref, o_ref, *, glu):
    y = jnp.dot(a_ref[...], w_ref[...], preferred_element_type=F32)
    if glu:
        d = o_ref.shape[1]
        y = y[:, :d] * jax.nn.sigmoid(y[:, d:])
    o_ref[...] = x_ref[...] + gate_ref[...] * _rms(y, gp_ref[...])


def _mm_post(a, w, x, g_post, gate, seq, glu):
    n, k = a.shape
    d = x.shape[1]
    e = w.shape[1]
    tm = ROW_TILE
    tpb = seq // tm
    return pl.pallas_call(
        functools.partial(_mm_post_kernel, glu=glu),
        grid=(n // tm,),
        in_specs=[_row_spec(tm, k), _const_spec((k, e)), _row_spec(tm, d), _const_spec((1, d)),
                  _batch_spec(d, tpb)],
        out_specs=_row_spec(tm, d),
        out_shape=jax.ShapeDtypeStruct((n, d), F32),
        compiler_params=_params("parallel"),
        name="matmul_post",
    )(a, w, x, g_post, gate)


def _s5_kernel(u_ref, b_ref, c_ref, d_ref, apow_ref, tpow_ref, o_ref, st_ref, carry_ref, *, n_steps):
    j = pl.program_id(2)
    t = u_ref.shape[0]
    half = carry_ref.shape[1] // 2
    pad = st_ref.shape[0] - t

    @pl.when(j == 0)
    def _():
        carry_ref[...] = jnp.zeros_like(carry_ref)
        st_ref[0:pad, :] = jnp.zeros((pad, 2 * half), F32)

    u = u_ref[...]
    st_ref[pad:, :] = jnp.dot(u.astype(BF16), b_ref[...], preferred_element_type=F32)
    for k in range(n_steps):
        sft = 1 << k
        ar = apow_ref[k:k + 1, :half]
        ai = apow_ref[k:k + 1, half:]
        cur_re = st_ref[pad:, :half]
        cur_im = st_ref[pad:, half:]
        sh_re = st_ref[pad - sft:pad - sft + t, :half]
        sh_im = st_ref[pad - sft:pad - sft + t, half:]
        st_ref[pad:, :half] = cur_re + (ar * sh_re - ai * sh_im)
        st_ref[pad:, half:] = cur_im + (ar * sh_im + ai * sh_re)
    cr = carry_ref[:, :half]
    ci = carry_ref[:, half:]
    pr = tpow_ref[:, :half]
    pi = tpow_ref[:, half:]
    s_re = st_ref[pad:, :half] + (pr * cr - pi * ci)
    s_im = st_ref[pad:, half:] + (pr * ci + pi * cr)
    carry_ref[:, :half] = s_re[t - 1:t, :]
    carry_ref[:, half:] = s_im[t - 1:t, :]
    st = jnp.concatenate([s_re, s_im], axis=1).astype(BF16)
    y = jnp.dot(st, c_ref[...], preferred_element_type=F32)
    o_ref[...] = jax.nn.gelu(y + d_ref[...] * u).astype(o_ref.dtype)


def _s5_scan(u, bblk, cblk, d_skip, apow, tpow, bsz, seq):
    n, width = u.shape
    cb = SSM_CH_BLOCK
    t = SSM_CHUNK
    nst = bblk.shape[2]
    n_steps = apow.shape[1]
    chunks = seq // t
    return pl.pallas_call(
        functools.partial(_s5_kernel, n_steps=n_steps),
        grid=(bsz, width // cb, chunks),
        in_specs=[
            pl.BlockSpec((t, cb), lambda b, c, j: (b * chunks + j, c)),
            pl.BlockSpec((None, cb, nst), lambda b, c, j: (c, 0, 0)),
            pl.BlockSpec((None, nst, cb), lambda b, c, j: (c, 0, 0)),
            pl.BlockSpec((1, cb), lambda b, c, j: (0, c)),
            pl.BlockSpec((None, n_steps, nst), lambda b, c, j: (c, 0, 0)),
            pl.BlockSpec((None, t, nst), lambda b, c, j: (c, 0, 0)),
        ],
        out_specs=pl.BlockSpec((t, cb), lambda b, c, j: (b * chunks + j, c)),
        out_shape=jax.ShapeDtypeStruct((n, width), BF16),
        scratch_shapes=[pltpu.VMEM((t // 2 + t, nst), F32), pltpu.VMEM((1, nst), F32)],
        compiler_params=_params("parallel", "parallel", "arbitrary"),
        name="s5_scan",
    )(u, bblk, cblk, d_skip, apow, tpow)


def _s5_tables(log_step, a_re, a_im, b_re, b_im, c_re, c_im):
    g, p = a_re.shape
    gpb = SSM_CH_BLOCK // SSM_GROUP
    nblk = g // gpb
    delta = jnp.exp(log_step)[:, None]
    mag = jnp.exp(a_re * delta)
    ang = a_im * delta
    lb_re, lb_im = mag * jnp.cos(ang), mag * jnp.sin(ang)
    den = a_re * a_re + a_im * a_im
    nr, ni = lb_re - 1.0, lb_im
    coef_re = (nr * a_re + ni * a_im) / den
    coef_im = (ni * a_re - nr * a_im) / den
    bb_re = coef_re[..., None] * b_re - coef_im[..., None] * b_im
    bb_im = coef_re[..., None] * b_im + coef_im[..., None] * b_re
    eye = jnp.eye(gpb, dtype=F32)

    def blockdiag_in(m):
        m = m.reshape(nblk, gpb, p, SSM_GROUP)
        return jnp.einsum('ngpc,gh->ngchp', m, eye).reshape(nblk, gpb * SSM_GROUP, gpb * p)

    def blockdiag_out(m):
        m = m.reshape(nblk, gpb, SSM_GROUP, p)
        return jnp.einsum('ngcp,gh->ngphc', m, eye).reshape(nblk, gpb * p, gpb * SSM_GROUP)

    bblk = jnp.concatenate([blockdiag_in(bb_re), blockdiag_in(bb_im)], axis=2).astype(BF16)
    cblk = jnp.concatenate([blockdiag_out(c_re), blockdiag_out(-c_im)], axis=1).astype(BF16)
    lr = lb_re.reshape(nblk, gpb * p)
    li = lb_im.reshape(nblk, gpb * p)
    n_steps = int(math.log2(SSM_CHUNK))
    sq = [(lr, li)]
    for _ in range(n_steps - 1):
        r, i = sq[-1]
        sq.append((r * r - i * i, 2.0 * r * i))
    apow = jnp.stack([jnp.concatenate([r, i], axis=1) for r, i in sq], axis=1)
    pr, pi = lr[:, None, :], li[:, None, :]
    for k in range(n_steps):
        r, i = sq[k]
        r, i = r[:, None, :], i[:, None, :]
        pr, pi = (jnp.concatenate([pr, pr * r - pi * i], axis=1),
                  jnp.concatenate([pi, pr * i + pi * r], axis=1))
    tpow = jnp.concatenate([pr, pi], axis=2)
    return bblk, cblk, apow, tpow


def _rope(x, cos, sin):
    return x * cos + pltpu.roll(x, LANES // 2, 1) * sin


def _kv_kernel(x_ref, g_ref, sh_ref, sc_ref, wc_ref, wr_ref, gl_ref, wk_ref, wv_ref, cos_ref, sin_ref,
               kn_ref, kr_ref, v_ref):
    hs = _modnorm(x_ref[...], g_ref[...], sh_ref[...], sc_ref[...]).astype(BF16)
    c = jnp.dot(hs, wc_ref[...], preferred_element_type=F32)
    ckv = _rms(c, gl_ref[...]).astype(BF16)
    kn_ref[...] = jnp.dot(ckv, wk_ref[...], preferred_element_type=F32).astype(BF16)
    v_ref[...] = jnp.dot(ckv, wv_ref[...], preferred_element_type=F32).astype(BF16)
    r = jnp.dot(hs, wr_ref[...], preferred_element_type=F32)
    kr_ref[...] = _rope(r, cos_ref[...], sin_ref[...]).astype(BF16)


def _mla_kv(x, g, shift, scale, wc, wr, gl, wk, wv, cos, sin, seq):
    n, d = x.shape
    tm = ROW_TILE
    tpb = seq // tm
    hk = wk.shape[1]
    hv = wv.shape[1]
    return pl.pallas_call(
        _kv_kernel,
        grid=(n // tm,),
        in_specs=[_row_spec(tm, d), _const_spec((1, d)), _batch_spec(d, tpb), _batch_spec(d, tpb),
                  _const_spec(wc.shape), _const_spec(wr.shape), _const_spec(gl.shape),
                  _const_spec(wk.shape), _const_spec(wv.shape), _row_spec(tm, LANES), _row_spec(tm, LANES)],
        out_specs=[_row_spec(tm, hk), _row_spec(tm, LANES), _row_spec(tm, hv)],
        out_shape=[jax.ShapeDtypeStruct((n, hk), BF16), jax.ShapeDtypeStruct((n, LANES), BF16),
                   jax.ShapeDtypeStruct((n, hv), BF16)],
        compiler_params=_params("parallel"),
        name="mla_kv",
    )(x, g, shift, scale, wc, wr, gl, wk, wv, cos, sin)


def _q_kernel(x_ref, g_ref, sh_ref, sc_ref, wd_ref, gq_ref, wn_ref, wr_ref, cos_ref, sin_ref,
              qn_ref, qr_ref):
    h = _modnorm(x_ref[...], g_ref[...], sh_ref[...], sc_ref[...]).astype(BF16)
    ql = jnp.dot(h, wd_ref[...], preferred_element_type=F32)
    qn = _rms(ql, gq_ref[...]).astype(BF16)
    qn_ref[...] = (jnp.dot(qn, wn_ref[...], preferred_element_type=F32) * SOFTMAX_SCALE).astype(BF16)
    r = jnp.dot(qn, wr_ref[...], preferred_element_type=F32)
    cos = cos_ref[...] * SOFTMAX_SCALE
    sin = sin_ref[...] * SOFTMAX_SCALE
    for hd in range(qr_ref.shape[1] // LANES):
        sl = slice(hd * LANES, (hd + 1) * LANES)
        qr_ref[:, sl] = _rope(r[:, sl], cos, sin).astype(BF16)


def _mla_q(x, g, shift, scale, wd, gq, wn, wr, cos, sin, seq):
    n, d = x.shape
    tm = ROW_TILE
    tpb = seq // tm
    e = wn.shape[1]
    return pl.pallas_call(
        _q_kernel,
        grid=(n // tm,),
        in_specs=[_row_spec(tm, d), _const_spec((1, d)), _batch_spec(d, tpb), _batch_spec(d, tpb),
                  _const_spec(wd.shape), _const_spec(gq.shape), _const_spec(wn.shape), _const_spec(wr.shape),
                  _row_spec(tm, LANES), _row_spec(tm, LANES)],
        out_specs=[_row_spec(tm, e), _row_spec(tm, e)],
        out_shape=[jax.ShapeDtypeStruct((n, e), BF16), jax.ShapeDtypeStruct((n, e), BF16)],
        compiler_params=_params("parallel"),
        name="mla_q",
    )(x, g, shift, scale, wd, gq, wn, wr, cos, sin)


def _attn_kernel(qn_ref, qr_ref, kn_ref, kr_ref, v_ref, o_ref):
    i = pl.program_id(2)
    tq = qn_ref.shape[0]
    tk = tq
    q = jnp.concatenate([qn_ref[...], qr_ref[...]], axis=1)
    row = i * tq + lax.broadcasted_iota(jnp.int32, (tq, tk), 0)
    col0 = lax.broadcasted_iota(jnp.int32, (tq, tk), 1)

    def body(j, carry):
        m, l, acc = carry
        start = pl.multiple_of(j * tk, tk)
        k = jnp.concatenate([kn_ref[pl.ds(start, tk), :], kr_ref[pl.ds(start, tk), :]], axis=1)
        s = lax.dot_general(q, k, (((1,), (1,)), ((), ())), preferred_element_type=F32)
        s = jnp.where(col0 + j * tk <= row, s, -jnp.inf)
        m_new = jnp.maximum(m, jnp.max(s, axis=-1, keepdims=True))
        alpha = jnp.exp(m - m_new)
        p = jnp.exp(s - m_new)
        l = alpha * l + jnp.sum(p, axis=-1, keepdims=True)
        acc = alpha * acc + jnp.dot(p.astype(BF16), v_ref[pl.ds(start, tk), :], preferred_element_type=F32)
        return m_new, l, acc

    init = (jnp.full((tq, 1), -jnp.inf, F32), jnp.zeros((tq, 1), F32), jnp.zeros((tq, v_ref.shape[1]), F32))
    _, l, acc = lax.fori_loop(0, i + 1, body, init)
    o_ref[...] = (acc / l).astype(o_ref.dtype)


def _attention(qn, qr, kn, kr, v, bsz, seq):
    tq = ATTN_TILE
    hd = LANES
    return pl.pallas_call(
        _attn_kernel,
        grid=(bsz, N_HEADS, seq // tq),
        in_specs=[
            pl.BlockSpec((None, tq, hd), lambda b, h, i: (b, i, h)),
            pl.BlockSpec((None, tq, hd), lambda b, h, i: (b, i, h)),
            pl.BlockSpec((None, seq, hd), lambda b, h, i: (b, 0, h)),
            pl.BlockSpec((None, seq, hd), lambda b, h, i: (b, 0, 0)),
            pl.BlockSpec((None, seq, hd), lambda b, h, i: (b, 0, h)),
        ],
        out_specs=pl.BlockSpec((None, tq, hd), lambda b, h, i: (b, i, h)),
        out_shape=jax.ShapeDtypeStruct((bsz, seq, N_HEADS * hd), BF16),
        compiler_params=_params("parallel", "parallel", "parallel"),
        name="attention",
    )(qn, qr, kn, kr, v)


def _route_kernel(x_ref, g_ref, sh_ref, sc_ref, wr_ref, h_ref, info_ref, cnt_ref, carry_ref):
    @pl.when(pl.program_id(0) == 0)
    def _():
        carry_ref[...] = jnp.zeros_like(carry_ref)

    tm = x_ref.shape[0]
    h = _modnorm(x_ref[...], g_ref[...], sh_ref[...], sc_ref[...])
    h_ref[...] = h
    logits = jnp.dot(h, wr_ref[...], preferred_element_type=F32, precision=lax.Precision.HIGHEST)
    lane = lax.broadcasted_iota(jnp.int32, (tm, LANES), 1).astype(F32)
    neg = -jnp.inf
    lg = jnp.where(lane < N_EXPERTS, logits, neg)
    l1 = jnp.max(lg, axis=-1, keepdims=True)
    e1 = jnp.min(jnp.where(lg == l1, lane, float(LANES)), axis=-1, keepdims=True)
    lg2 = jnp.where(lane == e1, neg, lg)
    l2 = jnp.max(lg2, axis=-1, keepdims=True)
    e2 = jnp.min(jnp.where(lg2 == l2, lane, float(LANES)), axis=-1, keepdims=True)
    tt = jnp.exp(l2 - l1)
    g1 = 1.0 / (1.0 + tt)
    g2 = tt / (1.0 + tt)
    oh1 = lane == e1
    oh2 = lane == e2
    oh = jnp.where(oh1 | oh2, 1.0, 0.0)
    tri = (lax.broadcasted_iota(jnp.int32, (tm, tm), 0) > lax.broadcasted_iota(jnp.int32, (tm, tm), 1))
    cum = jnp.dot(jnp.where(tri, 1.0, 0.0).astype(BF16), oh.astype(BF16),
                  preferred_element_type=F32) + carry_ref[...]
    r1 = jnp.sum(jnp.where(oh1, cum, 0.0), axis=-1, keepdims=True)
    r2 = jnp.sum(jnp.where(oh2, cum, 0.0), axis=-1, keepdims=True)
    carry_ref[...] = carry_ref[...] + jnp.sum(oh, axis=0, keepdims=True)
    cnt_ref[...] = carry_ref[...]
    info = jnp.where(lane == 0, e1,
           jnp.where(lane == 1, e2,
           jnp.where(lane == 2, g1,
           jnp.where(lane == 3, g2,
           jnp.where(lane == 4, r1, r2)))))
    info_ref[...] = info


def _route(x, g, shift, scale, wr_pad, seq):
    n, d = x.shape
    tm = ROUTE_TILE
    tpb = seq // tm
    return pl.pallas_call(
        _route_kernel,
        grid=(n // tm,),
        in_specs=[_row_spec(tm, d), _const_spec((1, d)), _batch_spec(d, tpb), _batch_spec(d, tpb),
                  _const_spec((d, LANES))],
        out_specs=[_row_spec(tm, d), _row_spec(tm, LANES), _const_spec((1, LANES))],
        out_shape=[jax.ShapeDtypeStruct((n, d), F32), jax.ShapeDtypeStruct((n, LANES), F32),
                   jax.ShapeDtypeStruct((1, LANES), F32)],
        scratch_shapes=[pltpu.VMEM((1, LANES), F32)],
        compiler_params=_params("arbitrary"),
        name="moe_route",
    )(x, g, shift, scale, wr_pad)


def _row_copy(src_hbm, dst_ref, sem, src_row, dst_row):
    return pltpu.make_async_copy(src_hbm.at[pl.ds(src_row, 1)], dst_ref.at[pl.ds(dst_row, 1)], sem)


def _gather_kernel(idx_ref, src_hbm, o_ref, sem):
    tm = o_ref.shape[0]

    def issue(r, c):
        _row_copy(src_hbm, o_ref, sem, idx_ref[0, r], r).start()
        return c

    lax.fori_loop(0, tm, issue, 0)

    def drain(r, c):
        _row_copy(src_hbm, o_ref, sem, 0, r).wait()
        return c

    lax.fori_loop(0, tm, drain, 0)


def _gather_rows(src, idx):
    rows = idx.shape[0]
    d = src.shape[1]
    tm = GATHER_TILE
    idx3 = idx.reshape(rows // tm, 1, tm)
    return pl.pallas_call(
        _gather_kernel,
        grid=(rows // tm,),
        in_specs=[pl.BlockSpec((None, 1, tm), lambda i: (i, 0, 0), memory_space=pltpu.SMEM),
                  pl.BlockSpec(memory_space=pl.ANY)],
        out_specs=_row_spec(tm, d),
        out_shape=jax.ShapeDtypeStruct((rows, d), src.dtype),
        scratch_shapes=[pltpu.SemaphoreType.DMA(())],
        compiler_params=_params("arbitrary"),
        name="row_gather",
    )(idx3, src)


def _expert_up_kernel(te_ref, x_ref, wg_ref, wu_ref, o_ref):
    h = x_ref[...].astype(BF16)
    f = o_ref.shape[1]
    for c in range(0, f, FF_CHUNK):
        gt = jnp.dot(h, wg_ref[:, c:c + FF_CHUNK], preferred_element_type=F32)
        up = jnp.dot(h, wu_ref[:, c:c + FF_CHUNK], preferred_element_type=F32)
        o_ref[:, c:c + FF_CHUNK] = (gt * jax.nn.sigmoid(gt) * up).astype(o_ref.dtype)


def _expert_up(buf, wg, wu, tile_expert):
    rows, d = buf.shape
    f = wg.shape[2]
    tm = EXPERT_TILE
    return pl.pallas_call(
        _expert_up_kernel,
        grid_spec=pltpu.PrefetchScalarGridSpec(
            num_scalar_prefetch=1,
            grid=(rows // tm,),
            in_specs=[pl.BlockSpec((tm, d), lambda i, te: (i, 0)),
                      pl.BlockSpec((None, d, f), lambda i, te: (te[i], 0, 0)),
                      pl.BlockSpec((None, d, f), lambda i, te: (te[i], 0, 0))],
            out_specs=pl.BlockSpec((tm, f), lambda i, te: (i, 0)),
        ),
        out_shape=jax.ShapeDtypeStruct((rows, f), BF16),
        compiler_params=_params("arbitrary"),
        name="expert_up",
    )(tile_expert, buf, wg, wu)


def _expert_down_kernel(te_ref, a_ref, w_ref, o_ref):
    o_ref[...] = jnp.dot(a_ref[...], w_ref[...], preferred_element_type=F32)


def _expert_down(a, wd, tile_expert):
    rows, f = a.shape
    d = wd.shape[2]
    tm = EXPERT_TILE
    return pl.pallas_call(
        _expert_down_kernel,
        grid_spec=pltpu.PrefetchScalarGridSpec(
            num_scalar_prefetch=1,
            grid=(rows // tm,),
            in_specs=[pl.BlockSpec((tm, f), lambda i, te: (i, 0)),
                      pl.BlockSpec((None, f, d), lambda i, te: (te[i], 0, 0))],
            out_specs=pl.BlockSpec((tm, d), lambda i, te: (i, 0)),
        ),
        out_shape=jax.ShapeDtypeStruct((rows, d), F32),
        compiler_params=_params("arbitrary"),
        name="expert_down",
    )(tile_expert, a, wd)


def _combine_kernel(d1_ref, d2_ref, y_hbm, info_ref, x_ref, gp_ref, gate_ref, o_ref, buf_ref, sem):
    tm = o_ref.shape[0]

    def issue(r, c):
        _row_copy(y_hbm, buf_ref.at[0], sem, d1_ref[0, r], r).start()
        _row_copy(y_hbm, buf_ref.at[1], sem, d2_ref[0, r], r).start()
        return c

    lax.fori_loop(0, tm, issue, 0)

    def drain(r, c):
        _row_copy(y_hbm, buf_ref.at[0], sem, 0, r).wait()
        _row_copy(y_hbm, buf_ref.at[1], sem, 0, r).wait()
        return c

    lax.fori_loop(0, tm, drain, 0)
    info = info_ref[...]
    y = info[:, 2:3] * buf_ref[0] + info[:, 3:4] * buf_ref[1]
    o_ref[...] = x_ref[...] + gate_ref[...] * _rms(y, gp_ref[...])


def _combine(ybuf, dest1, dest2, info, x, g_post, gate, seq):
    n, d = x.shape
    tm = GATHER_TILE
    tpb = seq // tm
    idx_spec = pl.BlockSpec((None, 1, tm), lambda i: (i, 0, 0), memory_space=pltpu.SMEM)
    return pl.pallas_call(
        _combine_kernel,
        grid=(n // tm,),
        in_specs=[idx_spec, idx_spec, pl.BlockSpec(memory_space=pl.ANY), _row_spec(tm, LANES),
                  _row_spec(tm, d), _const_spec((1, d)), _batch_spec(d, tpb)],
        out_specs=_row_spec(tm, d),
        out_shape=jax.ShapeDtypeStruct((n, d), F32),
        scratch_shapes=[pltpu.VMEM((2, tm, d), F32), pltpu.SemaphoreType.DMA(())],
        compiler_params=_params("arbitrary"),
        name="moe_combine",
    )(dest1.reshape(n // tm, 1, tm), dest2.reshape(n // tm, 1, tm), ybuf, info, x, g_post, gate)


def _moe(x, g_pre, shift, scale, router_w, wg, wu, wd, g_post, gate, seq):
    n, d = x.shape
    wr_pad = jnp.pad(router_w, ((0, 0), (0, LANES - N_EXPERTS)))
    h, info, cnt = _route(x, g_pre, shift, scale, wr_pad, seq)
    e1 = info[:, 0].astype(jnp.int32)
    e2 = info[:, 1].astype(jnp.int32)
    r1 = info[:, 4].astype(jnp.int32)
    r2 = info[:, 5].astype(jnp.int32)
    counts = cnt[0, :N_EXPERTS].astype(jnp.int32)
    te = EXPERT_TILE
    padded = (counts + te - 1) // te * te
    pend = jnp.cumsum(padded)
    pstart = pend - padded
    dest1 = pstart[e1] + r1
    dest2 = pstart[e2] + r2
    rows = 2 * n + N_EXPERTS * te
    tok = jnp.arange(n, dtype=jnp.int32)
    inv = jnp.zeros((rows,), jnp.int32).at[dest1].set(tok).at[dest2].set(tok)
    tile_expert = jnp.minimum(
        jnp.searchsorted(pend, jnp.arange(rows // te, dtype=jnp.int32) * te, side='right'),
        N_EXPERTS - 1).astype(jnp.int32)
    buf = _gather_rows(h, inv)
    a = _expert_up(buf, wg, wu, tile_expert)
    ybuf = _expert_down(a, wd, tile_expert)
    return _combine(ybuf, dest1, dest2, info, x, g_post, gate, seq)


def _rope_lanes(w):
    half = QK_ROPE_DIM // 2
    z = jnp.zeros(w.shape[:-1] + (LANES // 2 - half,), w.dtype)
    return jnp.concatenate([w[..., :half], z, w[..., half:], z], axis=-1)


def kernel(x, c, positions, ada_w, ada_b, norm_mix_pre, norm_mix_post, norm_ffn_pre, norm_ffn_post, ssm_w_in, ssm_log_step, ssm_a_re, ssm_a_im, ssm_b_re, ssm_b_im, ssm_c_re, ssm_c_im, ssm_d, ssm_w_out, kv_ada_w, kv_ada_b, kv_norm, mla_w_dkv, mla_kv_norm, mla_w_ukv, mla_w_dq, mla_q_norm, mla_w_uq, mla_w_o, ffn_w_gu, ffn_w_down, moe_router, moe_w_gu, moe_w_down):
    bsz, seq, d = x.shape
    depth = ada_w.shape[0]
    n_a = ssm_w_in.shape[0]
    n = bsz * seq
    d_ff = ffn_w_down.shape[1]
    e_ff = moe_w_down.shape[2]

    c_pad = jnp.pad(c, ((0, 8 - bsz), (0, 0)))
    ada = _ada_proj(c_pad, ada_w, ada_b[:, None, :], 2048)[:, :bsz]
    kv_ada = _ada_proj(c_pad, kv_ada_w[None], kv_ada_b[None, None, :], 2048)[0, :bsz]

    def vec(a):
        return a[:, None, :]

    def gain(gv):
        return gv[None, :]

    inv_freq = ROPE_THETA ** (-jnp.arange(0, QK_ROPE_DIM, 2, dtype=F32) / QK_ROPE_DIM)
    ang = positions.astype(F32)[..., None] * inv_freq
    cos = jnp.cos(ang).reshape(n, -1)
    sin = jnp.sin(ang).reshape(n, -1)
    zpad = jnp.zeros_like(cos)
    cos_t = jnp.concatenate([cos, zpad, cos, zpad], axis=-1)
    sin_t = jnp.concatenate([-sin, zpad, sin, zpad], axis=-1)

    xs = x.reshape(n, d)
    kn = kr = v = None
    for i in range(depth):
        sh_m, sc_m, g_m, sh_f, sc_f, g_f = [vec(a) for a in jnp.split(ada[i], 6, axis=-1)]
        if i < n_a:
            u = _modmm(xs, gain(norm_mix_pre[i]), sh_m, sc_m, ssm_w_in[i].astype(BF16), seq, F32)
            bblk, cblk, apow, tpow = _s5_tables(ssm_log_step[i], ssm_a_re[i], ssm_a_im[i], ssm_b_re[i],
                                                ssm_b_im[i], ssm_c_re[i], ssm_c_im[i])
            yg = _s5_scan(u, bblk, cblk, ssm_d[i][None, :], apow, tpow, bsz, seq)
            xs = _mm_post(yg, ssm_w_out[i].astype(BF16), xs, gain(norm_mix_post[i]), g_m, seq, glu=True)
        else:
            if i == n_a:
                kv_sh, kv_sc = [vec(a) for a in jnp.split(kv_ada, 2, axis=-1)]
                w_ukv = mla_w_ukv.reshape(KV_LORA_RANK, N_HEADS, QK_NOPE_DIM + V_HEAD_DIM)
                wk = w_ukv[:, :, :QK_NOPE_DIM].reshape(KV_LORA_RANK, -1).astype(BF16)
                wv = w_ukv[:, :, QK_NOPE_DIM:].reshape(KV_LORA_RANK, -1).astype(BF16)
                wc = mla_w_dkv[:, :KV_LORA_RANK].astype(BF16)
                wr = _rope_lanes(mla_w_dkv[:, KV_LORA_RANK:]).astype(BF16)
                kn, kr, v = _mla_kv(xs, gain(kv_norm), kv_sh, kv_sc, wc, wr, gain(mla_kv_norm), wk, wv,
                                    cos_t, sin_t, seq)
                kn = kn.reshape(bsz, seq, -1)
                kr = kr.reshape(bsz, seq, -1)
                v = v.reshape(bsz, seq, -1)
            j = i - n_a
            w_uq = mla_w_uq[j].reshape(-1, N_HEADS, QK_NOPE_DIM + QK_ROPE_DIM)
            wn = w_uq[:, :, :QK_NOPE_DIM].reshape(w_uq.shape[0], -1).astype(BF16)
            wqr = _rope_lanes(w_uq[:, :, QK_NOPE_DIM:]).reshape(w_uq.shape[0], -1).astype(BF16)
            qn, qr = _mla_q(xs, gain(norm_mix_pre[i]), sh_m, sc_m, mla_w_dq[j].astype(BF16),
                            gain(mla_q_norm[j]), wn, wqr, cos_t, sin_t, seq)
            o = _attention(qn.reshape(bsz, seq, -1), qr.reshape(bsz, seq, -1), kn, kr, v, bsz, seq)
            xs = _mm_post(o.reshape(n, -1), mla_w_o[j].astype(BF16), xs, gain(norm_mix_post[i]), g_m, seq,
                          glu=False)
        if i % 2 == 0:
            w_gu = ffn_w_gu[i // 2]
            a = _ffn_up(xs, gain(norm_ffn_pre[i]), sh_f, sc_f, w_gu[:, :d_ff].astype(BF16),
                        w_gu[:, d_ff:].astype(BF16), seq)
            xs = _mm_post(a, ffn_w_down[i // 2].astype(BF16), xs, gain(norm_ffn_post[i]), g_f, seq, glu=False)
        else:
            w_gu = moe_w_gu[i // 2]
            xs = _moe(xs, gain(norm_ffn_pre[i]), sh_f, sc_f, moe_router[i // 2],
                      w_gu[:, :, :e_ff].astype(BF16), w_gu[:, :, e_ff:].astype(BF16),
                      moe_w_down[i // 2].astype(BF16), gain(norm_ffn_post[i]), g_f, seq)
    return xs.reshape(bsz, seq, d)
```

```python
import functools
import math

import jax
import jax.numpy as jnp
from jax import lax
from jax.experimental import pallas as pl
from jax.experimental.pallas import tpu as pltpu

F32 = jnp.float32
BF16 = jnp.bfloat16

NORM_EPS = 1e-6
LANES = 128
SSM_GROUP = 16
SSM_STATE = 64
N_HEADS = 8
QK_NOPE_DIM = 128
QK_ROPE_DIM = 64
V_HEAD_DIM = 128
KV_LORA_RANK = 256
ROPE_THETA = 10000.0
N_EXPERTS = 8
SOFTMAX_SCALE = (QK_NOPE_DIM + QK_ROPE_DIM) ** -0.5
Q_SCALE = SOFTMAX_SCALE * math.log2(math.e)

ROW_TILE = 512
SSM_CHUNK = 128
SSM_CH_BLOCK = 128
ATTN_TILE = 512
ROUTE_TILE = 256
EXPERT_TILE = 256
GATHER_TILE = 256
FF_CHUNK = 256


def _params(*sem):
    return pltpu.CompilerParams(dimension_semantics=sem)


def _rms(x, g):
    return x * lax.rsqrt(jnp.mean(x * x, axis=-1, keepdims=True) + NORM_EPS) * g


def _modnorm(x, g, shift, scale):
    return _rms(x, g) * (1.0 + scale) + shift


def _row_spec(tm, d):
    return pl.BlockSpec((tm, d), lambda i: (i, 0))


def _const_spec(shape):
    return pl.BlockSpec(shape, lambda i: tuple(0 for _ in shape))


def _batch_spec(d, tiles_per_batch):
    return pl.BlockSpec((None, 1, d), lambda i: (i // tiles_per_batch, 0, 0))


def _ada_kernel(c_ref, w_ref, b_ref, o_ref):
    c = c_ref[...]
    ca = c * jax.nn.sigmoid(c)
    o_ref[...] = jnp.dot(ca, w_ref[...], preferred_element_type=F32) + b_ref[...]


def _ada_proj(c_pad, w, b, tn):
    nl, d, e = w.shape
    return pl.pallas_call(
        _ada_kernel,
        grid=(nl, e // tn),
        in_specs=[
            pl.BlockSpec((8, d), lambda l, j: (0, 0)),
            pl.BlockSpec((None, d, tn), lambda l, j: (l, 0, j)),
            pl.BlockSpec((None, 1, tn), lambda l, j: (l, 0, j)),
        ],
        out_specs=pl.BlockSpec((None, 8, tn), lambda l, j: (l, 0, j)),
        out_shape=jax.ShapeDtypeStruct((nl, 8, e), F32),
        compiler_params=_params("parallel", "parallel"),
        name="ada_proj",
    )(c_pad, w, b)


def _modmm_kernel(x_ref, g_ref, sh_ref, sc_ref, w_ref, o_ref):
    h = _modnorm(x_ref[...], g_ref[...], sh_ref[...], sc_ref[...]).astype(BF16)
    o_ref[...] = jnp.dot(h, w_ref[...], preferred_element_type=F32).astype(o_ref.dtype)


def _modmm(x, g, shift, scale, w, seq, out_dtype):
    n, d = x.shape
    e = w.shape[1]
    tm = ROW_TILE
    tpb = seq // tm
    return pl.pallas_call(
        _modmm_kernel,
        grid=(n // tm,),
        in_specs=[_row_spec(tm, d), _const_spec((1, d)), _batch_spec(d, tpb), _batch_spec(d, tpb),
                  _const_spec((d, e))],
        out_specs=_row_spec(tm, e),
        out_shape=jax.ShapeDtypeStruct((n, e), out_dtype),
        compiler_params=_params("parallel"),
        name="modnorm_matmul",
    )(x, g, shift, scale, w)


def _ffn_up_kernel(x_ref, g_ref, sh_ref, sc_ref, wg_ref, wu_ref, o_ref):
    h = _modnorm(x_ref[...], g_ref[...], sh_ref[...], sc_ref[...]).astype(BF16)
    f = o_ref.shape[1]
    for c in range(0, f, FF_CHUNK):
        gt = jnp.dot(h, wg_ref[:, c:c + FF_CHUNK], preferred_element_type=F32)
        up = jnp.dot(h, wu_ref[:, c:c + FF_CHUNK], preferred_element_type=F32)
        o_ref[:, c:c + FF_CHUNK] = (gt * jax.nn.sigmoid(gt) * up).astype(o_ref.dtype)


def _ffn_up(x, g, shift, scale, wg, wu, seq):
    n, d = x.shape
    f = wg.shape[1]
    tm = ROW_TILE
    tpb = seq // tm
    return pl.pallas_call(
        _ffn_up_kernel,
        grid=(n // tm,),
        in_specs=[_row_spec(tm, d), _const_spec((1, d)), _batch_spec(d, tpb), _batch_spec(d, tpb),
                  _const_spec((d, f)), _const_spec((d, f))],
        out_specs=_row_spec(tm, f),
        out_shape=jax.ShapeDtypeStruct((n, f), BF16),
        compiler_params=_params("parallel"),
        name="ffn_up",
    )(x, g, shift, scale, wg, wu)


def _mm_post_kernel(a_ref, w_ref, x_ref, gp_ref, gate_ref, o_ref, *, glu):
    y = jnp.dot(a_ref[...], w_ref[...], preferred_element_type=F32)
    if glu:
        d = o_ref.shape[1]
        y = y[:, :d] * jax.nn.sigmoid(y[:, d:])
    o_ref[...] = x_ref[...] + gate_ref[...] * _rms(y, gp_ref[...])


def _mm_post(a, w, x, g_post, gate, seq, glu):
    n, k = a.shape
    d = x.shape[1]
    e = w.shape[1]
    tm = ROW_TILE
    tpb = seq // tm
    return pl.pallas_call(
        functools.partial(_mm_post_kernel, glu=glu),
        grid=(n // tm,),
        in_specs=[_row_spec(tm, k), _const_spec((k, e)), _row_spec(tm, d), _const_spec((1, d)),
                  _batch_spec(d, tpb)],
        out_specs=_row_spec(tm, d),
        out_shape=jax.ShapeDtypeStruct((n, d), F32),
        compiler_params=_params("parallel"),
        name="matmul_post",
    )(a, w, x, g_post, gate)


def _s5_kernel(u_ref, b_ref, c_ref, d_ref, apow_ref, tpow_ref, o_ref, st_ref, carry_ref, *, n_steps):
    j = pl.program_id(2)
    t = u_ref.shape[0]
    half = carry_ref.shape[1] // 2
    pad = st_ref.shape[0] - t

    @pl.when(j == 0)
    def _():
        carry_ref[...] = jnp.zeros_like(carry_ref)
        st_ref[0:pad, :] = jnp.zeros((pad, 2 * half), F32)

    u = u_ref[...]
    st_ref[pad:, :] = jnp.dot(u.astype(BF16), b_ref[...], preferred_element_type=F32)
    for k in range(n_steps):
        sft = 1 << k
        ar = apow_ref[k:k + 1, :half]
        ai = apow_ref[k:k + 1, half:]
        cur_re = st_ref[pad:, :half]
        cur_im = st_ref[pad:, half:]
        sh_re = st_ref[pad - sft:pad - sft + t, :half]
        sh_im = st_ref[pad - sft:pad - sft + t, half:]
        st_ref[pad:, :half] = cur_re + (ar * sh_re - ai * sh_im)
        st_ref[pad:, half:] = cur_im + (ar * sh_im + ai * sh_re)
    cr = carry_ref[:, :half]
    ci = carry_ref[:, half:]
    pr = tpow_ref[:, :half]
    pi = tpow_ref[:, half:]
    s_re = st_ref[pad:, :half] + (pr * cr - pi * ci)
    s_im = st_ref[pad:, half:] + (pr * ci + pi * cr)
    carry_ref[:, :half] = s_re[t - 1:t, :]
    carry_ref[:, half:] = s_im[t - 1:t, :]
    st = jnp.concatenate([s_re, s_im], axis=1).astype(BF16)
    y = jnp.dot(st, c_ref[...], preferred_element_type=F32)
    o_ref[...] = jax.nn.gelu(y + d_ref[...] * u).astype(o_ref.dtype)


def _s5_scan(u, bblk, cblk, d_skip, apow, tpow, bsz, seq):
    n, width = u.shape
    cb = SSM_CH_BLOCK
    t = SSM_CHUNK
    nst = bblk.shape[2]
    n_steps = apow.shape[1]
    chunks = seq // t
    return pl.pallas_call(
        functools.partial(_s5_kernel, n_steps=n_steps),
        grid=(bsz, width // cb, chunks),
        in_specs=[
            pl.BlockSpec((t, cb), lambda b, c, j: (b * chunks + j, c)),
            pl.BlockSpec((None, cb, nst), lambda b, c, j: (c, 0, 0)),
            pl.BlockSpec((None, nst, cb), lambda b, c, j: (c, 0, 0)),
            pl.BlockSpec((1, cb), lambda b, c, j: (0, c)),
            pl.BlockSpec((None, n_steps, nst), lambda b, c, j: (c, 0, 0)),
            pl.BlockSpec((None, t, nst), lambda b, c, j: (c, 0, 0)),
        ],
        out_specs=pl.BlockSpec((t, cb), lambda b, c, j: (b * chunks + j, c)),
        out_shape=jax.ShapeDtypeStruct((n, width), BF16),
        scratch_shapes=[pltpu.VMEM((t // 2 + t, nst), F32), pltpu.VMEM((1, nst), F32)],
        compiler_params=_params("parallel", "parallel", "arbitrary"),
        name="s5_scan",
    )(u, bblk, cblk, d_skip, apow, tpow)


def _s5_tables(log_step, a_re, a_im, b_re, b_im, c_re, c_im):
    g, p = a_re.shape
    gpb = SSM_CH_BLOCK // SSM_GROUP
    nblk = g // gpb
    delta = jnp.exp(log_step)[:, None]
    mag = jnp.exp(a_re * delta)
    ang = a_im * delta
    lb_re, lb_im = mag * jnp.cos(ang), mag * jnp.sin(ang)
    den = a_re * a_re + a_im * a_im
    nr, ni = lb_re - 1.0, lb_im
    coef_re = (nr * a_re + ni * a_im) / den
    coef_im = (ni * a_re - nr * a_im) / den
    bb_re = coef_re[..., None] * b_re - coef_im[..., None] * b_im
    bb_im = coef_re[..., None] * b_im + coef_im[..., None] * b_re
    eye = jnp.eye(gpb, dtype=F32)

    def blockdiag_in(m):
        m = m.reshape(nblk, gpb, p, SSM_GROUP)
        return jnp.einsum('ngpc,gh->ngchp', m, eye).reshape(nblk, gpb * SSM_GROUP, gpb * p)

    def blockdiag_out(m):
        m = m.reshape(nblk, gpb, SSM_GROUP, p)
        return jnp.einsum('ngcp,gh->ngphc', m, eye).reshape(nblk, gpb * p, gpb * SSM_GROUP)

    bblk = jnp.concatenate([blockdiag_in(bb_re), blockdiag_in(bb_im)], axis=2).astype(BF16)
    cblk = jnp.concatenate([blockdiag_out(c_re), blockdiag_out(-c_im)], axis=1).astype(BF16)
    lr = lb_re.reshape(nblk, gpb * p)
    li = lb_im.reshape(nblk, gpb * p)
    n_steps = int(math.log2(SSM_CHUNK))
    sq = [(lr, li)]
    for _ in range(n_steps - 1):
        r, i = sq[-1]
        sq.append((r * r - i * i, 2.0 * r * i))
    apow = jnp.stack([jnp.concatenate([r, i], axis=1) for r, i in sq], axis=1)
    pr, pi = lr[:, None, :], li[:, None, :]
    for k in range(n_steps):
        r, i = sq[k]
        r, i = r[:, None, :], i[:, None, :]
        pr, pi = (jnp.concatenate([pr, pr * r - pi * i], axis=1),
                  jnp.concatenate([pi, pr * i + pi * r], axis=1))
    tpow = jnp.concatenate([pr, pi], axis=2)
    return bblk, cblk, apow, tpow


def _rope(x, cos, sin):
    return x * cos + pltpu.roll(x, LANES // 2, 1) * sin


def _kv_kernel(x_ref, g_ref, sh_ref, sc_ref, wc_ref, wr_ref, gl_ref, wk_ref, wv_ref, cos_ref, sin_ref,
               kn_ref, kr_ref, v_ref):
    hs = _modnorm(x_ref[...], g_ref[...], sh_ref[...], sc_ref[...]).astype(BF16)
    c = jnp.dot(hs, wc_ref[...], preferred_element_type=F32)
    ckv = _rms(c, gl_ref[...]).astype(BF16)
    kn_ref[...] = jnp.dot(ckv, wk_ref[...], preferred_element_type=F32).astype(BF16)
    v_ref[...] = jnp.dot(ckv, wv_ref[...], preferred_element_type=F32).astype(BF16)
    r = jnp.dot(hs, wr_ref[...], preferred_element_type=F32)
    kr_ref[...] = _rope(r, cos_ref[...], sin_ref[...]).astype(BF16)


def _mla_kv(x, g, shift, scale, wc, wr, gl, wk, wv, cos, sin, seq):
    n, d = x.shape
    tm = ROW_TILE
    tpb = seq // tm
    hk = wk.shape[1]
    hv = wv.shape[1]
    return pl.pallas_call(
        _kv_kernel,
        grid=(n // tm,),
        in_specs=[_row_spec(tm, d), _const_spec((1, d)), _batch_spec(d, tpb), _batch_spec(d, tpb),
                  _const_spec(wc.shape), _const_spec(wr.shape), _const_spec(gl.shape),
                  _const_spec(wk.shape), _const_spec(wv.shape), _row_spec(tm, LANES), _row_spec(tm, LANES)],
        out_specs=[_row_spec(tm, hk), _row_spec(tm, LANES), _row_spec(tm, hv)],
        out_shape=[jax.ShapeDtypeStruct((n, hk), BF16), jax.ShapeDtypeStruct((n, LANES), BF16),
                   jax.ShapeDtypeStruct((n, hv), BF16)],
        compiler_params=_params("parallel"),
        name="mla_kv",
    )(x, g, shift, scale, wc, wr, gl, wk, wv, cos, sin)


def _q_kernel(x_ref, g_ref, sh_ref, sc_ref, wd_ref, gq_ref, wn_ref, wr_ref, cos_ref, sin_ref,
              qn_ref, qr_ref):
    h = _modnorm(x_ref[...], g_ref[...], sh_ref[...], sc_ref[...]).astype(BF16)
    ql = jnp.dot(h, wd_ref[...], preferred_element_type=F32)
    qn = _rms(ql, gq_ref[...]).astype(BF16)
    qn_ref[...] = (jnp.dot(qn, wn_ref[...], preferred_element_type=F32) * Q_SCALE).astype(BF16)
    r = jnp.dot(qn, wr_ref[...], preferred_element_type=F32)
    cos = cos_ref[...] * Q_SCALE
    sin = sin_ref[...] * Q_SCALE
    for hd in range(qr_ref.shape[1] // LANES):
        sl = slice(hd * LANES, (hd + 1) * LANES)
        qr_ref[:, sl] = _rope(r[:, sl], cos, sin).astype(BF16)


def _mla_q(x, g, shift, scale, wd, gq, wn, wr, cos, sin, seq):
    n, d = x.shape
    tm = ROW_TILE
    tpb = seq // tm
    e = wn.shape[1]
    return pl.pallas_call(
        _q_kernel,
        grid=(n // tm,),
        in_specs=[_row_spec(tm, d), _const_spec((1, d)), _batch_spec(d, tpb), _batch_spec(d, tpb),
                  _const_spec(wd.shape), _const_spec(gq.shape), _const_spec(wn.shape), _const_spec(wr.shape),
                  _row_spec(tm, LANES), _row_spec(tm, LANES)],
        out_specs=[_row_spec(tm, e), _row_spec(tm, e)],
        out_shape=[jax.ShapeDtypeStruct((n, e), BF16), jax.ShapeDtypeStruct((n, e), BF16)],
        compiler_params=_params("parallel"),
        name="mla_q",
    )(x, g, shift, scale, wd, gq, wn, wr, cos, sin)


def _attn_kernel(qn_ref, qr_ref, kn_ref, kr_ref, v_ref, o_ref):
    i = pl.program_id(2)
    tq = qn_ref.shape[0]
    tk = tq
    q = jnp.concatenate([qn_ref[...], qr_ref[...]], axis=1)

    def step(j, carry, diagonal):
        m, l, acc = carry
        start = pl.multiple_of(j * tk, tk)
        k = jnp.concatenate([kn_ref[pl.ds(start, tk), :], kr_ref[pl.ds(start, tk), :]], axis=1)
        s = lax.dot_general(q, k, (((1,), (1,)), ((), ())), preferred_element_type=F32)
        if diagonal:
            row = lax.broadcasted_iota(jnp.int32, (tq, tk), 0)
            col = lax.broadcasted_iota(jnp.int32, (tq, tk), 1)
            s = jnp.where(col <= row, s, -jnp.inf)
        m_new = jnp.maximum(m, jnp.max(s, axis=-1, keepdims=True))
        alpha = jnp.exp2(m - m_new)
        p = jnp.exp2(s - m_new)
        l = alpha * l + jnp.sum(p, axis=-1, keepdims=True)
        acc = alpha * acc + jnp.dot(p.astype(BF16), v_ref[pl.ds(start, tk), :], preferred_element_type=F32)
        return m_new, l, acc

    init = (jnp.full((tq, 1), -jnp.inf, F32), jnp.zeros((tq, 1), F32), jnp.zeros((tq, v_ref.shape[1]), F32))
    def pair(jj, c):
        return step(2 * jj + 1, step(2 * jj, c, False), False)

    carry = lax.fori_loop(0, i // 2, pair, init)
    carry = lax.fori_loop(0, i % 2, lambda _, c: step(i - 1, c, False), carry)
    _, l, acc = step(i, carry, diagonal=True)
    o_ref[...] = (acc / l).astype(o_ref.dtype)


def _attention(qn, qr, kn, kr, v, bsz, seq):
    tq = ATTN_TILE
    hd = LANES
    return pl.pallas_call(
        _attn_kernel,
        grid=(bsz, N_HEADS, seq // tq),
        in_specs=[
            pl.BlockSpec((None, tq, hd), lambda b, h, i: (b, i, h)),
            pl.BlockSpec((None, tq, hd), lambda b, h, i: (b, i, h)),
            pl.BlockSpec((None, seq, hd), lambda b, h, i: (b, 0, h)),
            pl.BlockSpec((None, seq, hd), lambda b, h, i: (b, 0, 0)),
            pl.BlockSpec((None, seq, hd), lambda b, h, i: (b, 0, h)),
        ],
        out_specs=pl.BlockSpec((None, tq, hd), lambda b, h, i: (b, i, h)),
        out_shape=jax.ShapeDtypeStruct((bsz, seq, N_HEADS * hd), BF16),
        compiler_params=_params("parallel", "parallel", "parallel"),
        name="attention",
    )(qn, qr, kn, kr, v)


def _route_kernel(x_ref, g_ref, sh_ref, sc_ref, wr_ref, h_ref, info_ref, cnt_ref, carry_ref):
    @pl.when(pl.program_id(0) == 0)
    def _():
        carry_ref[...] = jnp.zeros_like(carry_ref)

    tm = x_ref.shape[0]
    h = _modnorm(x_ref[...], g_ref[...], sh_ref[...], sc_ref[...])
    _store_row_tiles(h_ref, h)
    logits = jnp.dot(h, wr_ref[...], preferred_element_type=F32, precision=lax.Precision.HIGHEST)
    lane = lax.broadcasted_iota(jnp.int32, (tm, LANES), 1).astype(F32)
    neg = -jnp.inf
    lg = jnp.where(lane < N_EXPERTS, logits, neg)
    l1 = jnp.max(lg, axis=-1, keepdims=True)
    e1 = jnp.min(jnp.where(lg == l1, lane, float(LANES)), axis=-1, keepdims=True)
    lg2 = jnp.where(lane == e1, neg, lg)
    l2 = jnp.max(lg2, axis=-1, keepdims=True)
    e2 = jnp.min(jnp.where(lg2 == l2, lane, float(LANES)), axis=-1, keepdims=True)
    tt = jnp.exp(l2 - l1)
    g1 = 1.0 / (1.0 + tt)
    g2 = tt / (1.0 + tt)
    oh1 = lane == e1
    oh2 = lane == e2
    oh = jnp.where(oh1 | oh2, 1.0, 0.0)
    tri = (lax.broadcasted_iota(jnp.int32, (tm, tm), 0) > lax.broadcasted_iota(jnp.int32, (tm, tm), 1))
    cum = jnp.dot(jnp.where(tri, 1.0, 0.0).astype(BF16), oh.astype(BF16),
                  preferred_element_type=F32) + carry_ref[...]
    r1 = jnp.sum(jnp.where(oh1, cum, 0.0), axis=-1, keepdims=True)
    r2 = jnp.sum(jnp.where(oh2, cum, 0.0), axis=-1, keepdims=True)
    carry_ref[...] = carry_ref[...] + jnp.sum(oh, axis=0, keepdims=True)
    cnt_ref[...] = carry_ref[...]
    info = jnp.where(lane == 0, e1,
           jnp.where(lane == 1, e2,
           jnp.where(lane == 2, g1,
           jnp.where(lane == 3, g2,
           jnp.where(lane == 4, r1, r2)))))
    info_ref[...] = info


def _route(x, g, shift, scale, wr_pad, seq):
    n, d = x.shape
    tm = ROUTE_TILE
    tpb = seq // tm
    return pl.pallas_call(
        _route_kernel,
        grid=(n // tm,),
        in_specs=[_row_spec(tm, d), _const_spec((1, d)), _batch_spec(d, tpb), _batch_spec(d, tpb),
                  _const_spec((d, LANES))],
        out_specs=[_row_tile_spec(tm, d), _row_spec(tm, LANES), _const_spec((1, LANES))],
        out_shape=[jax.ShapeDtypeStruct((n, d // LANES, LANES), F32), jax.ShapeDtypeStruct((n, LANES), F32),
                   jax.ShapeDtypeStruct((1, LANES), F32)],
        scratch_shapes=[pltpu.VMEM((1, LANES), F32)],
        compiler_params=_params("arbitrary"),
        name="moe_route",
    )(x, g, shift, scale, wr_pad)


def _row_tile_spec(tm, d):
    return pl.BlockSpec((tm, d // LANES, LANES), lambda i, *_: (i, 0, 0))


def _store_row_tiles(ref, val):
    for s in range(ref.shape[1]):
        ref[:, s, :] = val[:, s * LANES:(s + 1) * LANES]


def _load_row_tiles(ref):
    return jnp.concatenate([ref[:, s, :] for s in range(ref.shape[1])], axis=1)


def _start_row_gather(src_hbm, idx_ref, dst_ref, sem):
    def issue(r, c):
        pltpu.make_async_copy(src_hbm.at[pl.ds(idx_ref[0, r], 1)], dst_ref.at[pl.ds(r, 1)], sem).start()
        return c

    lax.fori_loop(0, dst_ref.shape[0], issue, 0, unroll=8)


def _wait_row_gather(src_hbm, dst_ref, sem):
    pltpu.make_async_copy(src_hbm.at[pl.ds(0, dst_ref.shape[0])], dst_ref, sem).wait()


def _expert_up_kernel(te_ref, idx_cur_ref, idx_nxt_ref, h_hbm, wg_ref, wu_ref, o_ref, xbuf_ref, sem):
    i = pl.program_id(0)
    slot = i % 2

    @pl.when(i == 0)
    def _():
        _start_row_gather(h_hbm, idx_cur_ref, xbuf_ref.at[0], sem.at[0])

    @pl.when(i + 1 < pl.num_programs(0))
    def _():
        _start_row_gather(h_hbm, idx_nxt_ref, xbuf_ref.at[1 - slot], sem.at[1 - slot])

    _wait_row_gather(h_hbm, xbuf_ref.at[slot], sem.at[slot])
    h = _load_row_tiles(xbuf_ref.at[slot]).astype(BF16)
    f = o_ref.shape[1]
    for c in range(0, f, FF_CHUNK):
        gt = jnp.dot(h, wg_ref[:, c:c + FF_CHUNK], preferred_element_type=F32)
        up = jnp.dot(h, wu_ref[:, c:c + FF_CHUNK], preferred_element_type=F32)
        o_ref[:, c:c + FF_CHUNK] = (gt * jax.nn.sigmoid(gt) * up).astype(o_ref.dtype)


def _expert_up(h3, inv, wg, wu, tile_expert):
    rows = inv.shape[0]
    _, ns, _ = h3.shape
    d = wg.shape[1]
    f = wg.shape[2]
    tm = EXPERT_TILE
    nt = rows // tm
    idx3 = inv.reshape(nt, 1, tm)
    return pl.pallas_call(
        _expert_up_kernel,
        grid_spec=pltpu.PrefetchScalarGridSpec(
            num_scalar_prefetch=1,
            grid=(nt,),
            in_specs=[pl.BlockSpec((None, 1, tm), lambda i, te: (i, 0, 0), memory_space=pltpu.SMEM),
                      pl.BlockSpec((None, 1, tm), lambda i, te: (jnp.minimum(i + 1, nt - 1), 0, 0),
                                   memory_space=pltpu.SMEM),
                      pl.BlockSpec(memory_space=pl.ANY),
                      pl.BlockSpec((None, d, f), lambda i, te: (te[i], 0, 0)),
                      pl.BlockSpec((None, d, f), lambda i, te: (te[i], 0, 0))],
            out_specs=pl.BlockSpec((tm, f), lambda i, te: (i, 0)),
            scratch_shapes=[pltpu.VMEM((2, tm, ns, LANES), F32), pltpu.SemaphoreType.DMA((2,))],
        ),
        out_shape=jax.ShapeDtypeStruct((rows, f), BF16),
        compiler_params=_params("arbitrary"),
        name="expert_up",
    )(tile_expert, idx3, idx3, h3, wg, wu)


def _expert_down_kernel(te_ref, a_ref, w_ref, o_ref):
    _store_row_tiles(o_ref, jnp.dot(a_ref[...], w_ref[...], preferred_element_type=F32))


def _expert_down(a, wd, tile_expert):
    rows, f = a.shape
    d = wd.shape[2]
    tm = EXPERT_TILE
    return pl.pallas_call(
        _expert_down_kernel,
        grid_spec=pltpu.PrefetchScalarGridSpec(
            num_scalar_prefetch=1,
            grid=(rows // tm,),
            in_specs=[pl.BlockSpec((tm, f), lambda i, te: (i, 0)),
                      pl.BlockSpec((None, f, d), lambda i, te: (te[i], 0, 0))],
            out_specs=_row_tile_spec(tm, d),
        ),
        out_shape=jax.ShapeDtypeStruct((rows, d // LANES, LANES), F32),
        compiler_params=_params("arbitrary"),
        name="expert_down",
    )(tile_expert, a, wd)


def _combine_kernel(d1_ref, d2_ref, y_hbm, info_ref, x_ref, gp_ref, gate_ref, o_ref, buf_ref, sem):
    _start_row_gather(y_hbm, d1_ref, buf_ref.at[0], sem.at[0])
    _start_row_gather(y_hbm, d2_ref, buf_ref.at[1], sem.at[1])
    info = info_ref[...]
    g1 = info[:, 2:3]
    g2 = info[:, 3:4]
    _wait_row_gather(y_hbm, buf_ref.at[0], sem.at[0])
    _wait_row_gather(y_hbm, buf_ref.at[1], sem.at[1])
    y = g1 * _load_row_tiles(buf_ref.at[0]) + g2 * _load_row_tiles(buf_ref.at[1])
    o_ref[...] = x_ref[...] + gate_ref[...] * _rms(y, gp_ref[...])


def _combine(ybuf3, dest1, dest2, info, x, g_post, gate, seq):
    n, d = x.shape
    tm = GATHER_TILE
    tpb = seq // tm
    idx_spec = pl.BlockSpec((None, 1, tm), lambda i: (i, 0, 0), memory_space=pltpu.SMEM)
    return pl.pallas_call(
        _combine_kernel,
        grid=(n // tm,),
        in_specs=[idx_spec, idx_spec, pl.BlockSpec(memory_space=pl.ANY), _row_spec(tm, LANES),
                  _row_spec(tm, d), _const_spec((1, d)), _batch_spec(d, tpb)],
        out_specs=_row_spec(tm, d),
        out_shape=jax.ShapeDtypeStruct((n, d), F32),
        scratch_shapes=[pltpu.VMEM((2, tm, d // LANES, LANES), F32), pltpu.SemaphoreType.DMA((2,))],
        compiler_params=_params("arbitrary"),
        name="moe_combine",
    )(dest1.reshape(n // tm, 1, tm), dest2.reshape(n // tm, 1, tm), ybuf3, info, x, g_post, gate)


def _moe(x, g_pre, shift, scale, router_w, wg, wu, wd, g_post, gate, seq):
    n, d = x.shape
    wr_pad = jnp.pad(router_w, ((0, 0), (0, LANES - N_EXPERTS)))
    h3, info, cnt = _route(x, g_pre, shift, scale, wr_pad, seq)
    e1 = info[:, 0].astype(jnp.int32)
    e2 = info[:, 1].astype(jnp.int32)
    r1 = info[:, 4].astype(jnp.int32)
    r2 = info[:, 5].astype(jnp.int32)
    counts = cnt[0, :N_EXPERTS].astype(jnp.int32)
    te = EXPERT_TILE
    padded = (counts + te - 1) // te * te
    pend = jnp.cumsum(padded)
    pstart = pend - padded
    dest1 = pstart[e1] + r1
    dest2 = pstart[e2] + r2
    rows = 2 * n + N_EXPERTS * te
    tok = jnp.arange(n, dtype=jnp.int32)
    inv = jnp.zeros((rows,), jnp.int32).at[dest1].set(tok).at[dest2].set(tok)
    tile_start = jnp.arange(rows // te, dtype=jnp.int32) * te
    tile_expert = jnp.minimum(jnp.sum((tile_start[:, None] >= pend[None, :]).astype(jnp.int32), axis=1),
                              N_EXPERTS - 1)
    a = _expert_up(h3, inv, wg, wu, tile_expert)
    ybuf3 = _expert_down(a, wd, tile_expert)
    return _combine(ybuf3, dest1, dest2, info, x, g_post, gate, seq)


def _rope_lanes(w):
    half = QK_ROPE_DIM // 2
    z = jnp.zeros(w.shape[:-1] + (LANES // 2 - half,), w.dtype)
    return jnp.concatenate([w[..., :half], z, w[..., half:], z], axis=-1)


def kernel(x, c, positions, ada_w, ada_b, norm_mix_pre, norm_mix_post, norm_ffn_pre, norm_ffn_post, ssm_w_in, ssm_log_step, ssm_a_re, ssm_a_im, ssm_b_re, ssm_b_im, ssm_c_re, ssm_c_im, ssm_d, ssm_w_out, kv_ada_w, kv_ada_b, kv_norm, mla_w_dkv, mla_kv_norm, mla_w_ukv, mla_w_dq, mla_q_norm, mla_w_uq, mla_w_o, ffn_w_gu, ffn_w_down, moe_router, moe_w_gu, moe_w_down):
    bsz, seq, d = x.shape
    depth = ada_w.shape[0]
    n_a = ssm_w_in.shape[0]
    n = bsz * seq
    d_ff = ffn_w_down.shape[1]
    e_ff = moe_w_down.shape[2]

    c_pad = jnp.pad(c, ((0, 8 - bsz), (0, 0)))
    ada = _ada_proj(c_pad, ada_w, ada_b[:, None, :], 2048)[:, :bsz]
    kv_ada = _ada_proj(c_pad, kv_ada_w[None], kv_ada_b[None, None, :], 2048)[0, :bsz]

    def vec(a):
        return a[:, None, :]

    def gain(gv):
        return gv[None, :]

    inv_freq = ROPE_THETA ** (-jnp.arange(0, QK_ROPE_DIM, 2, dtype=F32) / QK_ROPE_DIM)
    ang = positions.astype(F32)[..., None] * inv_freq
    cos = jnp.cos(ang).reshape(n, -1)
    sin = jnp.sin(ang).reshape(n, -1)
    zpad = jnp.zeros_like(cos)
    cos_t = jnp.concatenate([cos, zpad, cos, zpad], axis=-1)
    sin_t = jnp.concatenate([-sin, zpad, sin, zpad], axis=-1)

    xs = x.reshape(n, d)
    kn = kr = v = None
    for i in range(depth):
        sh_m, sc_m, g_m, sh_f, sc_f, g_f = [vec(a) for a in jnp.split(ada[i], 6, axis=-1)]
        if i < n_a:
            u = _modmm(xs, gain(norm_mix_pre[i]), sh_m, sc_m, ssm_w_in[i].astype(BF16), seq, F32)
            bblk, cblk, apow, tpow = _s5_tables(ssm_log_step[i], ssm_a_re[i], ssm_a_im[i], ssm_b_re[i],
                                                ssm_b_im[i], ssm_c_re[i], ssm_c_im[i])
            yg = _s5_scan(u, bblk, cblk, ssm_d[i][None, :], apow, tpow, bsz, seq)
            xs = _mm_post(yg, ssm_w_out[i].astype(BF16), xs, gain(norm_mix_post[i]), g_m, seq, glu=True)
        else:
            if i == n_a:
                kv_sh, kv_sc = [vec(a) for a in jnp.split(kv_ada, 2, axis=-1)]
                w_ukv = mla_w_ukv.reshape(KV_LORA_RANK, N_HEADS, QK_NOPE_DIM + V_HEAD_DIM)
                wk = w_ukv[:, :, :QK_NOPE_DIM].reshape(KV_LORA_RANK, -1).astype(BF16)
                wv = w_ukv[:, :, QK_NOPE_DIM:].reshape(KV_LORA_RANK, -1).astype(BF16)
                wc = mla_w_dkv[:, :KV_LORA_RANK].astype(BF16)
                wr = _rope_lanes(mla_w_dkv[:, KV_LORA_RANK:]).astype(BF16)
                kn, kr, v = _mla_kv(xs, gain(kv_norm), kv_sh, kv_sc, wc, wr, gain(mla_kv_norm), wk, wv,
                                    cos_t, sin_t, seq)
                kn = kn.reshape(bsz, seq, -1)
                kr = kr.reshape(bsz, seq, -1)
                v = v.reshape(bsz, seq, -1)
            j = i - n_a
            w_uq = mla_w_uq[j].reshape(-1, N_HEADS, QK_NOPE_DIM + QK_ROPE_DIM)
            wn = w_uq[:, :, :QK_NOPE_DIM].reshape(w_uq.shape[0], -1).astype(BF16)
            wqr = _rope_lanes(w_uq[:, :, QK_NOPE_DIM:]).reshape(w_uq.shape[0], -1).astype(BF16)
            qn, qr = _mla_q(xs, gain(norm_mix_pre[i]), sh_m, sc_m, mla_w_dq[j].astype(BF16),
                            gain(mla_q_norm[j]), wn, wqr, cos_t, sin_t, seq)
            o = _attention(qn.reshape(bsz, seq, -1), qr.reshape(bsz, seq, -1), kn, kr, v, bsz, seq)
            xs = _mm_post(o.reshape(n, -1), mla_w_o[j].astype(BF16), xs, gain(norm_mix_post[i]), g_m, seq,
                          glu=False)
        if i % 2 == 0:
            w_gu = ffn_w_gu[i // 2]
            a = _ffn_up(xs, gain(norm_ffn_pre[i]), sh_f, sc_f, w_gu[:, :d_ff].astype(BF16),
                        w_gu[:, d_ff:].astype(BF16), seq)
            xs = _mm_post(a, ffn_w_down[i // 2].astype(BF16), xs, gain(norm_ffn_post[i]), g_f, seq, glu=False)
        else:
            w_gu = moe_w_gu[i // 2]
            xs = _moe(xs, gain(norm_ffn_pre[i]), sh_f, sc_f, moe_router[i // 2],
                      w_gu[:, :, :e_ff].astype(BF16), w_gu[:, :, e_ff:].astype(BF16),
                      moe_w_down[i // 2].astype(BF16), gain(norm_ffn_post[i]), g_f, seq)
    return xs.reshape(bsz, seq, d)
```

```python
import functools
import math

import jax
import jax.numpy as jnp
from jax import lax
from jax.experimental import pallas as pl
from jax.experimental.pallas import tpu as pltpu

F32 = jnp.float32
BF16 = jnp.bfloat16

NORM_EPS = 1e-6
LANES = 128
SSM_GROUP = 16
SSM_STATE = 64
N_HEADS = 8
QK_NOPE_DIM = 128
QK_ROPE_DIM = 64
V_HEAD_DIM = 128
KV_LORA_RANK = 256
ROPE_THETA = 10000.0
N_EXPERTS = 8
SOFTMAX_SCALE = (QK_NOPE_DIM + QK_ROPE_DIM) ** -0.5
Q_SCALE = SOFTMAX_SCALE * math.log2(math.e)

ROW_TILE = 512
S5_ROW_TILE = 1024
SSM_CHUNK = 128
SSM_CH_BLOCK = 128
ATTN_TILE = 512
ROUTE_TILE = 256
EXPERT_TILE = 256
GATHER_TILE = 256
FF_CHUNK = 256


def _params(*sem):
    return pltpu.CompilerParams(dimension_semantics=sem)


def _rms(x, g):
    return x * lax.rsqrt(jnp.mean(x * x, axis=-1, keepdims=True) + NORM_EPS) * g


def _modnorm(x, g, shift, scale):
    return _rms(x, g) * (1.0 + scale) + shift


def _row_spec(tm, d):
    return pl.BlockSpec((tm, d), lambda i: (i, 0))


def _const_spec(shape):
    return pl.BlockSpec(shape, lambda i: tuple(0 for _ in shape))


def _batch_spec(d, tiles_per_batch):
    return pl.BlockSpec((None, 1, d), lambda i: (i // tiles_per_batch, 0, 0))


def _ada_kernel(c_ref, w_ref, b_ref, o_ref):
    c = c_ref[...]
    ca = c * jax.nn.sigmoid(c)
    o_ref[...] = jnp.dot(ca, w_ref[...], preferred_element_type=F32) + b_ref[...]


def _ada_proj(c_pad, w, b, tn):
    nl, d, e = w.shape
    return pl.pallas_call(
        _ada_kernel,
        grid=(nl, e // tn),
        in_specs=[
            pl.BlockSpec((8, d), lambda l, j: (0, 0)),
            pl.BlockSpec((None, d, tn), lambda l, j: (l, 0, j)),
            pl.BlockSpec((None, 1, tn), lambda l, j: (l, 0, j)),
        ],
        out_specs=pl.BlockSpec((None, 8, tn), lambda l, j: (l, 0, j)),
        out_shape=jax.ShapeDtypeStruct((nl, 8, e), F32),
        compiler_params=_params("parallel", "parallel"),
        name="ada_proj",
    )(c_pad, w, b)


def _modmm_kernel(x_ref, g_ref, sh_ref, sc_ref, w_ref, o_ref):
    h = _modnorm(x_ref[...], g_ref[...], sh_ref[...], sc_ref[...]).astype(BF16)
    o_ref[...] = jnp.dot(h, w_ref[...], preferred_element_type=F32).astype(o_ref.dtype)


def _modmm(x, g, shift, scale, w, seq, out_dtype):
    n, d = x.shape
    e = w.shape[1]
    tm = ROW_TILE
    tpb = seq // tm
    return pl.pallas_call(
        _modmm_kernel,
        grid=(n // tm,),
        in_specs=[_row_spec(tm, d), _const_spec((1, d)), _batch_spec(d, tpb), _batch_spec(d, tpb),
                  _const_spec((d, e))],
        out_specs=_row_spec(tm, e),
        out_shape=jax.ShapeDtypeStruct((n, e), out_dtype),
        compiler_params=_params("parallel"),
        name="modnorm_matmul",
    )(x, g, shift, scale, w)


def _ffn_up_kernel(x_ref, g_ref, sh_ref, sc_ref, wg_ref, wu_ref, o_ref):
    h = _modnorm(x_ref[...], g_ref[...], sh_ref[...], sc_ref[...]).astype(BF16)
    f = o_ref.shape[1]
    for c in range(0, f, FF_CHUNK):
        gt = jnp.dot(h, wg_ref[:, c:c + FF_CHUNK], preferred_element_type=F32)
        up = jnp.dot(h, wu_ref[:, c:c + FF_CHUNK], preferred_element_type=F32)
        o_ref[:, c:c + FF_CHUNK] = (gt * jax.nn.sigmoid(gt) * up).astype(o_ref.dtype)


def _ffn_up(x, g, shift, scale, wg, wu, seq):
    n, d = x.shape
    f = wg.shape[1]
    tm = ROW_TILE
    tpb = seq // tm
    return pl.pallas_call(
        _ffn_up_kernel,
        grid=(n // tm,),
        in_specs=[_row_spec(tm, d), _const_spec((1, d)), _batch_spec(d, tpb), _batch_spec(d, tpb),
                  _const_spec((d, f)), _const_spec((d, f))],
        out_specs=_row_spec(tm, f),
        out_shape=jax.ShapeDtypeStruct((n, f), BF16),
        compiler_params=_params("parallel"),
        name="ffn_up",
    )(x, g, shift, scale, wg, wu)


def _mm_post_kernel(a_ref, w_ref, x_ref, gp_ref, gate_ref, o_ref, *, glu):
    y = jnp.dot(a_ref[...], w_ref[...], preferred_element_type=F32)
    if glu:
        d = o_ref.shape[1]
        y = y[:, :d] * jax.nn.sigmoid(y[:, d:])
    o_ref[...] = x_ref[...] + gate_ref[...] * _rms(y, gp_ref[...])


def _mm_post(a, w, x, g_post, gate, seq, glu):
    n, k = a.shape
    d = x.shape[1]
    e = w.shape[1]
    tm = ROW_TILE
    tpb = seq // tm
    return pl.pallas_call(
        functools.partial(_mm_post_kernel, glu=glu),
        grid=(n // tm,),
        in_specs=[_row_spec(tm, k), _const_spec((k, e)), _row_spec(tm, d), _const_spec((1, d)),
                  _batch_spec(d, tpb)],
        out_specs=_row_spec(tm, d),
        out_shape=jax.ShapeDtypeStruct((n, d), F32),
        compiler_params=_params("parallel"),
        name="matmul_post",
    )(a, w, x, g_post, gate)


def _s5_kernel(u_ref, b_ref, c_ref, d_ref, apow_ref, tpow_ref, o_ref, st_ref, carry_ref, *, n_steps):
    j = pl.program_id(2)
    t = u_ref.shape[0]
    half = carry_ref.shape[1] // 2
    pad = st_ref.shape[0] - t

    @pl.when(j == 0)
    def _():
        carry_ref[...] = jnp.zeros_like(carry_ref)
        st_ref[0:pad, :] = jnp.zeros((pad, 2 * half), F32)

    u = u_ref[...]
    st_ref[pad:, :] = jnp.dot(u.astype(BF16), b_ref[...], preferred_element_type=F32)
    for k in range(n_steps):
        sft = 1 << k
        ar = apow_ref[k:k + 1, :half]
        ai = apow_ref[k:k + 1, half:]
        cur_re = st_ref[pad:, :half]
        cur_im = st_ref[pad:, half:]
        sh_re = st_ref[pad - sft:pad - sft + t, :half]
        sh_im = st_ref[pad - sft:pad - sft + t, half:]
        st_ref[pad:, :half] = cur_re + (ar * sh_re - ai * sh_im)
        st_ref[pad:, half:] = cur_im + (ar * sh_im + ai * sh_re)
    cr = carry_ref[:, :half]
    ci = carry_ref[:, half:]
    pr = tpow_ref[:, :half]
    pi = tpow_ref[:, half:]
    s_re = st_ref[pad:, :half] + (pr * cr - pi * ci)
    s_im = st_ref[pad:, half:] + (pr * ci + pi * cr)
    carry_ref[:, :half] = s_re[t - 1:t, :]
    carry_ref[:, half:] = s_im[t - 1:t, :]
    st = jnp.concatenate([s_re, s_im], axis=1).astype(BF16)
    y = jnp.dot(st, c_ref[...], preferred_element_type=F32)
    o_ref[...] = jax.nn.gelu(y + d_ref[...] * u).astype(o_ref.dtype)


def _s5_scan(u, bblk, cblk, d_skip, apow, tpow, bsz, seq):
    n, width = u.shape
    cb = SSM_CH_BLOCK
    t = SSM_CHUNK
    nst = bblk.shape[2]
    n_steps = apow.shape[1]
    chunks = seq // t
    return pl.pallas_call(
        functools.partial(_s5_kernel, n_steps=n_steps),
        grid=(bsz, width // cb, chunks),
        in_specs=[
            pl.BlockSpec((t, cb), lambda b, c, j: (b * chunks + j, c)),
            pl.BlockSpec((None, cb, nst), lambda b, c, j: (c, 0, 0)),
            pl.BlockSpec((None, nst, cb), lambda b, c, j: (c, 0, 0)),
            pl.BlockSpec((1, cb), lambda b, c, j: (0, c)),
            pl.BlockSpec((None, n_steps, nst), lambda b, c, j: (c, 0, 0)),
            pl.BlockSpec((None, t, nst), lambda b, c, j: (c, 0, 0)),
        ],
        out_specs=pl.BlockSpec((t, cb), lambda b, c, j: (b * chunks + j, c)),
        out_shape=jax.ShapeDtypeStruct((n, width), BF16),
        scratch_shapes=[pltpu.VMEM((t // 2 + t, nst), F32), pltpu.VMEM((1, nst), F32)],
        compiler_params=_params("parallel", "parallel", "arbitrary"),
        name="s5_scan",
    )(u, bblk, cblk, d_skip, apow, tpow)


def _s5_tables(log_step, a_re, a_im, b_re, b_im, c_re, c_im):
    g, p = a_re.shape
    gpb = SSM_CH_BLOCK // SSM_GROUP
    nblk = g // gpb
    delta = jnp.exp(log_step)[:, None]
    mag = jnp.exp(a_re * delta)
    ang = a_im * delta
    lb_re, lb_im = mag * jnp.cos(ang), mag * jnp.sin(ang)
    den = a_re * a_re + a_im * a_im
    nr, ni = lb_re - 1.0, lb_im
    coef_re = (nr * a_re + ni * a_im) / den
    coef_im = (ni * a_re - nr * a_im) / den
    bb_re = coef_re[..., None] * b_re - coef_im[..., None] * b_im
    bb_im = coef_re[..., None] * b_im + coef_im[..., None] * b_re
    eye = jnp.eye(gpb, dtype=F32)

    def blockdiag_in(m):
        m = m.reshape(nblk, gpb, p, SSM_GROUP)
        return jnp.einsum('ngpc,gh->ngchp', m, eye).reshape(nblk, gpb * SSM_GROUP, gpb * p)

    def blockdiag_out(m):
        m = m.reshape(nblk, gpb, SSM_GROUP, p)
        return jnp.einsum('ngcp,gh->ngphc', m, eye).reshape(nblk, gpb * p, gpb * SSM_GROUP)

    bblk = jnp.concatenate([blockdiag_in(bb_re), blockdiag_in(bb_im)], axis=2).astype(BF16)
    cblk = jnp.concatenate([blockdiag_out(c_re), blockdiag_out(-c_im)], axis=1).astype(BF16)
    lr = lb_re.reshape(nblk, gpb * p)
    li = lb_im.reshape(nblk, gpb * p)
    n_steps = int(math.log2(SSM_CHUNK))
    sq = [(lr, li)]
    for _ in range(n_steps - 1):
        r, i = sq[-1]
        sq.append((r * r - i * i, 2.0 * r * i))
    apow = jnp.stack([jnp.concatenate([r, i], axis=1) for r, i in sq], axis=1)
    pr, pi = lr[:, None, :], li[:, None, :]
    for k in range(n_steps):
        r, i = sq[k]
        r, i = r[:, None, :], i[:, None, :]
        pr, pi = (jnp.concatenate([pr, pr * r - pi * i], axis=1),
                  jnp.concatenate([pi, pr * i + pi * r], axis=1))
    tpow = jnp.concatenate([pr, pi], axis=2)
    return bblk, cblk, apow, tpow


def _s5_in_kernel(x_ref, g_ref, sh_ref, sc_ref, wt_ref, o_ref):
    h = _modnorm(x_ref[...], g_ref[...], sh_ref[...], sc_ref[...]).astype(BF16)
    ut = lax.dot_general(wt_ref[...], h, (((1,), (1,)), ((), ())), preferred_element_type=F32)
    for k in range(o_ref.shape[1]):
        o_ref[:, k, :] = ut[:, k * LANES:(k + 1) * LANES]


def _s5_in(x, g, shift, scale, wt, seq):
    n, d = x.shape
    width = wt.shape[0]
    tm = S5_ROW_TILE
    tpb = seq // tm
    return pl.pallas_call(
        _s5_in_kernel,
        grid=(n // tm,),
        in_specs=[_row_spec(tm, d), _const_spec((1, d)), _batch_spec(d, tpb), _batch_spec(d, tpb),
                  _const_spec((width, d))],
        out_specs=pl.BlockSpec((width, tm // LANES, LANES), lambda i: (0, i, 0)),
        out_shape=jax.ShapeDtypeStruct((width, n // LANES, LANES), F32),
        compiler_params=_params("parallel"),
        name="s5_in",
    )(x, g, shift, scale, wt)


def _s5_conv_kernel(u_ref, k_ref, s_ref, cc_ref, a_ref, d_ref, o_ref, acc_ref, hs_ref, *, chunks_per_seq):
    gc, nc, t = u_ref.shape
    ny = gc * t
    pad = hs_ref.shape[0] - nc
    causal = lax.broadcasted_iota(jnp.int32, (t, t), 1) >= lax.broadcasted_iota(jnp.int32, (t, t), 0)

    def rhs_rows(ci):
        tiles = []
        for c in range(gc):
            lag = jnp.broadcast_to(k_ref[ci, c:c + 1, :], (t, t))
            toep = pltpu.roll(lag, 0, 1, stride=1, stride_axis=0)
            tiles.append(jnp.where(causal, toep, 0.0).astype(BF16))
        tiles.append(s_ref[ci])
        return jnp.concatenate(tiles, axis=1)

    for c0 in range(0, gc, 2):
        lhs = jnp.concatenate([u_ref[c0], u_ref[c0 + 1]], axis=1).astype(BF16)
        rhs = jnp.concatenate([rhs_rows(c0), rhs_rows(c0 + 1)], axis=0)
        part = jnp.dot(lhs, rhs, preferred_element_type=F32)
        if c0 == 0:
            acc_ref[...] = part
        else:
            acc_ref[...] += part

    hs_ref[0:pad, :] = jnp.zeros((pad, LANES), F32)
    hs_ref[pad:, :] = acc_ref[:, ny:]
    jl = lax.broadcasted_iota(jnp.int32, (nc, LANES), 0) & (chunks_per_seq - 1)
    for k in range(a_ref.shape[0] // 2):
        sft = 1 << k
        cur = hs_ref[pad:, :]
        sh = jnp.where(jl >= sft, hs_ref[pad - sft:pad - sft + nc, :], 0.0)
        hs_ref[pad:, :] = (cur + sh * a_ref[2 * k:2 * k + 1, :]
                           + pltpu.roll(sh, LANES // 2, 1) * a_ref[2 * k + 1:2 * k + 2, :])
    h_in = jnp.where(jl >= 1, hs_ref[pad - 1:pad - 1 + nc, :], 0.0).astype(BF16)
    y = acc_ref[:, :ny] + jnp.dot(h_in, cc_ref[...], preferred_element_type=F32)
    for c in range(gc):
        o_ref[c] = jax.nn.gelu(y[:, c * t:(c + 1) * t] + d_ref[c:c + 1, :] * u_ref[c])


def _s5_conv(u3, ktab, stab, cctab, atab, dtab, seq):
    width, nc, t = u3.shape
    gc = SSM_GROUP
    chunks_per_seq = seq // t
    blk = lambda *tail: pl.BlockSpec((None,) + tail, lambda g: (g,) + tuple(0 for _ in tail))
    return pl.pallas_call(
        functools.partial(_s5_conv_kernel, chunks_per_seq=chunks_per_seq),
        grid=(width // gc,),
        in_specs=[pl.BlockSpec((gc, nc, t), lambda g: (g, 0, 0)),
                  blk(gc, gc, t), blk(gc, t, LANES), blk(LANES, gc * t), blk(atab.shape[1], LANES), blk(gc, t)],
        out_specs=pl.BlockSpec((gc, nc, t), lambda g: (g, 0, 0)),
        out_shape=jax.ShapeDtypeStruct((width, nc, t), F32),
        scratch_shapes=[pltpu.VMEM((nc, gc * t + LANES), F32), pltpu.VMEM((chunks_per_seq + nc, LANES), F32)],
        compiler_params=_params("parallel"),
        name="s5_conv",
    )(u3, ktab, stab, cctab, atab, dtab)


def _s5_out_kernel(y_ref, w_ref, x_ref, gp_ref, gate_ref, o_ref, m_ref):
    d = o_ref.shape[1]
    yt = jnp.concatenate([y_ref[:, k, :] for k in range(y_ref.shape[1])], axis=1)
    y = yt.T.astype(BF16)
    for c in range(0, d, FF_CHUNK):
        za = jnp.dot(y, w_ref[:, c:c + FF_CHUNK], preferred_element_type=F32)
        zb = jnp.dot(y, w_ref[:, d + c:d + c + FF_CHUNK], preferred_element_type=F32)
        m_ref[:, c:c + FF_CHUNK] = za * jax.nn.sigmoid(zb)
    o_ref[...] = x_ref[...] + gate_ref[...] * _rms(m_ref[...], gp_ref[...])


def _s5_out(y3, w, x, g_post, gate, seq):
    n, d = x.shape
    width = y3.shape[0]
    tm = S5_ROW_TILE
    tpb = seq // tm
    return pl.pallas_call(
        _s5_out_kernel,
        grid=(n // tm,),
        in_specs=[pl.BlockSpec((width, tm // LANES, LANES), lambda i: (0, i, 0)), _const_spec(w.shape),
                  _row_spec(tm, d), _const_spec((1, d)), _batch_spec(d, tpb)],
        out_specs=_row_spec(tm, d),
        out_shape=jax.ShapeDtypeStruct((n, d), F32),
        scratch_shapes=[pltpu.VMEM((tm, d), F32)],
        compiler_params=_params("parallel"),
        name="s5_out",
    )(y3, w, x, g_post, gate)


def _s5_conv_tables(log_step, a_re, a_im, b_re, b_im, c_re, c_im, d_skip, chunks_per_seq):
    g, p = a_re.shape
    t = LANES
    gc = SSM_GROUP
    hi = lax.Precision.HIGHEST
    delta = jnp.exp(log_step)[:, None]
    mag = jnp.exp(a_re * delta)
    ang = a_im * delta
    lb_re, lb_im = mag * jnp.cos(ang), mag * jnp.sin(ang)
    den = a_re * a_re + a_im * a_im
    nr, ni = lb_re - 1.0, lb_im
    coef_re = (nr * a_re + ni * a_im) / den
    coef_im = (ni * a_re - nr * a_im) / den
    bb_re = coef_re[..., None] * b_re - coef_im[..., None] * b_im
    bb_im = coef_re[..., None] * b_im + coef_im[..., None] * b_re

    def cmul(xr, xi, yr, yi):
        return xr * yr - xi * yi, xr * yi + xi * yr

    sq = [(lb_re, lb_im)]
    for _ in range(int(math.log2(t))):
        sq.append(cmul(*sq[-1], *sq[-1]))
    pr, pi = jnp.ones((1, g, p), F32), jnp.zeros((1, g, p), F32)
    for k in range(int(math.log2(t))):
        nr_, ni_ = cmul(pr, pi, sq[k][0][None], sq[k][1][None])
        pr, pi = jnp.concatenate([pr, nr_], axis=0), jnp.concatenate([pi, ni_], axis=0)
    lt_re, lt_im = sq[-1]
    p1r = jnp.concatenate([pr[1:], lt_re[None]], axis=0)
    p1i = jnp.concatenate([pi[1:], lt_im[None]], axis=0)

    w_re = jnp.einsum('gcp,gpd->gcdp', c_re, bb_re) - jnp.einsum('gcp,gpd->gcdp', c_im, bb_im)
    w_im = jnp.einsum('gcp,gpd->gcdp', c_re, bb_im) + jnp.einsum('gcp,gpd->gcdp', c_im, bb_re)
    ktab = (jnp.einsum('gcdp,kgp->gdck', w_re, pr, precision=hi)
            - jnp.einsum('gcdp,kgp->gdck', w_im, pi, precision=hi))
    rr = pr[::-1].transpose(1, 0, 2)[:, None]
    ri = pi[::-1].transpose(1, 0, 2)[:, None]
    br = bb_re.transpose(0, 2, 1)[:, :, None, :]
    bi = bb_im.transpose(0, 2, 1)[:, :, None, :]
    s_re, s_im = cmul(rr, ri, br, bi)
    stab = jnp.concatenate([s_re, s_im], axis=-1).astype(BF16)
    cr = c_re.transpose(0, 2, 1)[:, :, :, None]
    ci = c_im.transpose(0, 2, 1)[:, :, :, None]
    qr = p1r.transpose(1, 2, 0)[:, :, None, :]
    qi = p1i.transpose(1, 2, 0)[:, :, None, :]
    m_re, m_im = cmul(cr, ci, qr, qi)
    cctab = jnp.concatenate([m_re, -m_im], axis=1).reshape(g, 2 * p, gc * t).astype(BF16)
    rows = []
    ar, ai = lt_re, lt_im
    for _ in range(int(math.log2(chunks_per_seq))):
        rows += [jnp.concatenate([ar, ar], axis=1), jnp.concatenate([-ai, ai], axis=1)]
        ar, ai = cmul(ar, ai, ar, ai)
    atab = jnp.stack(rows, axis=1)
    dtab = jnp.broadcast_to(d_skip.reshape(g, gc, 1), (g, gc, t))
    return ktab, stab, cctab, atab, dtab


def _rope(x, cos, sin):
    return x * cos + pltpu.roll(x, LANES // 2, 1) * sin


def _kv_kernel(x_ref, g_ref, sh_ref, sc_ref, wc_ref, wr_ref, gl_ref, wk_ref, wv_ref, cos_ref, sin_ref,
               kn_ref, kr_ref, v_ref):
    hs = _modnorm(x_ref[...], g_ref[...], sh_ref[...], sc_ref[...]).astype(BF16)
    c = jnp.dot(hs, wc_ref[...], preferred_element_type=F32)
    ckv = _rms(c, gl_ref[...]).astype(BF16)
    kn_ref[...] = jnp.dot(ckv, wk_ref[...], preferred_element_type=F32).astype(BF16)
    v_ref[...] = jnp.dot(ckv, wv_ref[...], preferred_element_type=F32).astype(BF16)
    r = jnp.dot(hs, wr_ref[...], preferred_element_type=F32)
    kr_ref[...] = _rope(r, cos_ref[...], sin_ref[...]).astype(BF16)


def _mla_kv(x, g, shift, scale, wc, wr, gl, wk, wv, cos, sin, seq):
    n, d = x.shape
    tm = ROW_TILE
    tpb = seq // tm
    hk = wk.shape[1]
    hv = wv.shape[1]
    return pl.pallas_call(
        _kv_kernel,
        grid=(n // tm,),
        in_specs=[_row_spec(tm, d), _const_spec((1, d)), _batch_spec(d, tpb), _batch_spec(d, tpb),
                  _const_spec(wc.shape), _const_spec(wr.shape), _const_spec(gl.shape),
                  _const_spec(wk.shape), _const_spec(wv.shape), _row_spec(tm, LANES), _row_spec(tm, LANES)],
        out_specs=[_row_spec(tm, hk), _row_spec(tm, LANES), _row_spec(tm, hv)],
        out_shape=[jax.ShapeDtypeStruct((n, hk), BF16), jax.ShapeDtypeStruct((n, LANES), BF16),
                   jax.ShapeDtypeStruct((n, hv), BF16)],
        compiler_params=_params("parallel"),
        name="mla_kv",
    )(x, g, shift, scale, wc, wr, gl, wk, wv, cos, sin)


def _q_kernel(x_ref, g_ref, sh_ref, sc_ref, wd_ref, gq_ref, wn_ref, wr_ref, cos_ref, sin_ref,
              qn_ref, qr_ref):
    h = _modnorm(x_ref[...], g_ref[...], sh_ref[...], sc_ref[...]).astype(BF16)
    ql = jnp.dot(h, wd_ref[...], preferred_element_type=F32)
    qn = _rms(ql, gq_ref[...]).astype(BF16)
    qn_ref[...] = (jnp.dot(qn, wn_ref[...], preferred_element_type=F32) * Q_SCALE).astype(BF16)
    r = jnp.dot(qn, wr_ref[...], preferred_element_type=F32)
    cos = cos_ref[...] * Q_SCALE
    sin = sin_ref[...] * Q_SCALE
    for hd in range(qr_ref.shape[1] // LANES):
        sl = slice(hd * LANES, (hd + 1) * LANES)
        qr_ref[:, sl] = _rope(r[:, sl], cos, sin).astype(BF16)


def _mla_q(x, g, shift, scale, wd, gq, wn, wr, cos, sin, seq):
    n, d = x.shape
    tm = ROW_TILE
    tpb = seq // tm
    e = wn.shape[1]
    return pl.pallas_call(
        _q_kernel,
        grid=(n // tm,),
        in_specs=[_row_spec(tm, d), _const_spec((1, d)), _batch_spec(d, tpb), _batch_spec(d, tpb),
                  _const_spec(wd.shape), _const_spec(gq.shape), _const_spec(wn.shape), _const_spec(wr.shape),
                  _row_spec(tm, LANES), _row_spec(tm, LANES)],
        out_specs=[_row_spec(tm, e), _row_spec(tm, e)],
        out_shape=[jax.ShapeDtypeStruct((n, e), BF16), jax.ShapeDtypeStruct((n, e), BF16)],
        compiler_params=_params("parallel"),
        name="mla_q",
    )(x, g, shift, scale, wd, gq, wn, wr, cos, sin)


def _attn_kernel(qn_ref, qr_ref, kn_ref, kr_ref, v_ref, o_ref):
    i = pl.program_id(2)
    tq = qn_ref.shape[0]
    tk = tq
    q = jnp.concatenate([qn_ref[...], qr_ref[...]], axis=1)

    def step(j, carry, diagonal):
        m, l, acc = carry
        start = pl.multiple_of(j * tk, tk)
        k = jnp.concatenate([kn_ref[pl.ds(start, tk), :], kr_ref[pl.ds(start, tk), :]], axis=1)
        s = lax.dot_general(q, k, (((1,), (1,)), ((), ())), preferred_element_type=F32)
        if diagonal:
            row = lax.broadcasted_iota(jnp.int32, (tq, tk), 0)
            col = lax.broadcasted_iota(jnp.int32, (tq, tk), 1)
            s = jnp.where(col <= row, s, -jnp.inf)
        m_new = jnp.maximum(m, jnp.max(s, axis=-1, keepdims=True))
        alpha = jnp.exp2(m - m_new)
        p = jnp.exp2(s - m_new)
        l = alpha * l + jnp.sum(p, axis=-1, keepdims=True)
        acc = alpha * acc + jnp.dot(p.astype(BF16), v_ref[pl.ds(start, tk), :], preferred_element_type=F32)
        return m_new, l, acc

    init = (jnp.full((tq, 1), -jnp.inf, F32), jnp.zeros((tq, 1), F32), jnp.zeros((tq, v_ref.shape[1]), F32))
    def pair(jj, c):
        return step(2 * jj + 1, step(2 * jj, c, False), False)

    carry = lax.fori_loop(0, i // 2, pair, init)
    carry = lax.fori_loop(0, i % 2, lambda _, c: step(i - 1, c, False), carry)
    _, l, acc = step(i, carry, diagonal=True)
    o_ref[...] = (acc / l).astype(o_ref.dtype)


def _attention(qn, qr, kn, kr, v, bsz, seq):
    tq = ATTN_TILE
    hd = LANES
    return pl.pallas_call(
        _attn_kernel,
        grid=(bsz, N_HEADS, seq // tq),
        in_specs=[
            pl.BlockSpec((None, tq, hd), lambda b, h, i: (b, i, h)),
            pl.BlockSpec((None, tq, hd), lambda b, h, i: (b, i, h)),
            pl.BlockSpec((None, seq, hd), lambda b, h, i: (b, 0, h)),
            pl.BlockSpec((None, seq, hd), lambda b, h, i: (b, 0, 0)),
            pl.BlockSpec((None, seq, hd), lambda b, h, i: (b, 0, h)),
        ],
        out_specs=pl.BlockSpec((None, tq, hd), lambda b, h, i: (b, i, h)),
        out_shape=jax.ShapeDtypeStruct((bsz, seq, N_HEADS * hd), BF16),
        compiler_params=_params("parallel", "parallel", "parallel"),
        name="attention",
    )(qn, qr, kn, kr, v)


def _route_kernel(x_ref, g_ref, sh_ref, sc_ref, wr_ref, h_ref, info_ref, cnt_ref, carry_ref):
    @pl.when(pl.program_id(0) == 0)
    def _():
        carry_ref[...] = jnp.zeros_like(carry_ref)

    tm = x_ref.shape[0]
    h = _modnorm(x_ref[...], g_ref[...], sh_ref[...], sc_ref[...])
    _store_row_tiles(h_ref, h)
    logits = jnp.dot(h, wr_ref[...], preferred_element_type=F32, precision=lax.Precision.HIGHEST)
    lane = lax.broadcasted_iota(jnp.int32, (tm, LANES), 1).astype(F32)
    neg = -jnp.inf
    lg = jnp.where(lane < N_EXPERTS, logits, neg)
    l1 = jnp.max(lg, axis=-1, keepdims=True)
    e1 = jnp.min(jnp.where(lg == l1, lane, float(LANES)), axis=-1, keepdims=True)
    lg2 = jnp.where(lane == e1, neg, lg)
    l2 = jnp.max(lg2, axis=-1, keepdims=True)
    e2 = jnp.min(jnp.where(lg2 == l2, lane, float(LANES)), axis=-1, keepdims=True)
    tt = jnp.exp(l2 - l1)
    g1 = 1.0 / (1.0 + tt)
    g2 = tt / (1.0 + tt)
    oh1 = lane == e1
    oh2 = lane == e2
    oh = jnp.where(oh1 | oh2, 1.0, 0.0)
    tri = (lax.broadcasted_iota(jnp.int32, (tm, tm), 0) > lax.broadcasted_iota(jnp.int32, (tm, tm), 1))
    cum = jnp.dot(jnp.where(tri, 1.0, 0.0).astype(BF16), oh.astype(BF16),
                  preferred_element_type=F32) + carry_ref[...]
    r1 = jnp.sum(jnp.where(oh1, cum, 0.0), axis=-1, keepdims=True)
    r2 = jnp.sum(jnp.where(oh2, cum, 0.0), axis=-1, keepdims=True)
    carry_ref[...] = carry_ref[...] + jnp.sum(oh, axis=0, keepdims=True)
    cnt_ref[...] = carry_ref[...]
    info = jnp.where(lane == 0, e1,
           jnp.where(lane == 1, e2,
           jnp.where(lane == 2, g1,
           jnp.where(lane == 3, g2,
           jnp.where(lane == 4, r1, r2)))))
    info_ref[...] = info


def _route(x, g, shift, scale, wr_pad, seq):
    n, d = x.shape
    tm = ROUTE_TILE
    tpb = seq // tm
    return pl.pallas_call(
        _route_kernel,
        grid=(n // tm,),
        in_specs=[_row_spec(tm, d), _const_spec((1, d)), _batch_spec(d, tpb), _batch_spec(d, tpb),
                  _const_spec((d, LANES))],
        out_specs=[_row_tile_spec(tm, d), _row_spec(tm, LANES), _const_spec((1, LANES))],
        out_shape=[jax.ShapeDtypeStruct((n, d // LANES, LANES), F32), jax.ShapeDtypeStruct((n, LANES), F32),
                   jax.ShapeDtypeStruct((1, LANES), F32)],
        scratch_shapes=[pltpu.VMEM((1, LANES), F32)],
        compiler_params=_params("arbitrary"),
        name="moe_route",
    )(x, g, shift, scale, wr_pad)


def _row_tile_spec(tm, d):
    return pl.BlockSpec((tm, d // LANES, LANES), lambda i, *_: (i, 0, 0))


def _store_row_tiles(ref, val):
    for s in range(ref.shape[1]):
        ref[:, s, :] = val[:, s * LANES:(s + 1) * LANES]


def _load_row_tiles(ref):
    return jnp.concatenate([ref[:, s, :] for s in range(ref.shape[1])], axis=1)


def _start_row_gather(src_hbm, idx_ref, dst_ref, sem):
    def issue(r, c):
        pltpu.make_async_copy(src_hbm.at[pl.ds(idx_ref[0, r], 1)], dst_ref.at[pl.ds(r, 1)], sem).start()
        return c

    lax.fori_loop(0, dst_ref.shape[0], issue, 0, unroll=8)


def _wait_row_gather(src_hbm, dst_ref, sem):
    pltpu.make_async_copy(src_hbm.at[pl.ds(0, dst_ref.shape[0])], dst_ref, sem).wait()


def _expert_up_kernel(te_ref, idx_cur_ref, idx_nxt_ref, h_hbm, wg_ref, wu_ref, o_ref, xbuf_ref, sem):
    i = pl.program_id(0)
    slot = i % 2

    @pl.when(i == 0)
    def _():
        _start_row_gather(h_hbm, idx_cur_ref, xbuf_ref.at[0], sem.at[0])

    @pl.when(i + 1 < pl.num_programs(0))
    def _():
        _start_row_gather(h_hbm, idx_nxt_ref, xbuf_ref.at[1 - slot], sem.at[1 - slot])

    _wait_row_gather(h_hbm, xbuf_ref.at[slot], sem.at[slot])
    h = _load_row_tiles(xbuf_ref.at[slot]).astype(BF16)
    f = o_ref.shape[1]
    for c in range(0, f, FF_CHUNK):
        gt = jnp.dot(h, wg_ref[:, c:c + FF_CHUNK], preferred_element_type=F32)
        up = jnp.dot(h, wu_ref[:, c:c + FF_CHUNK], preferred_element_type=F32)
        o_ref[:, c:c + FF_CHUNK] = (gt * jax.nn.sigmoid(gt) * up).astype(o_ref.dtype)


def _expert_up(h3, inv, wg, wu, tile_expert):
    rows = inv.shape[0]
    _, ns, _ = h3.shape
    d = wg.shape[1]
    f = wg.shape[2]
    tm = EXPERT_TILE
    nt = rows // tm
    idx3 = inv.reshape(nt, 1, tm)
    return pl.pallas_call(
        _expert_up_kernel,
        grid_spec=pltpu.PrefetchScalarGridSpec(
            num_scalar_prefetch=1,
            grid=(nt,),
            in_specs=[pl.BlockSpec((None, 1, tm), lambda i, te: (i, 0, 0), memory_space=pltpu.SMEM),
                      pl.BlockSpec((None, 1, tm), lambda i, te: (jnp.minimum(i + 1, nt - 1), 0, 0),
                                   memory_space=pltpu.SMEM),
                      pl.BlockSpec(memory_space=pl.ANY),
                      pl.BlockSpec((None, d, f), lambda i, te: (te[i], 0, 0)),
                      pl.BlockSpec((None, d, f), lambda i, te: (te[i], 0, 0))],
            out_specs=pl.BlockSpec((tm, f), lambda i, te: (i, 0)),
            scratch_shapes=[pltpu.VMEM((2, tm, ns, LANES), F32), pltpu.SemaphoreType.DMA((2,))],
        ),
        out_shape=jax.ShapeDtypeStruct((rows, f), BF16),
        compiler_params=_params("arbitrary"),
        name="expert_up",
    )(tile_expert, idx3, idx3, h3, wg, wu)


def _expert_down_kernel(te_ref, a_ref, w_ref, o_ref):
    _store_row_tiles(o_ref, jnp.dot(a_ref[...], w_ref[...], preferred_element_type=F32))


def _expert_down(a, wd, tile_expert):
    rows, f = a.shape
    d = wd.shape[2]
    tm = EXPERT_TILE
    return pl.pallas_call(
        _expert_down_kernel,
        grid_spec=pltpu.PrefetchScalarGridSpec(
            num_scalar_prefetch=1,
            grid=(rows // tm,),
            in_specs=[pl.BlockSpec((tm, f), lambda i, te: (i, 0)),
                      pl.BlockSpec((None, f, d), lambda i, te: (te[i], 0, 0))],
            out_specs=_row_tile_spec(tm, d),
        ),
        out_shape=jax.ShapeDtypeStruct((rows, d // LANES, LANES), F32),
        compiler_params=_params("arbitrary"),
        name="expert_down",
    )(tile_expert, a, wd)


def _combine_kernel(d1_ref, d2_ref, y_hbm, info_ref, x_ref, gp_ref, gate_ref, o_ref, buf_ref, sem):
    _start_row_gather(y_hbm, d1_ref, buf_ref.at[0], sem.at[0])
    _start_row_gather(y_hbm, d2_ref, buf_ref.at[1], sem.at[1])
    info = info_ref[...]
    g1 = info[:, 2:3]
    g2 = info[:, 3:4]
    _wait_row_gather(y_hbm, buf_ref.at[0], sem.at[0])
    _wait_row_gather(y_hbm, buf_ref.at[1], sem.at[1])
    y = g1 * _load_row_tiles(buf_ref.at[0]) + g2 * _load_row_tiles(buf_ref.at[1])
    o_ref[...] = x_ref[...] + gate_ref[...] * _rms(y, gp_ref[...])


def _combine(ybuf3, dest1, dest2, info, x, g_post, gate, seq):
    n, d = x.shape
    tm = GATHER_TILE
    tpb = seq // tm
    idx_spec = pl.BlockSpec((None, 1, tm), lambda i: (i, 0, 0), memory_space=pltpu.SMEM)
    return pl.pallas_call(
        _combine_kernel,
        grid=(n // tm,),
        in_specs=[idx_spec, idx_spec, pl.BlockSpec(memory_space=pl.ANY), _row_spec(tm, LANES),
                  _row_spec(tm, d), _const_spec((1, d)), _batch_spec(d, tpb)],
        out_specs=_row_spec(tm, d),
        out_shape=jax.ShapeDtypeStruct((n, d), F32),
        scratch_shapes=[pltpu.VMEM((2, tm, d // LANES, LANES), F32), pltpu.SemaphoreType.DMA((2,))],
        compiler_params=_params("arbitrary"),
        name="moe_combine",
    )(dest1.reshape(n // tm, 1, tm), dest2.reshape(n // tm, 1, tm), ybuf3, info, x, g_post, gate)


def _moe(x, g_pre, shift, scale, router_w, wg, wu, wd, g_post, gate, seq):
    n, d = x.shape
    wr_pad = jnp.pad(router_w, ((0, 0), (0, LANES - N_EXPERTS)))
    h3, info, cnt = _route(x, g_pre, shift, scale, wr_pad, seq)
    e1 = info[:, 0].astype(jnp.int32)
    e2 = info[:, 1].astype(jnp.int32)
    r1 = info[:, 4].astype(jnp.int32)
    r2 = info[:, 5].astype(jnp.int32)
    counts = cnt[0, :N_EXPERTS].astype(jnp.int32)
    te = EXPERT_TILE
    padded = (counts + te - 1) // te * te
    pend = jnp.cumsum(padded)
    pstart = pend - padded
    dest1 = pstart[e1] + r1
    dest2 = pstart[e2] + r2
    rows = 2 * n + N_EXPERTS * te
    tok = jnp.arange(n, dtype=jnp.int32)
    inv = jnp.zeros((rows,), jnp.int32).at[dest1].set(tok).at[dest2].set(tok)
    tile_start = jnp.arange(rows // te, dtype=jnp.int32) * te
    tile_expert = jnp.minimum(jnp.sum((tile_start[:, None] >= pend[None, :]).astype(jnp.int32), axis=1),
                              N_EXPERTS - 1)
    a = _expert_up(h3, inv, wg, wu, tile_expert)
    ybuf3 = _expert_down(a, wd, tile_expert)
    return _combine(ybuf3, dest1, dest2, info, x, g_post, gate, seq)


def _rope_lanes(w):
    half = QK_ROPE_DIM // 2
    z = jnp.zeros(w.shape[:-1] + (LANES // 2 - half,), w.dtype)
    return jnp.concatenate([w[..., :half], z, w[..., half:], z], axis=-1)


def kernel(x, c, positions, ada_w, ada_b, norm_mix_pre, norm_mix_post, norm_ffn_pre, norm_ffn_post, ssm_w_in, ssm_log_step, ssm_a_re, ssm_a_im, ssm_b_re, ssm_b_im, ssm_c_re, ssm_c_im, ssm_d, ssm_w_out, kv_ada_w, kv_ada_b, kv_norm, mla_w_dkv, mla_kv_norm, mla_w_ukv, mla_w_dq, mla_q_norm, mla_w_uq, mla_w_o, ffn_w_gu, ffn_w_down, moe_router, moe_w_gu, moe_w_down):
    bsz, seq, d = x.shape
    depth = ada_w.shape[0]
    n_a = ssm_w_in.shape[0]
    n = bsz * seq
    d_ff = ffn_w_down.shape[1]
    e_ff = moe_w_down.shape[2]

    c_pad = jnp.pad(c, ((0, 8 - bsz), (0, 0)))
    ada = _ada_proj(c_pad, ada_w, ada_b[:, None, :], 2048)[:, :bsz]
    kv_ada = _ada_proj(c_pad, kv_ada_w[None], kv_ada_b[None, None, :], 2048)[0, :bsz]

    def vec(a):
        return a[:, None, :]

    def gain(gv):
        return gv[None, :]

    inv_freq = ROPE_THETA ** (-jnp.arange(0, QK_ROPE_DIM, 2, dtype=F32) / QK_ROPE_DIM)
    ang = positions.astype(F32)[..., None] * inv_freq
    cos = jnp.cos(ang).reshape(n, -1)
    sin = jnp.sin(ang).reshape(n, -1)
    zpad = jnp.zeros_like(cos)
    cos_t = jnp.concatenate([cos, zpad, cos, zpad], axis=-1)
    sin_t = jnp.concatenate([-sin, zpad, sin, zpad], axis=-1)

    xs = x.reshape(n, d)
    kn = kr = v = None
    for i in range(depth):
        sh_m, sc_m, g_m, sh_f, sc_f, g_f = [vec(a) for a in jnp.split(ada[i], 6, axis=-1)]
        if i < n_a:
            u3 = _s5_in(xs, gain(norm_mix_pre[i]), sh_m, sc_m, ssm_w_in[i].T.astype(BF16), seq)
            tabs = _s5_conv_tables(ssm_log_step[i], ssm_a_re[i], ssm_a_im[i], ssm_b_re[i], ssm_b_im[i],
                                   ssm_c_re[i], ssm_c_im[i], ssm_d[i], seq // LANES)
            y3 = _s5_conv(u3, *tabs, seq)
            xs = _s5_out(y3, ssm_w_out[i].astype(BF16), xs, gain(norm_mix_post[i]), g_m, seq)
        else:
            if i == n_a:
                kv_sh, kv_sc = [vec(a) for a in jnp.split(kv_ada, 2, axis=-1)]
                w_ukv = mla_w_ukv.reshape(KV_LORA_RANK, N_HEADS, QK_NOPE_DIM + V_HEAD_DIM)
                wk = w_ukv[:, :, :QK_NOPE_DIM].reshape(KV_LORA_RANK, -1).astype(BF16)
                wv = w_ukv[:, :, QK_NOPE_DIM:].reshape(KV_LORA_RANK, -1).astype(BF16)
                wc = mla_w_dkv[:, :KV_LORA_RANK].astype(BF16)
                wr = _rope_lanes(mla_w_dkv[:, KV_LORA_RANK:]).astype(BF16)
                kn, kr, v = _mla_kv(xs, gain(kv_norm), kv_sh, kv_sc, wc, wr, gain(mla_kv_norm), wk, wv,
                                    cos_t, sin_t, seq)
                kn = kn.reshape(bsz, seq, -1)
                kr = kr.reshape(bsz, seq, -1)
                v = v.reshape(bsz, seq, -1)
            j = i - n_a
            w_uq = mla_w_uq[j].reshape(-1, N_HEADS, QK_NOPE_DIM + QK_ROPE_DIM)
            wn = w_uq[:, :, :QK_NOPE_DIM].reshape(w_uq.shape[0], -1).astype(BF16)
            wqr = _rope_lanes(w_uq[:, :, QK_NOPE_DIM:]).reshape(w_uq.shape[0], -1).astype(BF16)
            qn, qr = _mla_q(xs, gain(norm_mix_pre[i]), sh_m, sc_m, mla_w_dq[j].astype(BF16),
                            gain(mla_q_norm[j]), wn, wqr, cos_t, sin_t, seq)
            o = _attention(qn.reshape(bsz, seq, -1), qr.reshape(bsz, seq, -1), kn, kr, v, bsz, seq)
            xs = _mm_post(o.reshape(n, -1), mla_w_o[j].astype(BF16), xs, gain(norm_mix_post[i]), g_m, seq,
                          glu=False)
        if i % 2 == 0:
            w_gu = ffn_w_gu[i // 2]
            a = _ffn_up(xs, gain(norm_ffn_pre[i]), sh_f, sc_f, w_gu[:, :d_ff].astype(BF16),
                        w_gu[:, d_ff:].astype(BF16), seq)
            xs = _mm_post(a, ffn_w_down[i // 2].astype(BF16), xs, gain(norm_ffn_post[i]), g_f, seq, glu=False)
        else:
            w_gu = moe_w_gu[i // 2]
            xs = _moe(xs, gain(norm_ffn_pre[i]), sh_f, sc_f, moe_router[i // 2],
                      w_gu[:, :, :e_ff].astype(BF16), w_gu[:, :, e_ff:].astype(BF16),
                      moe_w_down[i // 2].astype(BF16), gain(norm_ffn_post[i]), g_f, seq)
    return xs.reshape(bsz, seq, d)
```

```python
import functools
import math

import jax
import jax.numpy as jnp
from jax import lax
from jax.experimental import pallas as pl
from jax.experimental.pallas import tpu as pltpu

F32 = jnp.float32
BF16 = jnp.bfloat16

NORM_EPS = 1e-6
LANES = 128
SSM_GROUP = 16
SSM_STATE = 64
N_HEADS = 8
QK_NOPE_DIM = 128
QK_ROPE_DIM = 64
V_HEAD_DIM = 128
KV_LORA_RANK = 256
ROPE_THETA = 10000.0
N_EXPERTS = 8
SOFTMAX_SCALE = (QK_NOPE_DIM + QK_ROPE_DIM) ** -0.5
Q_SCALE = SOFTMAX_SCALE * math.log2(math.e)

ROW_TILE = 512
S5_ROW_TILE = 1024
SSM_CHUNK = 128
SSM_CH_BLOCK = 128
ATTN_TILE = 512
ROUTE_TILE = 256
EXPERT_TILE = 256
GATHER_TILE = 256
FF_CHUNK = 256


def _params(*sem):
    return pltpu.CompilerParams(dimension_semantics=sem)


def _rms(x, g):
    return x * lax.rsqrt(jnp.mean(x * x, axis=-1, keepdims=True) + NORM_EPS) * g


def _modnorm(x, g, shift, scale):
    return _rms(x, g) * (1.0 + scale) + shift


def _row_spec(tm, d):
    return pl.BlockSpec((tm, d), lambda i: (i, 0))


def _const_spec(shape):
    return pl.BlockSpec(shape, lambda i: tuple(0 for _ in shape))


def _batch_spec(d, tiles_per_batch):
    return pl.BlockSpec((None, 1, d), lambda i: (i // tiles_per_batch, 0, 0))


def _ada_kernel(c_ref, w_ref, b_ref, o_ref):
    c = c_ref[...]
    ca = c * jax.nn.sigmoid(c)
    o_ref[...] = jnp.dot(ca, w_ref[...], preferred_element_type=F32) + b_ref[...]


def _ada_proj(c_pad, w, b, tn):
    nl, d, e = w.shape
    return pl.pallas_call(
        _ada_kernel,
        grid=(nl, e // tn),
        in_specs=[
            pl.BlockSpec((8, d), lambda l, j: (0, 0)),
            pl.BlockSpec((None, d, tn), lambda l, j: (l, 0, j)),
            pl.BlockSpec((None, 1, tn), lambda l, j: (l, 0, j)),
        ],
        out_specs=pl.BlockSpec((None, 8, tn), lambda l, j: (l, 0, j)),
        out_shape=jax.ShapeDtypeStruct((nl, 8, e), F32),
        compiler_params=_params("parallel", "parallel"),
        name="ada_proj",
    )(c_pad, w, b)


def _modmm_kernel(x_ref, g_ref, sh_ref, sc_ref, w_ref, o_ref):
    h = _modnorm(x_ref[...], g_ref[...], sh_ref[...], sc_ref[...]).astype(BF16)
    o_ref[...] = jnp.dot(h, w_ref[...], preferred_element_type=F32).astype(o_ref.dtype)


def _modmm(x, g, shift, scale, w, seq, out_dtype):
    n, d = x.shape
    e = w.shape[1]
    tm = ROW_TILE
    tpb = seq // tm
    return pl.pallas_call(
        _modmm_kernel,
        grid=(n // tm,),
        in_specs=[_row_spec(tm, d), _const_spec((1, d)), _batch_spec(d, tpb), _batch_spec(d, tpb),
                  _const_spec((d, e))],
        out_specs=_row_spec(tm, e),
        out_shape=jax.ShapeDtypeStruct((n, e), out_dtype),
        compiler_params=_params("parallel"),
        name="modnorm_matmul",
    )(x, g, shift, scale, w)


def _ffn_up_kernel(x_ref, g_ref, sh_ref, sc_ref, wg_ref, wu_ref, o_ref):
    h = _modnorm(x_ref[...], g_ref[...], sh_ref[...], sc_ref[...]).astype(BF16)
    f = o_ref.shape[1]
    for c in range(0, f, FF_CHUNK):
        gt = jnp.dot(h, wg_ref[:, c:c + FF_CHUNK], preferred_element_type=F32)
        up = jnp.dot(h, wu_ref[:, c:c + FF_CHUNK], preferred_element_type=F32)
        o_ref[:, c:c + FF_CHUNK] = (gt * jax.nn.sigmoid(gt) * up).astype(o_ref.dtype)


def _ffn_up(x, g, shift, scale, wg, wu, seq):
    n, d = x.shape
    f = wg.shape[1]
    tm = ROW_TILE
    tpb = seq // tm
    return pl.pallas_call(
        _ffn_up_kernel,
        grid=(n // tm,),
        in_specs=[_row_spec(tm, d), _const_spec((1, d)), _batch_spec(d, tpb), _batch_spec(d, tpb),
                  _const_spec((d, f)), _const_spec((d, f))],
        out_specs=_row_spec(tm, f),
        out_shape=jax.ShapeDtypeStruct((n, f), BF16),
        compiler_params=_params("parallel"),
        name="ffn_up",
    )(x, g, shift, scale, wg, wu)


def _mm_post_kernel(a_ref, w_ref, x_ref, gp_ref, gate_ref, o_ref, *, glu):
    y = jnp.dot(a_ref[...], w_ref[...], preferred_element_type=F32)
    if glu:
        d = o_ref.shape[1]
        y = y[:, :d] * jax.nn.sigmoid(y[:, d:])
    o_ref[...] = x_ref[...] + gate_ref[...] * _rms(y, gp_ref[...])


def _mm_post(a, w, x, g_post, gate, seq, glu):
    n, k = a.shape
    d = x.shape[1]
    e = w.shape[1]
    tm = ROW_TILE
    tpb = seq // tm
    return pl.pallas_call(
        functools.partial(_mm_post_kernel, glu=glu),
        grid=(n // tm,),
        in_specs=[_row_spec(tm, k), _const_spec((k, e)), _row_spec(tm, d), _const_spec((1, d)),
                  _batch_spec(d, tpb)],
        out_specs=_row_spec(tm, d),
        out_shape=jax.ShapeDtypeStruct((n, d), F32),
        compiler_params=_params("parallel"),
        name="matmul_post",
    )(a, w, x, g_post, gate)


def _s5_kernel(u_ref, b_ref, c_ref, d_ref, apow_ref, tpow_ref, o_ref, st_ref, carry_ref, *, n_steps):
    j = pl.program_id(2)
    t = u_ref.shape[0]
    half = carry_ref.shape[1] // 2
    pad = st_ref.shape[0] - t

    @pl.when(j == 0)
    def _():
        carry_ref[...] = jnp.zeros_like(carry_ref)
        st_ref[0:pad, :] = jnp.zeros((pad, 2 * half), F32)

    u = u_ref[...]
    st_ref[pad:, :] = jnp.dot(u.astype(BF16), b_ref[...], preferred_element_type=F32)
    for k in range(n_steps):
        sft = 1 << k
        ar = apow_ref[k:k + 1, :half]
        ai = apow_ref[k:k + 1, half:]
        cur_re = st_ref[pad:, :half]
        cur_im = st_ref[pad:, half:]
        sh_re = st_ref[pad - sft:pad - sft + t, :half]
        sh_im = st_ref[pad - sft:pad - sft + t, half:]
        st_ref[pad:, :half] = cur_re + (ar * sh_re - ai * sh_im)
        st_ref[pad:, half:] = cur_im + (ar * sh_im + ai * sh_re)
    cr = carry_ref[:, :half]
    ci = carry_ref[:, half:]
    pr = tpow_ref[:, :half]
    pi = tpow_ref[:, half:]
    s_re = st_ref[pad:, :half] + (pr * cr - pi * ci)
    s_im = st_ref[pad:, half:] + (pr * ci + pi * cr)
    carry_ref[:, :half] = s_re[t - 1:t, :]
    carry_ref[:, half:] = s_im[t - 1:t, :]
    st = jnp.concatenate([s_re, s_im], axis=1).astype(BF16)
    y = jnp.dot(st, c_ref[...], preferred_element_type=F32)
    o_ref[...] = jax.nn.gelu(y + d_ref[...] * u).astype(o_ref.dtype)


def _s5_scan(u, bblk, cblk, d_skip, apow, tpow, bsz, seq):
    n, width = u.shape
    cb = SSM_CH_BLOCK
    t = SSM_CHUNK
    nst = bblk.shape[2]
    n_steps = apow.shape[1]
    chunks = seq // t
    return pl.pallas_call(
        functools.partial(_s5_kernel, n_steps=n_steps),
        grid=(bsz, width // cb, chunks),
        in_specs=[
            pl.BlockSpec((t, cb), lambda b, c, j: (b * chunks + j, c)),
            pl.BlockSpec((None, cb, nst), lambda b, c, j: (c, 0, 0)),
            pl.BlockSpec((None, nst, cb), lambda b, c, j: (c, 0, 0)),
            pl.BlockSpec((1, cb), lambda b, c, j: (0, c)),
            pl.BlockSpec((None, n_steps, nst), lambda b, c, j: (c, 0, 0)),
            pl.BlockSpec((None, t, nst), lambda b, c, j: (c, 0, 0)),
        ],
        out_specs=pl.BlockSpec((t, cb), lambda b, c, j: (b * chunks + j, c)),
        out_shape=jax.ShapeDtypeStruct((n, width), BF16),
        scratch_shapes=[pltpu.VMEM((t // 2 + t, nst), F32), pltpu.VMEM((1, nst), F32)],
        compiler_params=_params("parallel", "parallel", "arbitrary"),
        name="s5_scan",
    )(u, bblk, cblk, d_skip, apow, tpow)


def _s5_tables(log_step, a_re, a_im, b_re, b_im, c_re, c_im):
    g, p = a_re.shape
    gpb = SSM_CH_BLOCK // SSM_GROUP
    nblk = g // gpb
    delta = jnp.exp(log_step)[:, None]
    mag = jnp.exp(a_re * delta)
    ang = a_im * delta
    lb_re, lb_im = mag * jnp.cos(ang), mag * jnp.sin(ang)
    den = a_re * a_re + a_im * a_im
    nr, ni = lb_re - 1.0, lb_im
    coef_re = (nr * a_re + ni * a_im) / den
    coef_im = (ni * a_re - nr * a_im) / den
    bb_re = coef_re[..., None] * b_re - coef_im[..., None] * b_im
    bb_im = coef_re[..., None] * b_im + coef_im[..., None] * b_re
    eye = jnp.eye(gpb, dtype=F32)

    def blockdiag_in(m):
        m = m.reshape(nblk, gpb, p, SSM_GROUP)
        return jnp.einsum('ngpc,gh->ngchp', m, eye).reshape(nblk, gpb * SSM_GROUP, gpb * p)

    def blockdiag_out(m):
        m = m.reshape(nblk, gpb, SSM_GROUP, p)
        return jnp.einsum('ngcp,gh->ngphc', m, eye).reshape(nblk, gpb * p, gpb * SSM_GROUP)

    bblk = jnp.concatenate([blockdiag_in(bb_re), blockdiag_in(bb_im)], axis=2).astype(BF16)
    cblk = jnp.concatenate([blockdiag_out(c_re), blockdiag_out(-c_im)], axis=1).astype(BF16)
    lr = lb_re.reshape(nblk, gpb * p)
    li = lb_im.reshape(nblk, gpb * p)
    n_steps = int(math.log2(SSM_CHUNK))
    sq = [(lr, li)]
    for _ in range(n_steps - 1):
        r, i = sq[-1]
        sq.append((r * r - i * i, 2.0 * r * i))
    apow = jnp.stack([jnp.concatenate([r, i], axis=1) for r, i in sq], axis=1)
    pr, pi = lr[:, None, :], li[:, None, :]
    for k in range(n_steps):
        r, i = sq[k]
        r, i = r[:, None, :], i[:, None, :]
        pr, pi = (jnp.concatenate([pr, pr * r - pi * i], axis=1),
                  jnp.concatenate([pi, pr * i + pi * r], axis=1))
    tpow = jnp.concatenate([pr, pi], axis=2)
    return bblk, cblk, apow, tpow


def _s5_in_kernel(x_ref, g_ref, sh_ref, sc_ref, wt_ref, o_ref):
    h = _modnorm(x_ref[...], g_ref[...], sh_ref[...], sc_ref[...]).astype(BF16)
    ut = lax.dot_general(wt_ref[...], h, (((1,), (1,)), ((), ())), preferred_element_type=F32)
    for k in range(o_ref.shape[1]):
        o_ref[:, k, :] = ut[:, k * LANES:(k + 1) * LANES]


def _s5_in(x, g, shift, scale, wt, seq):
    n, d = x.shape
    width = wt.shape[0]
    tm = S5_ROW_TILE
    tpb = seq // tm
    return pl.pallas_call(
        _s5_in_kernel,
        grid=(n // tm,),
        in_specs=[_row_spec(tm, d), _const_spec((1, d)), _batch_spec(d, tpb), _batch_spec(d, tpb),
                  _const_spec((width, d))],
        out_specs=pl.BlockSpec((width, tm // LANES, LANES), lambda i: (0, i, 0)),
        out_shape=jax.ShapeDtypeStruct((width, n // LANES, LANES), F32),
        compiler_params=_params("parallel"),
        name="s5_in",
    )(x, g, shift, scale, wt)


def _s5_conv_kernel(u_ref, k_ref, s_ref, cc_ref, a_ref, d_ref, o_ref, acc_ref, hs_ref, *, chunks_per_seq):
    gc, nc, t = u_ref.shape
    ny = gc * t
    pad = hs_ref.shape[0] - nc
    causal = lax.broadcasted_iota(jnp.int32, (t, t), 1) >= lax.broadcasted_iota(jnp.int32, (t, t), 0)

    def rhs_rows(ci):
        tiles = []
        for c in range(gc):
            lag = jnp.broadcast_to(k_ref[ci, c:c + 1, :], (t, t))
            toep = pltpu.roll(lag, 0, 1, stride=1, stride_axis=0)
            tiles.append(jnp.where(causal, toep, 0.0).astype(BF16))
        tiles.append(s_ref[ci])
        return jnp.concatenate(tiles, axis=1)

    for c0 in range(0, gc, 2):
        lhs = jnp.concatenate([u_ref[c0], u_ref[c0 + 1]], axis=1).astype(BF16)
        rhs = jnp.concatenate([rhs_rows(c0), rhs_rows(c0 + 1)], axis=0)
        part = jnp.dot(lhs, rhs, preferred_element_type=F32)
        if c0 == 0:
            acc_ref[...] = part
        else:
            acc_ref[...] += part

    hs_ref[0:pad, :] = jnp.zeros((pad, LANES), F32)
    hs_ref[pad:, :] = acc_ref[:, ny:]
    jl = lax.broadcasted_iota(jnp.int32, (nc, LANES), 0) & (chunks_per_seq - 1)
    for k in range(a_ref.shape[0] // 2):
        sft = 1 << k
        cur = hs_ref[pad:, :]
        sh = jnp.where(jl >= sft, hs_ref[pad - sft:pad - sft + nc, :], 0.0)
        hs_ref[pad:, :] = (cur + sh * a_ref[2 * k:2 * k + 1, :]
                           + pltpu.roll(sh, LANES // 2, 1) * a_ref[2 * k + 1:2 * k + 2, :])
    h_in = jnp.where(jl >= 1, hs_ref[pad - 1:pad - 1 + nc, :], 0.0).astype(BF16)
    y = acc_ref[:, :ny] + jnp.dot(h_in, cc_ref[...], preferred_element_type=F32)
    for c in range(gc):
        o_ref[c] = jax.nn.gelu(y[:, c * t:(c + 1) * t] + d_ref[c:c + 1, :] * u_ref[c])


def _s5_conv(u3, ktab, stab, cctab, atab, dtab, seq):
    width, nc, t = u3.shape
    gc = SSM_GROUP
    chunks_per_seq = seq // t
    blk = lambda *tail: pl.BlockSpec((None,) + tail, lambda g: (g,) + tuple(0 for _ in tail))
    return pl.pallas_call(
        functools.partial(_s5_conv_kernel, chunks_per_seq=chunks_per_seq),
        grid=(width // gc,),
        in_specs=[pl.BlockSpec((gc, nc, t), lambda g: (g, 0, 0)),
                  blk(gc, gc, t), blk(gc, t, LANES), blk(LANES, gc * t), blk(atab.shape[1], LANES), blk(gc, t)],
        out_specs=pl.BlockSpec((gc, nc, t), lambda g: (g, 0, 0)),
        out_shape=jax.ShapeDtypeStruct((width, nc, t), F32),
        scratch_shapes=[pltpu.VMEM((nc, gc * t + LANES), F32), pltpu.VMEM((chunks_per_seq + nc, LANES), F32)],
        compiler_params=_params("parallel"),
        name="s5_conv",
    )(u3, ktab, stab, cctab, atab, dtab)


def _s5_out_kernel(y_ref, w_ref, x_ref, gp_ref, gate_ref, o_ref, m_ref):
    d = o_ref.shape[1]
    yt = jnp.concatenate([y_ref[:, k, :] for k in range(y_ref.shape[1])], axis=1)
    y = yt.T.astype(BF16)
    for c in range(0, d, FF_CHUNK):
        za = jnp.dot(y, w_ref[:, c:c + FF_CHUNK], preferred_element_type=F32)
        zb = jnp.dot(y, w_ref[:, d + c:d + c + FF_CHUNK], preferred_element_type=F32)
        m_ref[:, c:c + FF_CHUNK] = za * jax.nn.sigmoid(zb)
    o_ref[...] = x_ref[...] + gate_ref[...] * _rms(m_ref[...], gp_ref[...])


def _s5_out(y3, w, x, g_post, gate, seq):
    n, d = x.shape
    width = y3.shape[0]
    tm = S5_ROW_TILE
    tpb = seq // tm
    return pl.pallas_call(
        _s5_out_kernel,
        grid=(n // tm,),
        in_specs=[pl.BlockSpec((width, tm // LANES, LANES), lambda i: (0, i, 0)), _const_spec(w.shape),
                  _row_spec(tm, d), _const_spec((1, d)), _batch_spec(d, tpb)],
        out_specs=_row_spec(tm, d),
        out_shape=jax.ShapeDtypeStruct((n, d), F32),
        scratch_shapes=[pltpu.VMEM((tm, d), F32)],
        compiler_params=_params("parallel"),
        name="s5_out",
    )(y3, w, x, g_post, gate)


def _s5_conv_tables(log_step, a_re, a_im, b_re, b_im, c_re, c_im, d_skip, chunks_per_seq):
    g, p = a_re.shape
    t = LANES
    gc = SSM_GROUP
    hi = lax.Precision.HIGHEST
    delta = jnp.exp(log_step)[:, None]
    mag = jnp.exp(a_re * delta)
    ang = a_im * delta
    lb_re, lb_im = mag * jnp.cos(ang), mag * jnp.sin(ang)
    den = a_re * a_re + a_im * a_im
    nr, ni = lb_re - 1.0, lb_im
    coef_re = (nr * a_re + ni * a_im) / den
    coef_im = (ni * a_re - nr * a_im) / den
    bb_re = coef_re[..., None] * b_re - coef_im[..., None] * b_im
    bb_im = coef_re[..., None] * b_im + coef_im[..., None] * b_re

    def cmul(xr, xi, yr, yi):
        return xr * yr - xi * yi, xr * yi + xi * yr

    sq = [(lb_re, lb_im)]
    for _ in range(int(math.log2(t))):
        sq.append(cmul(*sq[-1], *sq[-1]))
    pr, pi = jnp.ones((1, g, p), F32), jnp.zeros((1, g, p), F32)
    for k in range(int(math.log2(t))):
        nr_, ni_ = cmul(pr, pi, sq[k][0][None], sq[k][1][None])
        pr, pi = jnp.concatenate([pr, nr_], axis=0), jnp.concatenate([pi, ni_], axis=0)
    lt_re, lt_im = sq[-1]
    p1r = jnp.concatenate([pr[1:], lt_re[None]], axis=0)
    p1i = jnp.concatenate([pi[1:], lt_im[None]], axis=0)

    w_re = jnp.einsum('gcp,gpd->gcdp', c_re, bb_re) - jnp.einsum('gcp,gpd->gcdp', c_im, bb_im)
    w_im = jnp.einsum('gcp,gpd->gcdp', c_re, bb_im) + jnp.einsum('gcp,gpd->gcdp', c_im, bb_re)
    ktab = (jnp.einsum('gcdp,kgp->gdck', w_re, pr, precision=hi)
            - jnp.einsum('gcdp,kgp->gdck', w_im, pi, precision=hi))
    rr = pr[::-1].transpose(1, 0, 2)[:, None]
    ri = pi[::-1].transpose(1, 0, 2)[:, None]
    br = bb_re.transpose(0, 2, 1)[:, :, None, :]
    bi = bb_im.transpose(0, 2, 1)[:, :, None, :]
    s_re, s_im = cmul(rr, ri, br, bi)
    stab = jnp.concatenate([s_re, s_im], axis=-1).astype(BF16)
    cr = c_re.transpose(0, 2, 1)[:, :, :, None]
    ci = c_im.transpose(0, 2, 1)[:, :, :, None]
    qr = p1r.transpose(1, 2, 0)[:, :, None, :]
    qi = p1i.transpose(1, 2, 0)[:, :, None, :]
    m_re, m_im = cmul(cr, ci, qr, qi)
    cctab = jnp.concatenate([m_re, -m_im], axis=1).reshape(g, 2 * p, gc * t).astype(BF16)
    rows = []
    ar, ai = lt_re, lt_im
    for _ in range(int(math.log2(chunks_per_seq))):
        rows += [jnp.concatenate([ar, ar], axis=1), jnp.concatenate([-ai, ai], axis=1)]
        ar, ai = cmul(ar, ai, ar, ai)
    atab = jnp.stack(rows, axis=1)
    dtab = jnp.broadcast_to(d_skip.reshape(g, gc, 1), (g, gc, t))
    return ktab, stab, cctab, atab, dtab


def _rope(x, cos, sin):
    return x * cos + pltpu.roll(x, LANES // 2, 1) * sin


def _kv_kernel(x_ref, g_ref, sh_ref, sc_ref, wc_ref, wr_ref, gl_ref, wk_ref, wv_ref, cos_ref, sin_ref,
               kn_ref, kr_ref, v_ref):
    hs = _modnorm(x_ref[...], g_ref[...], sh_ref[...], sc_ref[...]).astype(BF16)
    c = jnp.dot(hs, wc_ref[...], preferred_element_type=F32)
    ckv = _rms(c, gl_ref[...]).astype(BF16)
    kn_ref[...] = jnp.dot(ckv, wk_ref[...], preferred_element_type=F32).astype(BF16)
    v_ref[...] = jnp.dot(ckv, wv_ref[...], preferred_element_type=F32).astype(BF16)
    r = jnp.dot(hs, wr_ref[...], preferred_element_type=F32)
    kr_ref[...] = _rope(r, cos_ref[...], sin_ref[...]).astype(BF16)


def _mla_kv(x, g, shift, scale, wc, wr, gl, wk, wv, cos, sin, seq):
    n, d = x.shape
    tm = ROW_TILE
    tpb = seq // tm
    hk = wk.shape[1]
    hv = wv.shape[1]
    return pl.pallas_call(
        _kv_kernel,
        grid=(n // tm,),
        in_specs=[_row_spec(tm, d), _const_spec((1, d)), _batch_spec(d, tpb), _batch_spec(d, tpb),
                  _const_spec(wc.shape), _const_spec(wr.shape), _const_spec(gl.shape),
                  _const_spec(wk.shape), _const_spec(wv.shape), _row_spec(tm, LANES), _row_spec(tm, LANES)],
        out_specs=[_row_spec(tm, hk), _row_spec(tm, LANES), _row_spec(tm, hv)],
        out_shape=[jax.ShapeDtypeStruct((n, hk), BF16), jax.ShapeDtypeStruct((n, LANES), BF16),
                   jax.ShapeDtypeStruct((n, hv), BF16)],
        compiler_params=_params("parallel"),
        name="mla_kv",
    )(x, g, shift, scale, wc, wr, gl, wk, wv, cos, sin)


def _q_kernel(x_ref, g_ref, sh_ref, sc_ref, wd_ref, gq_ref, wn_ref, wr_ref, cos_ref, sin_ref,
              qn_ref, qr_ref):
    h = _modnorm(x_ref[...], g_ref[...], sh_ref[...], sc_ref[...]).astype(BF16)
    ql = jnp.dot(h, wd_ref[...], preferred_element_type=F32)
    qn = _rms(ql, gq_ref[...]).astype(BF16)
    qn_ref[...] = (jnp.dot(qn, wn_ref[...], preferred_element_type=F32) * Q_SCALE).astype(BF16)
    r = jnp.dot(qn, wr_ref[...], preferred_element_type=F32)
    cos = cos_ref[...] * Q_SCALE
    sin = sin_ref[...] * Q_SCALE
    for hd in range(qr_ref.shape[1] // LANES):
        sl = slice(hd * LANES, (hd + 1) * LANES)
        qr_ref[:, sl] = _rope(r[:, sl], cos, sin).astype(BF16)


def _mla_q(x, g, shift, scale, wd, gq, wn, wr, cos, sin, seq):
    n, d = x.shape
    tm = ROW_TILE
    tpb = seq // tm
    e = wn.shape[1]
    return pl.pallas_call(
        _q_kernel,
        grid=(n // tm,),
        in_specs=[_row_spec(tm, d), _const_spec((1, d)), _batch_spec(d, tpb), _batch_spec(d, tpb),
                  _const_spec(wd.shape), _const_spec(gq.shape), _const_spec(wn.shape), _const_spec(wr.shape),
                  _row_spec(tm, LANES), _row_spec(tm, LANES)],
        out_specs=[_row_spec(tm, e), _row_spec(tm, e)],
        out_shape=[jax.ShapeDtypeStruct((n, e), BF16), jax.ShapeDtypeStruct((n, e), BF16)],
        compiler_params=_params("parallel"),
        name="mla_q",
    )(x, g, shift, scale, wd, gq, wn, wr, cos, sin)


def _attn_kernel(qn_ref, qr_ref, kn_ref, kr_ref, v_ref, o_ref):
    i = pl.program_id(2)
    tq = qn_ref.shape[0]
    tk = tq
    q = jnp.concatenate([qn_ref[...], qr_ref[...]], axis=1)

    def step(j, carry, diagonal):
        m, l, acc = carry
        start = pl.multiple_of(j * tk, tk)
        k = jnp.concatenate([kn_ref[pl.ds(start, tk), :], kr_ref[pl.ds(start, tk), :]], axis=1)
        s = lax.dot_general(q, k, (((1,), (1,)), ((), ())), preferred_element_type=F32)
        if diagonal:
            row = lax.broadcasted_iota(jnp.int32, (tq, tk), 0)
            col = lax.broadcasted_iota(jnp.int32, (tq, tk), 1)
            s = jnp.where(col <= row, s, -jnp.inf)
        m_new = jnp.maximum(m, jnp.max(s, axis=-1, keepdims=True))
        alpha = jnp.exp2(m - m_new)
        p = jnp.exp2(s - m_new)
        l = alpha * l + jnp.sum(p, axis=-1, keepdims=True)
        acc = alpha * acc + jnp.dot(p.astype(BF16), v_ref[pl.ds(start, tk), :], preferred_element_type=F32)
        return m_new, l, acc

    init = (jnp.full((tq, 1), -jnp.inf, F32), jnp.zeros((tq, 1), F32), jnp.zeros((tq, v_ref.shape[1]), F32))
    def pair(jj, c):
        return step(2 * jj + 1, step(2 * jj, c, False), False)

    carry = lax.fori_loop(0, i // 2, pair, init)
    carry = lax.fori_loop(0, i % 2, lambda _, c: step(i - 1, c, False), carry)
    _, l, acc = step(i, carry, diagonal=True)
    o_ref[...] = (acc / l).astype(o_ref.dtype)


def _attention(qn, qr, kn, kr, v, bsz, seq):
    tq = ATTN_TILE
    hd = LANES
    return pl.pallas_call(
        _attn_kernel,
        grid=(bsz, N_HEADS, seq // tq),
        in_specs=[
            pl.BlockSpec((None, tq, hd), lambda b, h, i: (b, i, h)),
            pl.BlockSpec((None, tq, hd), lambda b, h, i: (b, i, h)),
            pl.BlockSpec((None, seq, hd), lambda b, h, i: (b, 0, h)),
            pl.BlockSpec((None, seq, hd), lambda b, h, i: (b, 0, 0)),
            pl.BlockSpec((None, seq, hd), lambda b, h, i: (b, 0, h)),
        ],
        out_specs=pl.BlockSpec((None, tq, hd), lambda b, h, i: (b, i, h)),
        out_shape=jax.ShapeDtypeStruct((bsz, seq, N_HEADS * hd), BF16),
        compiler_params=_params("parallel", "parallel", "parallel"),
        name="attention",
    )(qn, qr, kn, kr, v)


def _route_kernel(x_ref, g_ref, sh_ref, sc_ref, wr_ref, h_ref, info_ref, cnt_ref, carry_ref):
    @pl.when(pl.program_id(0) == 0)
    def _():
        carry_ref[...] = jnp.zeros_like(carry_ref)

    tm = x_ref.shape[0]
    h = _modnorm(x_ref[...], g_ref[...], sh_ref[...], sc_ref[...])
    _store_row_tiles(h_ref, h)
    logits = jnp.dot(h, wr_ref[...], preferred_element_type=F32, precision=lax.Precision.HIGHEST)
    lane = lax.broadcasted_iota(jnp.int32, (tm, LANES), 1).astype(F32)
    neg = -jnp.inf
    lg = jnp.where(lane < N_EXPERTS, logits, neg)
    l1 = jnp.max(lg, axis=-1, keepdims=True)
    e1 = jnp.min(jnp.where(lg == l1, lane, float(LANES)), axis=-1, keepdims=True)
    lg2 = jnp.where(lane == e1, neg, lg)
    l2 = jnp.max(lg2, axis=-1, keepdims=True)
    e2 = jnp.min(jnp.where(lg2 == l2, lane, float(LANES)), axis=-1, keepdims=True)
    tt = jnp.exp(l2 - l1)
    g1 = 1.0 / (1.0 + tt)
    g2 = tt / (1.0 + tt)
    oh1 = lane == e1
    oh2 = lane == e2
    oh = jnp.where(oh1 | oh2, 1.0, 0.0)
    tri = (lax.broadcasted_iota(jnp.int32, (tm, tm), 0) > lax.broadcasted_iota(jnp.int32, (tm, tm), 1))
    cum = jnp.dot(jnp.where(tri, 1.0, 0.0).astype(BF16), oh.astype(BF16),
                  preferred_element_type=F32) + carry_ref[...]
    r1 = jnp.sum(jnp.where(oh1, cum, 0.0), axis=-1, keepdims=True)
    r2 = jnp.sum(jnp.where(oh2, cum, 0.0), axis=-1, keepdims=True)
    carry_ref[...] = carry_ref[...] + jnp.sum(oh, axis=0, keepdims=True)
    cnt_ref[...] = carry_ref[...]
    info = jnp.where(lane == 0, e1,
           jnp.where(lane == 1, e2,
           jnp.where(lane == 2, g1,
           jnp.where(lane == 3, g2,
           jnp.where(lane == 4, r1, r2)))))
    info_ref[...] = info


def _route(x, g, shift, scale, wr_pad, seq):
    n, d = x.shape
    tm = ROUTE_TILE
    tpb = seq // tm
    return pl.pallas_call(
        _route_kernel,
        grid=(n // tm,),
        in_specs=[_row_spec(tm, d), _const_spec((1, d)), _batch_spec(d, tpb), _batch_spec(d, tpb),
                  _const_spec((d, LANES))],
        out_specs=[_row_tile_spec(tm, d), _row_spec(tm, LANES), _const_spec((1, LANES))],
        out_shape=[jax.ShapeDtypeStruct((n, d // LANES, LANES), F32), jax.ShapeDtypeStruct((n, LANES), F32),
                   jax.ShapeDtypeStruct((1, LANES), F32)],
        scratch_shapes=[pltpu.VMEM((1, LANES), F32)],
        compiler_params=_params("arbitrary"),
        name="moe_route",
    )(x, g, shift, scale, wr_pad)


def _row_tile_spec(tm, d):
    return pl.BlockSpec((tm, d // LANES, LANES), lambda i, *_: (i, 0, 0))


def _store_row_tiles(ref, val):
    for s in range(ref.shape[1]):
        ref[:, s, :] = val[:, s * LANES:(s + 1) * LANES]


def _load_row_tiles(ref):
    return jnp.concatenate([ref[:, s, :] for s in range(ref.shape[1])], axis=1)


DMA_ISSUE_UNROLL = 8


def _issue_row(src_hbm, idx_ref, dst_ref, sem, r, priority):
    pltpu.make_async_copy(src_hbm.at[pl.ds(idx_ref[0, r], 1)], dst_ref.at[pl.ds(r, 1)], sem).start(
        priority=priority)


def _issue_rows(src_hbm, idx_ref, dst_ref, sem, lo, hi):
    for r in range(lo, hi):
        _issue_row(src_hbm, idx_ref, dst_ref, sem, r, r % 2)


def _start_row_gather(src_hbm, idx_ref, dst_ref, sem):
    def issue(blk, c):
        for u in range(DMA_ISSUE_UNROLL):
            _issue_row(src_hbm, idx_ref, dst_ref, sem, blk * DMA_ISSUE_UNROLL + u, u % 2)
        return c

    lax.fori_loop(0, dst_ref.shape[0] // DMA_ISSUE_UNROLL, issue, 0)


def _wait_row_gather(src_hbm, dst_ref, sem):
    pltpu.make_async_copy(src_hbm.at[pl.ds(0, dst_ref.shape[0])], dst_ref, sem).wait()


def _expert_up_kernel(te_ref, idx_cur_ref, idx_nxt_ref, h_hbm, wg_ref, wu_ref, o_ref, xbuf_ref, sem):
    i = pl.program_id(0)
    slot = i % 2
    nxt = 1 - slot
    tm = xbuf_ref.shape[1]

    @pl.when(i == 0)
    def _():
        _start_row_gather(h_hbm, idx_cur_ref, xbuf_ref.at[0], sem.at[0])

    _wait_row_gather(h_hbm, xbuf_ref.at[slot], sem.at[slot])
    h = _load_row_tiles(xbuf_ref.at[slot]).astype(BF16)
    f = o_ref.shape[1]
    n_chunks = f // FF_CHUNK
    per = -(-tm // n_chunks)
    for ci in range(n_chunks):
        _issue_rows(h_hbm, idx_nxt_ref, xbuf_ref.at[nxt], sem.at[nxt], min(tm, ci * per), min(tm, (ci + 1) * per))
        c = ci * FF_CHUNK
        gt = jnp.dot(h, wg_ref[:, c:c + FF_CHUNK], preferred_element_type=F32)
        up = jnp.dot(h, wu_ref[:, c:c + FF_CHUNK], preferred_element_type=F32)
        o_ref[:, c:c + FF_CHUNK] = (gt * jax.nn.sigmoid(gt) * up).astype(o_ref.dtype)

    @pl.when(i == pl.num_programs(0) - 1)
    def _():
        _wait_row_gather(h_hbm, xbuf_ref.at[nxt], sem.at[nxt])


def _expert_up(h3, inv, wg, wu, tile_expert):
    rows = inv.shape[0]
    _, ns, _ = h3.shape
    d = wg.shape[1]
    f = wg.shape[2]
    tm = EXPERT_TILE
    nt = rows // tm
    idx3 = inv.reshape(nt, 1, tm)
    return pl.pallas_call(
        _expert_up_kernel,
        grid_spec=pltpu.PrefetchScalarGridSpec(
            num_scalar_prefetch=1,
            grid=(nt,),
            in_specs=[pl.BlockSpec((None, 1, tm), lambda i, te: (i, 0, 0), memory_space=pltpu.SMEM),
                      pl.BlockSpec((None, 1, tm), lambda i, te: (jnp.minimum(i + 1, nt - 1), 0, 0),
                                   memory_space=pltpu.SMEM),
                      pl.BlockSpec(memory_space=pl.ANY),
                      pl.BlockSpec((None, d, f), lambda i, te: (te[i], 0, 0)),
                      pl.BlockSpec((None, d, f), lambda i, te: (te[i], 0, 0))],
            out_specs=pl.BlockSpec((tm, f), lambda i, te: (i, 0)),
            scratch_shapes=[pltpu.VMEM((2, tm, ns, LANES), F32), pltpu.SemaphoreType.DMA((2,))],
        ),
        out_shape=jax.ShapeDtypeStruct((rows, f), BF16),
        compiler_params=_params("arbitrary"),
        name="expert_up",
    )(tile_expert, idx3, idx3, h3, wg, wu)


def _expert_down_kernel(te_ref, a_ref, w_ref, o_ref):
    _store_row_tiles(o_ref, jnp.dot(a_ref[...], w_ref[...], preferred_element_type=F32))


def _expert_down(a, wd, tile_expert):
    rows, f = a.shape
    d = wd.shape[2]
    tm = EXPERT_TILE
    return pl.pallas_call(
        _expert_down_kernel,
        grid_spec=pltpu.PrefetchScalarGridSpec(
            num_scalar_prefetch=1,
            grid=(rows // tm,),
            in_specs=[pl.BlockSpec((tm, f), lambda i, te: (i, 0)),
                      pl.BlockSpec((None, f, d), lambda i, te: (te[i], 0, 0))],
            out_specs=_row_tile_spec(tm, d),
        ),
        out_shape=jax.ShapeDtypeStruct((rows, d // LANES, LANES), F32),
        compiler_params=_params("arbitrary"),
        name="expert_down",
    )(tile_expert, a, wd)


def _combine_kernel(d1c_ref, d2c_ref, d1n_ref, d2n_ref, y_hbm, info_ref, x_ref, gp_ref, gate_ref, o_ref,
                    buf_ref, sem):
    i = pl.program_id(0)
    slot = i % 2
    nxt = 1 - slot
    tm, d = o_ref.shape
    ns = buf_ref.shape[3]
    per = tm // ns

    @pl.when(i == 0)
    def _():
        _start_row_gather(y_hbm, d1c_ref, buf_ref.at[0, 0], sem.at[0, 0])
        _start_row_gather(y_hbm, d2c_ref, buf_ref.at[0, 1], sem.at[0, 1])

    _wait_row_gather(y_hbm, buf_ref.at[slot, 0], sem.at[slot, 0])
    _wait_row_gather(y_hbm, buf_ref.at[slot, 1], sem.at[slot, 1])
    info = info_ref[...]
    g1 = info[:, 2:3]
    g2 = info[:, 3:4]
    ys = []
    ssq = jnp.zeros((tm, 1), F32)
    for s in range(ns):
        _issue_rows(y_hbm, d1n_ref, buf_ref.at[nxt, 0], sem.at[nxt, 0], s * per, (s + 1) * per)
        _issue_rows(y_hbm, d2n_ref, buf_ref.at[nxt, 1], sem.at[nxt, 1], s * per, (s + 1) * per)
        y_s = g1 * buf_ref[slot, 0, :, s, :] + g2 * buf_ref[slot, 1, :, s, :]
        ys.append(y_s)
        ssq = ssq + jnp.sum(y_s * y_s, axis=-1, keepdims=True)
    inv = lax.rsqrt(ssq / d + NORM_EPS)
    for s in range(ns):
        sl = slice(s * LANES, (s + 1) * LANES)
        o_ref[:, sl] = x_ref[:, sl] + gate_ref[:, sl] * (ys[s] * inv * gp_ref[:, sl])

    @pl.when(i == pl.num_programs(0) - 1)
    def _():
        _wait_row_gather(y_hbm, buf_ref.at[nxt, 0], sem.at[nxt, 0])
        _wait_row_gather(y_hbm, buf_ref.at[nxt, 1], sem.at[nxt, 1])


def _combine(ybuf3, dest1, dest2, info, x, g_post, gate, seq):
    n, d = x.shape
    tm = GATHER_TILE
    nt = n // tm
    tpb = seq // tm
    cur_spec = pl.BlockSpec((None, 1, tm), lambda i: (i, 0, 0), memory_space=pltpu.SMEM)
    nxt_spec = pl.BlockSpec((None, 1, tm), lambda i: (jnp.minimum(i + 1, nt - 1), 0, 0), memory_space=pltpu.SMEM)
    d1 = dest1.reshape(nt, 1, tm)
    d2 = dest2.reshape(nt, 1, tm)
    return pl.pallas_call(
        _combine_kernel,
        grid=(nt,),
        in_specs=[cur_spec, cur_spec, nxt_spec, nxt_spec, pl.BlockSpec(memory_space=pl.ANY),
                  _row_spec(tm, LANES), _row_spec(tm, d), _const_spec((1, d)), _batch_spec(d, tpb)],
        out_specs=_row_spec(tm, d),
        out_shape=jax.ShapeDtypeStruct((n, d), F32),
        scratch_shapes=[pltpu.VMEM((2, 2, tm, d // LANES, LANES), F32), pltpu.SemaphoreType.DMA((2, 2))],
        compiler_params=_params("arbitrary"),
        name="moe_combine",
    )(d1, d2, d1, d2, ybuf3, info, x, g_post, gate)


def _moe(x, g_pre, shift, scale, router_w, wg, wu, wd, g_post, gate, seq):
    n, d = x.shape
    wr_pad = jnp.pad(router_w, ((0, 0), (0, LANES - N_EXPERTS)))
    h3, info, cnt = _route(x, g_pre, shift, scale, wr_pad, seq)
    e1 = info[:, 0].astype(jnp.int32)
    e2 = info[:, 1].astype(jnp.int32)
    r1 = info[:, 4].astype(jnp.int32)
    r2 = info[:, 5].astype(jnp.int32)
    counts = cnt[0, :N_EXPERTS].astype(jnp.int32)
    te = EXPERT_TILE
    padded = (counts + te - 1) // te * te
    pend = jnp.cumsum(padded)
    pstart = pend - padded
    dest1 = pstart[e1] + r1
    dest2 = pstart[e2] + r2
    rows = 2 * n + N_EXPERTS * te
    tok = jnp.arange(n, dtype=jnp.int32)
    inv = jnp.zeros((rows,), jnp.int32).at[jnp.concatenate([dest1, dest2])].set(
        jnp.concatenate([tok, tok]), unique_indices=True)
    tile_start = jnp.arange(rows // te, dtype=jnp.int32) * te
    tile_expert = jnp.minimum(jnp.sum((tile_start[:, None] >= pend[None, :]).astype(jnp.int32), axis=1),
                              N_EXPERTS - 1)
    a = _expert_up(h3, inv, wg, wu, tile_expert)
    ybuf3 = _expert_down(a, wd, tile_expert)
    return _combine(ybuf3, dest1, dest2, info, x, g_post, gate, seq)


def _rope_lanes(w):
    half = QK_ROPE_DIM // 2
    z = jnp.zeros(w.shape[:-1] + (LANES // 2 - half,), w.dtype)
    return jnp.concatenate([w[..., :half], z, w[..., half:], z], axis=-1)


def kernel(x, c, positions, ada_w, ada_b, norm_mix_pre, norm_mix_post, norm_ffn_pre, norm_ffn_post, ssm_w_in, ssm_log_step, ssm_a_re, ssm_a_im, ssm_b_re, ssm_b_im, ssm_c_re, ssm_c_im, ssm_d, ssm_w_out, kv_ada_w, kv_ada_b, kv_norm, mla_w_dkv, mla_kv_norm, mla_w_ukv, mla_w_dq, mla_q_norm, mla_w_uq, mla_w_o, ffn_w_gu, ffn_w_down, moe_router, moe_w_gu, moe_w_down):
    bsz, seq, d = x.shape
    depth = ada_w.shape[0]
    n_a = ssm_w_in.shape[0]
    n = bsz * seq
    d_ff = ffn_w_down.shape[1]
    e_ff = moe_w_down.shape[2]

    c_pad = jnp.pad(c, ((0, 8 - bsz), (0, 0)))
    ada = _ada_proj(c_pad, ada_w, ada_b[:, None, :], 2048)[:, :bsz]
    kv_ada = _ada_proj(c_pad, kv_ada_w[None], kv_ada_b[None, None, :], 2048)[0, :bsz]

    def vec(a):
        return a[:, None, :]

    def gain(gv):
        return gv[None, :]

    inv_freq = ROPE_THETA ** (-jnp.arange(0, QK_ROPE_DIM, 2, dtype=F32) / QK_ROPE_DIM)
    ang = positions.astype(F32)[..., None] * inv_freq
    cos = jnp.cos(ang).reshape(n, -1)
    sin = jnp.sin(ang).reshape(n, -1)
    zpad = jnp.zeros_like(cos)
    cos_t = jnp.concatenate([cos, zpad, cos, zpad], axis=-1)
    sin_t = jnp.concatenate([-sin, zpad, sin, zpad], axis=-1)

    xs = x.reshape(n, d)
    kn = kr = v = None
    for i in range(depth):
        sh_m, sc_m, g_m, sh_f, sc_f, g_f = [vec(a) for a in jnp.split(ada[i], 6, axis=-1)]
        if i < n_a:
            u3 = _s5_in(xs, gain(norm_mix_pre[i]), sh_m, sc_m, ssm_w_in[i].T.astype(BF16), seq)
            tabs = _s5_conv_tables(ssm_log_step[i], ssm_a_re[i], ssm_a_im[i], ssm_b_re[i], ssm_b_im[i],
                                   ssm_c_re[i], ssm_c_im[i], ssm_d[i], seq // LANES)
            y3 = _s5_conv(u3, *tabs, seq)
            xs = _s5_out(y3, ssm_w_out[i].astype(BF16), xs, gain(norm_mix_post[i]), g_m, seq)
        else:
            if i == n_a:
                kv_sh, kv_sc = [vec(a) for a in jnp.split(kv_ada, 2, axis=-1)]
                w_ukv = mla_w_ukv.reshape(KV_LORA_RANK, N_HEADS, QK_NOPE_DIM + V_HEAD_DIM)
                wk = w_ukv[:, :, :QK_NOPE_DIM].reshape(KV_LORA_RANK, -1).astype(BF16)
                wv = w_ukv[:, :, QK_NOPE_DIM:].reshape(KV_LORA_RANK, -1).astype(BF16)
                wc = mla_w_dkv[:, :KV_LORA_RANK].astype(BF16)
                wr = _rope_lanes(mla_w_dkv[:, KV_LORA_RANK:]).astype(BF16)
                kn, kr, v = _mla_kv(xs, gain(kv_norm), kv_sh, kv_sc, wc, wr, gain(mla_kv_norm), wk, wv,
                                    cos_t, sin_t, seq)
                kn = kn.reshape(bsz, seq, -1)
                kr = kr.reshape(bsz, seq, -1)
                v = v.reshape(bsz, seq, -1)
            j = i - n_a
            w_uq = mla_w_uq[j].reshape(-1, N_HEADS, QK_NOPE_DIM + QK_ROPE_DIM)
            wn = w_uq[:, :, :QK_NOPE_DIM].reshape(w_uq.shape[0], -1).astype(BF16)
            wqr = _rope_lanes(w_uq[:, :, QK_NOPE_DIM:]).reshape(w_uq.shape[0], -1).astype(BF16)
            qn, qr = _mla_q(xs, gain(norm_mix_pre[i]), sh_m, sc_m, mla_w_dq[j].astype(BF16),
                            gain(mla_q_norm[j]), wn, wqr, cos_t, sin_t, seq)
            o = _attention(qn.reshape(bsz, seq, -1), qr.reshape(bsz, seq, -1), kn, kr, v, bsz, seq)
            xs = _mm_post(o.reshape(n, -1), mla_w_o[j].astype(BF16), xs, gain(norm_mix_post[i]), g_m, seq,
                          glu=False)
        if i % 2 == 0:
            w_gu = ffn_w_gu[i // 2]
            a = _ffn_up(xs, gain(norm_ffn_pre[i]), sh_f, sc_f, w_gu[:, :d_ff].astype(BF16),
                        w_gu[:, d_ff:].astype(BF16), seq)
            xs = _mm_post(a, ffn_w_down[i // 2].astype(BF16), xs, gain(norm_ffn_post[i]), g_f, seq, glu=False)
        else:
            w_gu = moe_w_gu[i // 2]
            xs = _moe(xs, gain(norm_ffn_pre[i]), sh_f, sc_f, moe_router[i // 2],
                      w_gu[:, :, :e_ff].astype(BF16), w_gu[:, :, e_ff:].astype(BF16),
                      moe_w_down[i // 2].astype(BF16), gain(norm_ffn_post[i]), g_f, seq)
    return xs.reshape(bsz, seq, d)
```

```python
import functools
import math

import jax
import jax.numpy as jnp
from jax import lax
from jax.experimental import pallas as pl
from jax.experimental.pallas import tpu as pltpu

F32 = jnp.float32
BF16 = jnp.bfloat16

NORM_EPS = 1e-6
LANES = 128
SSM_GROUP = 16
SSM_STATE = 64
N_HEADS = 8
QK_NOPE_DIM = 128
QK_ROPE_DIM = 64
V_HEAD_DIM = 128
KV_LORA_RANK = 256
ROPE_THETA = 10000.0
N_EXPERTS = 8
SOFTMAX_SCALE = (QK_NOPE_DIM + QK_ROPE_DIM) ** -0.5
Q_SCALE = SOFTMAX_SCALE * math.log2(math.e)

ROW_TILE = 512
S5_ROW_TILE = 1024
SSM_CHUNK = 128
SSM_CH_BLOCK = 128
ATTN_TILE = 1024
ROUTE_TILE = 256
EXPERT_TILE = 256
GATHER_TILE = 256
FF_CHUNK = 256
CAST_BLOCK_BYTES = 8 * 1024 * 1024


def _params(*sem):
    return pltpu.CompilerParams(dimension_semantics=sem)


def _rms(x, g):
    return x * lax.rsqrt(jnp.mean(x * x, axis=-1, keepdims=True) + NORM_EPS) * g


def _modnorm(x, g, shift, scale):
    return _rms(x, g) * (1.0 + scale) + shift


def _row_spec(tm, d):
    return pl.BlockSpec((tm, d), lambda i: (i, 0))


def _const_spec(shape):
    return pl.BlockSpec(shape, lambda i: tuple(0 for _ in shape))


def _batch_spec(d, tiles_per_batch):
    return pl.BlockSpec((None, 1, d), lambda i: (i // tiles_per_batch, 0, 0))


def _ada_kernel(c_ref, w_ref, b_ref, o_ref):
    c = c_ref[...]
    ca = c * jax.nn.sigmoid(c)
    o_ref[...] = jnp.dot(ca, w_ref[...], preferred_element_type=F32) + b_ref[...]


def _ada_proj(c_pad, w, b, tn):
    nl, d, e = w.shape
    return pl.pallas_call(
        _ada_kernel,
        grid=(nl, e // tn),
        in_specs=[
            pl.BlockSpec((8, d), lambda l, j: (0, 0)),
            pl.BlockSpec((None, d, tn), lambda l, j: (l, 0, j)),
            pl.BlockSpec((None, 1, tn), lambda l, j: (l, 0, j)),
        ],
        out_specs=pl.BlockSpec((None, 8, tn), lambda l, j: (l, 0, j)),
        out_shape=jax.ShapeDtypeStruct((nl, 8, e), F32),
        compiler_params=_params("parallel", "parallel"),
        name="ada_proj",
    )(c_pad, w, b)


def _cast_kernel(w_ref, *o_refs):
    f = o_refs[0].shape[-1]
    for k, o_ref in enumerate(o_refs):
        o_ref[...] = w_ref[:, k * f:(k + 1) * f].astype(o_ref.dtype)


def _cast_experts(w, n_split):
    e, k, ftot = w.shape
    f = ftot // n_split
    tk = max(t for t in range(16, k + 1, 16) if k % t == 0 and t * ftot * 4 <= CAST_BLOCK_BYTES)
    return pl.pallas_call(
        _cast_kernel,
        grid=(e, k // tk),
        in_specs=[pl.BlockSpec((None, tk, ftot), lambda a, b: (a, b, 0))],
        out_specs=[pl.BlockSpec((None, tk, f), lambda a, b: (a, b, 0)) for _ in range(n_split)],
        out_shape=[jax.ShapeDtypeStruct((e, k, f), BF16) for _ in range(n_split)],
        compiler_params=_params("parallel", "parallel"),
        name="cast_experts",
    )(w)


def _modmm_kernel(x_ref, g_ref, sh_ref, sc_ref, w_ref, o_ref):
    h = _modnorm(x_ref[...], g_ref[...], sh_ref[...], sc_ref[...]).astype(BF16)
    o_ref[...] = jnp.dot(h, w_ref[...], preferred_element_type=F32).astype(o_ref.dtype)


def _modmm(x, g, shift, scale, w, seq, out_dtype):
    n, d = x.shape
    e = w.shape[1]
    tm = ROW_TILE
    tpb = seq // tm
    return pl.pallas_call(
        _modmm_kernel,
        grid=(n // tm,),
        in_specs=[_row_spec(tm, d), _const_spec((1, d)), _batch_spec(d, tpb), _batch_spec(d, tpb),
                  _const_spec((d, e))],
        out_specs=_row_spec(tm, e),
        out_shape=jax.ShapeDtypeStruct((n, e), out_dtype),
        compiler_params=_params("parallel"),
        name="modnorm_matmul",
    )(x, g, shift, scale, w)


def _ffn_up_kernel(x_ref, g_ref, sh_ref, sc_ref, wg_ref, wu_ref, o_ref):
    h = _modnorm(x_ref[...], g_ref[...], sh_ref[...], sc_ref[...]).astype(BF16)
    f = o_ref.shape[1]
    for c in range(0, f, FF_CHUNK):
        gt = jnp.dot(h, wg_ref[:, c:c + FF_CHUNK], preferred_element_type=F32)
        up = jnp.dot(h, wu_ref[:, c:c + FF_CHUNK], preferred_element_type=F32)
        o_ref[:, c:c + FF_CHUNK] = (gt * jax.nn.sigmoid(gt) * up).astype(o_ref.dtype)


def _ffn_up(x, g, shift, scale, wg, wu, seq):
    n, d = x.shape
    f = wg.shape[1]
    tm = ROW_TILE
    tpb = seq // tm
    return pl.pallas_call(
        _ffn_up_kernel,
        grid=(n // tm,),
        in_specs=[_row_spec(tm, d), _const_spec((1, d)), _batch_spec(d, tpb), _batch_spec(d, tpb),
                  _const_spec((d, f)), _const_spec((d, f))],
        out_specs=_row_spec(tm, f),
        out_shape=jax.ShapeDtypeStruct((n, f), BF16),
        compiler_params=_params("parallel"),
        name="ffn_up",
    )(x, g, shift, scale, wg, wu)


def _mm_post_kernel(a_ref, w_ref, x_ref, gp_ref, gate_ref, o_ref, *, glu):
    y = jnp.dot(a_ref[...], w_ref[...], preferred_element_type=F32)
    if glu:
        d = o_ref.shape[1]
        y = y[:, :d] * jax.nn.sigmoid(y[:, d:])
    o_ref[...] = x_ref[...] + gate_ref[...] * _rms(y, gp_ref[...])


def _mm_post(a, w, x, g_post, gate, seq, glu):
    n, k = a.shape
    d = x.shape[1]
    e = w.shape[1]
    tm = ROW_TILE
    tpb = seq // tm
    return pl.pallas_call(
        functools.partial(_mm_post_kernel, glu=glu),
        grid=(n // tm,),
        in_specs=[_row_spec(tm, k), _const_spec((k, e)), _row_spec(tm, d), _const_spec((1, d)),
                  _batch_spec(d, tpb)],
        out_specs=_row_spec(tm, d),
        out_shape=jax.ShapeDtypeStruct((n, d), F32),
        compiler_params=_params("parallel"),
        name="matmul_post",
    )(a, w, x, g_post, gate)


def _s5_kernel(u_ref, b_ref, c_ref, d_ref, apow_ref, tpow_ref, o_ref, st_ref, carry_ref, *, n_steps):
    j = pl.program_id(2)
    t = u_ref.shape[0]
    half = carry_ref.shape[1] // 2
    pad = st_ref.shape[0] - t

    @pl.when(j == 0)
    def _():
        carry_ref[...] = jnp.zeros_like(carry_ref)
        st_ref[0:pad, :] = jnp.zeros((pad, 2 * half), F32)

    u = u_ref[...]
    st_ref[pad:, :] = jnp.dot(u.astype(BF16), b_ref[...], preferred_element_type=F32)
    for k in range(n_steps):
        sft = 1 << k
        ar = apow_ref[k:k + 1, :half]
        ai = apow_ref[k:k + 1, half:]
        cur_re = st_ref[pad:, :half]
        cur_im = st_ref[pad:, half:]
        sh_re = st_ref[pad - sft:pad - sft + t, :half]
        sh_im = st_ref[pad - sft:pad - sft + t, half:]
        st_ref[pad:, :half] = cur_re + (ar * sh_re - ai * sh_im)
        st_ref[pad:, half:] = cur_im + (ar * sh_im + ai * sh_re)
    cr = carry_ref[:, :half]
    ci = carry_ref[:, half:]
    pr = tpow_ref[:, :half]
    pi = tpow_ref[:, half:]
    s_re = st_ref[pad:, :half] + (pr * cr - pi * ci)
    s_im = st_ref[pad:, half:] + (pr * ci + pi * cr)
    carry_ref[:, :half] = s_re[t - 1:t, :]
    carry_ref[:, half:] = s_im[t - 1:t, :]
    st = jnp.concatenate([s_re, s_im], axis=1).astype(BF16)
    y = jnp.dot(st, c_ref[...], preferred_element_type=F32)
    o_ref[...] = jax.nn.gelu(y + d_ref[...] * u).astype(o_ref.dtype)


def _s5_scan(u, bblk, cblk, d_skip, apow, tpow, bsz, seq):
    n, width = u.shape
    cb = SSM_CH_BLOCK
    t = SSM_CHUNK
    nst = bblk.shape[2]
    n_steps = apow.shape[1]
    chunks = seq // t
    return pl.pallas_call(
        functools.partial(_s5_kernel, n_steps=n_steps),
        grid=(bsz, width // cb, chunks),
        in_specs=[
            pl.BlockSpec((t, cb), lambda b, c, j: (b * chunks + j, c)),
            pl.BlockSpec((None, cb, nst), lambda b, c, j: (c, 0, 0)),
            pl.BlockSpec((None, nst, cb), lambda b, c, j: (c, 0, 0)),
            pl.BlockSpec((1, cb), lambda b, c, j: (0, c)),
            pl.BlockSpec((None, n_steps, nst), lambda b, c, j: (c, 0, 0)),
            pl.BlockSpec((None, t, nst), lambda b, c, j: (c, 0, 0)),
        ],
        out_specs=pl.BlockSpec((t, cb), lambda b, c, j: (b * chunks + j, c)),
        out_shape=jax.ShapeDtypeStruct((n, width), BF16),
        scratch_shapes=[pltpu.VMEM((t // 2 + t, nst), F32), pltpu.VMEM((1, nst), F32)],
        compiler_params=_params("parallel", "parallel", "arbitrary"),
        name="s5_scan",
    )(u, bblk, cblk, d_skip, apow, tpow)


def _s5_tables(log_step, a_re, a_im, b_re, b_im, c_re, c_im):
    g, p = a_re.shape
    gpb = SSM_CH_BLOCK // SSM_GROUP
    nblk = g // gpb
    delta = jnp.exp(log_step)[:, None]
    mag = jnp.exp(a_re * delta)
    ang = a_im * delta
    lb_re, lb_im = mag * jnp.cos(ang), mag * jnp.sin(ang)
    den = a_re * a_re + a_im * a_im
    nr, ni = lb_re - 1.0, lb_im
    coef_re = (nr * a_re + ni * a_im) / den
    coef_im = (ni * a_re - nr * a_im) / den
    bb_re = coef_re[..., None] * b_re - coef_im[..., None] * b_im
    bb_im = coef_re[..., None] * b_im + coef_im[..., None] * b_re
    eye = jnp.eye(gpb, dtype=F32)

    def blockdiag_in(m):
        m = m.reshape(nblk, gpb, p, SSM_GROUP)
        return jnp.einsum('ngpc,gh->ngchp', m, eye).reshape(nblk, gpb * SSM_GROUP, gpb * p)

    def blockdiag_out(m):
        m = m.reshape(nblk, gpb, SSM_GROUP, p)
        return jnp.einsum('ngcp,gh->ngphc', m, eye).reshape(nblk, gpb * p, gpb * SSM_GROUP)

    bblk = jnp.concatenate([blockdiag_in(bb_re), blockdiag_in(bb_im)], axis=2).astype(BF16)
    cblk = jnp.concatenate([blockdiag_out(c_re), blockdiag_out(-c_im)], axis=1).astype(BF16)
    lr = lb_re.reshape(nblk, gpb * p)
    li = lb_im.reshape(nblk, gpb * p)
    n_steps = int(math.log2(SSM_CHUNK))
    sq = [(lr, li)]
    for _ in range(n_steps - 1):
        r, i = sq[-1]
        sq.append((r * r - i * i, 2.0 * r * i))
    apow = jnp.stack([jnp.concatenate([r, i], axis=1) for r, i in sq], axis=1)
    pr, pi = lr[:, None, :], li[:, None, :]
    for k in range(n_steps):
        r, i = sq[k]
        r, i = r[:, None, :], i[:, None, :]
        pr, pi = (jnp.concatenate([pr, pr * r - pi * i], axis=1),
                  jnp.concatenate([pi, pr * i + pi * r], axis=1))
    tpow = jnp.concatenate([pr, pi], axis=2)
    return bblk, cblk, apow, tpow


def _s5_in_kernel(x_ref, g_ref, sh_ref, sc_ref, wt_ref, o_ref):
    h = _modnorm(x_ref[...], g_ref[...], sh_ref[...], sc_ref[...]).astype(BF16)
    ut = lax.dot_general(wt_ref[...], h, (((1,), (1,)), ((), ())), preferred_element_type=F32)
    for k in range(o_ref.shape[1]):
        o_ref[:, k, :] = ut[:, k * LANES:(k + 1) * LANES]


def _s5_in(x, g, shift, scale, wt, seq):
    n, d = x.shape
    width = wt.shape[0]
    tm = S5_ROW_TILE
    tpb = seq // tm
    return pl.pallas_call(
        _s5_in_kernel,
        grid=(n // tm,),
        in_specs=[_row_spec(tm, d), _const_spec((1, d)), _batch_spec(d, tpb), _batch_spec(d, tpb),
                  _const_spec((width, d))],
        out_specs=pl.BlockSpec((width, tm // LANES, LANES), lambda i: (0, i, 0)),
        out_shape=jax.ShapeDtypeStruct((width, n // LANES, LANES), F32),
        compiler_params=_params("parallel"),
        name="s5_in",
    )(x, g, shift, scale, wt)


def _s5_conv_kernel(u_ref, k_ref, s_ref, cc_ref, a_ref, d_ref, o_ref, acc_ref, hs_ref, *, chunks_per_seq):
    gc, nc, t = u_ref.shape
    ny = gc * t
    pad = hs_ref.shape[0] - nc
    causal = lax.broadcasted_iota(jnp.int32, (t, t), 1) >= lax.broadcasted_iota(jnp.int32, (t, t), 0)

    def rhs_rows(ci):
        tiles = []
        for c in range(gc):
            lag = jnp.broadcast_to(k_ref[ci, c:c + 1, :], (t, t))
            toep = pltpu.roll(lag, 0, 1, stride=1, stride_axis=0)
            tiles.append(jnp.where(causal, toep, 0.0).astype(BF16))
        tiles.append(s_ref[ci])
        return jnp.concatenate(tiles, axis=1)

    for c0 in range(0, gc, 2):
        lhs = jnp.concatenate([u_ref[c0], u_ref[c0 + 1]], axis=1).astype(BF16)
        rhs = jnp.concatenate([rhs_rows(c0), rhs_rows(c0 + 1)], axis=0)
        part = jnp.dot(lhs, rhs, preferred_element_type=F32)
        if c0 == 0:
            acc_ref[...] = part
        else:
            acc_ref[...] += part

    hs_ref[0:pad, :] = jnp.zeros((pad, LANES), F32)
    hs_ref[pad:, :] = acc_ref[:, ny:]
    jl = lax.broadcasted_iota(jnp.int32, (nc, LANES), 0) & (chunks_per_seq - 1)
    for k in range(a_ref.shape[0] // 2):
        sft = 1 << k
        cur = hs_ref[pad:, :]
        sh = jnp.where(jl >= sft, hs_ref[pad - sft:pad - sft + nc, :], 0.0)
        hs_ref[pad:, :] = (cur + sh * a_ref[2 * k:2 * k + 1, :]
                           + pltpu.roll(sh, LANES // 2, 1) * a_ref[2 * k + 1:2 * k + 2, :])
    h_in = jnp.where(jl >= 1, hs_ref[pad - 1:pad - 1 + nc, :], 0.0).astype(BF16)
    y = acc_ref[:, :ny] + jnp.dot(h_in, cc_ref[...], preferred_element_type=F32)
    for c in range(gc):
        o_ref[c] = jax.nn.gelu(y[:, c * t:(c + 1) * t] + d_ref[c:c + 1, :] * u_ref[c])


def _s5_conv(u3, ktab, stab, cctab, atab, dtab, seq):
    width, nc, t = u3.shape
    gc = SSM_GROUP
    chunks_per_seq = seq // t
    blk = lambda *tail: pl.BlockSpec((None,) + tail, lambda g: (g,) + tuple(0 for _ in tail))
    return pl.pallas_call(
        functools.partial(_s5_conv_kernel, chunks_per_seq=chunks_per_seq),
        grid=(width // gc,),
        in_specs=[pl.BlockSpec((gc, nc, t), lambda g: (g, 0, 0)),
                  blk(gc, gc, t), blk(gc, t, LANES), blk(LANES, gc * t), blk(atab.shape[1], LANES), blk(gc, t)],
        out_specs=pl.BlockSpec((gc, nc, t), lambda g: (g, 0, 0)),
        out_shape=jax.ShapeDtypeStruct((width, nc, t), F32),
        scratch_shapes=[pltpu.VMEM((nc, gc * t + LANES), F32), pltpu.VMEM((chunks_per_seq + nc, LANES), F32)],
        compiler_params=_params("parallel"),
        name="s5_conv",
    )(u3, ktab, stab, cctab, atab, dtab)


def _s5_out_kernel(y_ref, w_ref, x_ref, gp_ref, gate_ref, o_ref, m_ref):
    d = o_ref.shape[1]
    yt = jnp.concatenate([y_ref[:, k, :] for k in range(y_ref.shape[1])], axis=1)
    y = yt.T.astype(BF16)
    for c in range(0, d, FF_CHUNK):
        za = jnp.dot(y, w_ref[:, c:c + FF_CHUNK], preferred_element_type=F32)
        zb = jnp.dot(y, w_ref[:, d + c:d + c + FF_CHUNK], preferred_element_type=F32)
        m_ref[:, c:c + FF_CHUNK] = za * jax.nn.sigmoid(zb)
    o_ref[...] = x_ref[...] + gate_ref[...] * _rms(m_ref[...], gp_ref[...])


def _s5_out(y3, w, x, g_post, gate, seq):
    n, d = x.shape
    width = y3.shape[0]
    tm = S5_ROW_TILE
    tpb = seq // tm
    return pl.pallas_call(
        _s5_out_kernel,
        grid=(n // tm,),
        in_specs=[pl.BlockSpec((width, tm // LANES, LANES), lambda i: (0, i, 0)), _const_spec(w.shape),
                  _row_spec(tm, d), _const_spec((1, d)), _batch_spec(d, tpb)],
        out_specs=_row_spec(tm, d),
        out_shape=jax.ShapeDtypeStruct((n, d), F32),
        scratch_shapes=[pltpu.VMEM((tm, d), F32)],
        compiler_params=_params("parallel"),
        name="s5_out",
    )(y3, w, x, g_post, gate)


def _s5_conv_tables(log_step, a_re, a_im, b_re, b_im, c_re, c_im, d_skip, chunks_per_seq):
    g, p = a_re.shape
    t = LANES
    gc = SSM_GROUP
    hi = lax.Precision.HIGHEST
    delta = jnp.exp(log_step)[:, None]
    mag = jnp.exp(a_re * delta)
    ang = a_im * delta
    lb_re, lb_im = mag * jnp.cos(ang), mag * jnp.sin(ang)
    den = a_re * a_re + a_im * a_im
    nr, ni = lb_re - 1.0, lb_im
    coef_re = (nr * a_re + ni * a_im) / den
    coef_im = (ni * a_re - nr * a_im) / den
    bb_re = coef_re[..., None] * b_re - coef_im[..., None] * b_im
    bb_im = coef_re[..., None] * b_im + coef_im[..., None] * b_re

    def cmul(xr, xi, yr, yi):
        return xr * yr - xi * yi, xr * yi + xi * yr

    sq = [(lb_re, lb_im)]
    for _ in range(int(math.log2(t))):
        sq.append(cmul(*sq[-1], *sq[-1]))
    pr, pi = jnp.ones((1, g, p), F32), jnp.zeros((1, g, p), F32)
    for k in range(int(math.log2(t))):
        nr_, ni_ = cmul(pr, pi, sq[k][0][None], sq[k][1][None])
        pr, pi = jnp.concatenate([pr, nr_], axis=0), jnp.concatenate([pi, ni_], axis=0)
    lt_re, lt_im = sq[-1]
    p1r = jnp.concatenate([pr[1:], lt_re[None]], axis=0)
    p1i = jnp.concatenate([pi[1:], lt_im[None]], axis=0)

    w_re = jnp.einsum('gcp,gpd->gcdp', c_re, bb_re) - jnp.einsum('gcp,gpd->gcdp', c_im, bb_im)
    w_im = jnp.einsum('gcp,gpd->gcdp', c_re, bb_im) + jnp.einsum('gcp,gpd->gcdp', c_im, bb_re)
    ktab = (jnp.einsum('gcdp,kgp->gdck', w_re, pr, precision=hi)
            - jnp.einsum('gcdp,kgp->gdck', w_im, pi, precision=hi))
    rr = pr[::-1].transpose(1, 0, 2)[:, None]
    ri = pi[::-1].transpose(1, 0, 2)[:, None]
    br = bb_re.transpose(0, 2, 1)[:, :, None, :]
    bi = bb_im.transpose(0, 2, 1)[:, :, None, :]
    s_re, s_im = cmul(rr, ri, br, bi)
    stab = jnp.concatenate([s_re, s_im], axis=-1).astype(BF16)
    cr = c_re.transpose(0, 2, 1)[:, :, :, None]
    ci = c_im.transpose(0, 2, 1)[:, :, :, None]
    qr = p1r.transpose(1, 2, 0)[:, :, None, :]
    qi = p1i.transpose(1, 2, 0)[:, :, None, :]
    m_re, m_im = cmul(cr, ci, qr, qi)
    cctab = jnp.concatenate([m_re, -m_im], axis=1).reshape(g, 2 * p, gc * t).astype(BF16)
    rows = []
    ar, ai = lt_re, lt_im
    for _ in range(int(math.log2(chunks_per_seq))):
        rows += [jnp.concatenate([ar, ar], axis=1), jnp.concatenate([-ai, ai], axis=1)]
        ar, ai = cmul(ar, ai, ar, ai)
    atab = jnp.stack(rows, axis=1)
    dtab = jnp.broadcast_to(d_skip.reshape(g, gc, 1), (g, gc, t))
    return ktab, stab, cctab, atab, dtab


def _rope(x, cos, sin):
    return x * cos + pltpu.roll(x, LANES // 2, 1) * sin


def _kv_kernel(x_ref, g_ref, sh_ref, sc_ref, wc_ref, wr_ref, gl_ref, wk_ref, wv_ref, cos_ref, sin_ref,
               kn_ref, kr_ref, v_ref):
    hs = _modnorm(x_ref[...], g_ref[...], sh_ref[...], sc_ref[...]).astype(BF16)
    c = jnp.dot(hs, wc_ref[...], preferred_element_type=F32)
    ckv = _rms(c, gl_ref[...]).astype(BF16)
    kn_ref[...] = jnp.dot(ckv, wk_ref[...], preferred_element_type=F32).astype(BF16)
    v_ref[...] = jnp.dot(ckv, wv_ref[...], preferred_element_type=F32).astype(BF16)
    r = jnp.dot(hs, wr_ref[...], preferred_element_type=F32)
    kr_ref[...] = _rope(r, cos_ref[...], sin_ref[...]).astype(BF16)


def _mla_kv(x, g, shift, scale, wc, wr, gl, wk, wv, cos, sin, seq):
    n, d = x.shape
    tm = ROW_TILE
    tpb = seq // tm
    hk = wk.shape[1]
    hv = wv.shape[1]
    return pl.pallas_call(
        _kv_kernel,
        grid=(n // tm,),
        in_specs=[_row_spec(tm, d), _const_spec((1, d)), _batch_spec(d, tpb), _batch_spec(d, tpb),
                  _const_spec(wc.shape), _const_spec(wr.shape), _const_spec(gl.shape),
                  _const_spec(wk.shape), _const_spec(wv.shape), _row_spec(tm, LANES), _row_spec(tm, LANES)],
        out_specs=[_row_spec(tm, hk), _row_spec(tm, LANES), _row_spec(tm, hv)],
        out_shape=[jax.ShapeDtypeStruct((n, hk), BF16), jax.ShapeDtypeStruct((n, LANES), BF16),
                   jax.ShapeDtypeStruct((n, hv), BF16)],
        compiler_params=_params("parallel"),
        name="mla_kv",
    )(x, g, shift, scale, wc, wr, gl, wk, wv, cos, sin)


def _q_kernel(x_ref, g_ref, sh_ref, sc_ref, wd_ref, gq_ref, wn_ref, wr_ref, cos_ref, sin_ref,
              qn_ref, qr_ref):
    h = _modnorm(x_ref[...], g_ref[...], sh_ref[...], sc_ref[...]).astype(BF16)
    ql = jnp.dot(h, wd_ref[...], preferred_element_type=F32)
    qn = _rms(ql, gq_ref[...]).astype(BF16)
    qn_ref[...] = (jnp.dot(qn, wn_ref[...], preferred_element_type=F32) * Q_SCALE).astype(BF16)
    r = jnp.dot(qn, wr_ref[...], preferred_element_type=F32)
    cos = cos_ref[...] * Q_SCALE
    sin = sin_ref[...] * Q_SCALE
    for hd in range(qr_ref.shape[1] // LANES):
        sl = slice(hd * LANES, (hd + 1) * LANES)
        qr_ref[:, sl] = _rope(r[:, sl], cos, sin).astype(BF16)


def _mla_q(x, g, shift, scale, wd, gq, wn, wr, cos, sin, seq):
    n, d = x.shape
    tm = ROW_TILE
    tpb = seq // tm
    e = wn.shape[1]
    return pl.pallas_call(
        _q_kernel,
        grid=(n // tm,),
        in_specs=[_row_spec(tm, d), _const_spec((1, d)), _batch_spec(d, tpb), _batch_spec(d, tpb),
                  _const_spec(wd.shape), _const_spec(gq.shape), _const_spec(wn.shape), _const_spec(wr.shape),
                  _row_spec(tm, LANES), _row_spec(tm, LANES)],
        out_specs=[_row_spec(tm, e), _row_spec(tm, e)],
        out_shape=[jax.ShapeDtypeStruct((n, e), BF16), jax.ShapeDtypeStruct((n, e), BF16)],
        compiler_params=_params("parallel"),
        name="mla_q",
    )(x, g, shift, scale, wd, gq, wn, wr, cos, sin)


def _attn_kernel(qn_ref, qr_ref, kn_ref, kr_ref, v_ref, o_ref):
    i = pl.program_id(2)
    tq = qn_ref.shape[0]
    tk = tq
    q = jnp.concatenate([qn_ref[...], qr_ref[...]], axis=1)

    def step(j, carry, diagonal):
        m, l, acc = carry
        start = pl.multiple_of(j * tk, tk)
        k = jnp.concatenate([kn_ref[pl.ds(start, tk), :], kr_ref[pl.ds(start, tk), :]], axis=1)
        s = lax.dot_general(q, k, (((1,), (1,)), ((), ())), preferred_element_type=F32)
        if diagonal:
            row = lax.broadcasted_iota(jnp.int32, (tq, tk), 0)
            col = lax.broadcasted_iota(jnp.int32, (tq, tk), 1)
            s = jnp.where(col <= row, s, -jnp.inf)
        m_new = jnp.maximum(m, jnp.max(s, axis=-1, keepdims=True))
        alpha = jnp.exp2(m - m_new)
        p = jnp.exp2(s - m_new)
        l = alpha * l + jnp.sum(p, axis=-1, keepdims=True)
        acc = alpha * acc + jnp.dot(p.astype(BF16), v_ref[pl.ds(start, tk), :], preferred_element_type=F32)
        return m_new, l, acc

    init = (jnp.full((tq, 1), -jnp.inf, F32), jnp.zeros((tq, 1), F32), jnp.zeros((tq, v_ref.shape[1]), F32))
    def pair(jj, c):
        return step(2 * jj + 1, step(2 * jj, c, False), False)

    carry = lax.fori_loop(0, i // 2, pair, init)
    carry = lax.fori_loop(0, i % 2, lambda _, c: step(i - 1, c, False), carry)
    _, l, acc = step(i, carry, diagonal=True)
    o_ref[...] = (acc / l).astype(o_ref.dtype)


def _attention(qn, qr, kn, kr, v, bsz, seq):
    tq = ATTN_TILE
    hd = LANES
    return pl.pallas_call(
        _attn_kernel,
        grid=(bsz, N_HEADS, seq // tq),
        in_specs=[
            pl.BlockSpec((None, tq, hd), lambda b, h, i: (b, i, h)),
            pl.BlockSpec((None, tq, hd), lambda b, h, i: (b, i, h)),
            pl.BlockSpec((None, seq, hd), lambda b, h, i: (b, 0, h)),
            pl.BlockSpec((None, seq, hd), lambda b, h, i: (b, 0, 0)),
            pl.BlockSpec((None, seq, hd), lambda b, h, i: (b, 0, h)),
        ],
        out_specs=pl.BlockSpec((None, tq, hd), lambda b, h, i: (b, i, h)),
        out_shape=jax.ShapeDtypeStruct((bsz, seq, N_HEADS * hd), BF16),
        compiler_params=_params("parallel", "parallel", "parallel"),
        name="attention",
    )(qn, qr, kn, kr, v)


INFO_ROWS = 8


def _route_kernel(x_ref, g_ref, sh_ref, sc_ref, wr_ref, h_ref, info_ref, infot_ref, cnt_ref, carry_ref):
    @pl.when(pl.program_id(0) == 0)
    def _():
        carry_ref[...] = jnp.zeros_like(carry_ref)

    tm = x_ref.shape[0]
    h = _modnorm(x_ref[...], g_ref[...], sh_ref[...], sc_ref[...])
    _store_row_tiles(h_ref, h)
    logits = jnp.dot(h, wr_ref[...], preferred_element_type=F32, precision=lax.Precision.HIGHEST)
    lane = lax.broadcasted_iota(jnp.int32, (tm, LANES), 1).astype(F32)
    neg = -jnp.inf
    lg = jnp.where(lane < N_EXPERTS, logits, neg)
    l1 = jnp.max(lg, axis=-1, keepdims=True)
    e1 = jnp.min(jnp.where(lg == l1, lane, float(LANES)), axis=-1, keepdims=True)
    lg2 = jnp.where(lane == e1, neg, lg)
    l2 = jnp.max(lg2, axis=-1, keepdims=True)
    e2 = jnp.min(jnp.where(lg2 == l2, lane, float(LANES)), axis=-1, keepdims=True)
    tt = jnp.exp(l2 - l1)
    g1 = 1.0 / (1.0 + tt)
    g2 = tt / (1.0 + tt)
    oh1 = lane == e1
    oh2 = lane == e2
    oh = jnp.where(oh1 | oh2, 1.0, 0.0)
    tri = (lax.broadcasted_iota(jnp.int32, (tm, tm), 0) > lax.broadcasted_iota(jnp.int32, (tm, tm), 1))
    cum = jnp.dot(jnp.where(tri, 1.0, 0.0).astype(BF16), oh.astype(BF16),
                  preferred_element_type=F32) + carry_ref[...]
    r1 = jnp.sum(jnp.where(oh1, cum, 0.0), axis=-1, keepdims=True)
    r2 = jnp.sum(jnp.where(oh2, cum, 0.0), axis=-1, keepdims=True)
    carry_ref[...] = carry_ref[...] + jnp.sum(oh, axis=0, keepdims=True)
    cnt_ref[...] = carry_ref[...]
    info = jnp.where(lane == 0, e1,
           jnp.where(lane == 1, e2,
           jnp.where(lane == 2, g1,
           jnp.where(lane == 3, g2,
           jnp.where(lane == 4, r1, r2)))))
    info_ref[...] = info
    infot_ref[...] = info.T[:INFO_ROWS, :]


def _route(x, g, shift, scale, wr_pad, seq):
    n, d = x.shape
    tm = ROUTE_TILE
    tpb = seq // tm
    return pl.pallas_call(
        _route_kernel,
        grid=(n // tm,),
        in_specs=[_row_spec(tm, d), _const_spec((1, d)), _batch_spec(d, tpb), _batch_spec(d, tpb),
                  _const_spec((d, LANES))],
        out_specs=[_row_tile_spec(tm, d), _row_spec(tm, LANES), pl.BlockSpec((INFO_ROWS, tm), lambda i: (0, i)),
                   _const_spec((1, LANES))],
        out_shape=[jax.ShapeDtypeStruct((n, d // LANES, LANES), F32), jax.ShapeDtypeStruct((n, LANES), F32),
                   jax.ShapeDtypeStruct((INFO_ROWS, n), F32), jax.ShapeDtypeStruct((1, LANES), F32)],
        scratch_shapes=[pltpu.VMEM((1, LANES), F32)],
        compiler_params=_params("arbitrary"),
        name="moe_route",
    )(x, g, shift, scale, wr_pad)


def _row_tile_spec(tm, d):
    return pl.BlockSpec((tm, d // LANES, LANES), lambda i, *_: (i, 0, 0))


def _store_row_tiles(ref, val):
    for s in range(ref.shape[1]):
        ref[:, s, :] = val[:, s * LANES:(s + 1) * LANES]


def _load_row_tiles(ref):
    return jnp.concatenate([ref[:, s, :] for s in range(ref.shape[1])], axis=1)


DMA_ISSUE_UNROLL = 8


def _issue_row(src_hbm, idx_ref, dst_ref, sem, r, priority):
    pltpu.make_async_copy(src_hbm.at[pl.ds(idx_ref[0, r], 1)], dst_ref.at[pl.ds(r, 1)], sem).start(
        priority=priority)


def _issue_rows(src_hbm, idx_ref, dst_ref, sem, lo, hi):
    for r in range(lo, hi):
        _issue_row(src_hbm, idx_ref, dst_ref, sem, r, r % 2)


def _start_row_gather(src_hbm, idx_ref, dst_ref, sem):
    def issue(blk, c):
        for u in range(DMA_ISSUE_UNROLL):
            _issue_row(src_hbm, idx_ref, dst_ref, sem, blk * DMA_ISSUE_UNROLL + u, 0)
        return c

    lax.fori_loop(0, dst_ref.shape[0] // DMA_ISSUE_UNROLL, issue, 0)


def _wait_row_gather(src_hbm, dst_ref, sem):
    pltpu.make_async_copy(src_hbm.at[pl.ds(0, dst_ref.shape[0])], dst_ref, sem).wait()


def _expert_up_kernel(te_ref, idx_cur_ref, idx_nxt_ref, h_hbm, wg_ref, wu_ref, o_ref, xbuf_ref, sem):
    i = pl.program_id(0)
    slot = i % 2
    nxt = 1 - slot

    @pl.when(i == 0)
    def _():
        _start_row_gather(h_hbm, idx_cur_ref, xbuf_ref.at[0], sem.at[0])

    @pl.when(i + 1 < pl.num_programs(0))
    def _():
        _start_row_gather(h_hbm, idx_nxt_ref, xbuf_ref.at[nxt], sem.at[nxt])

    _wait_row_gather(h_hbm, xbuf_ref.at[slot], sem.at[slot])
    h = _load_row_tiles(xbuf_ref.at[slot]).astype(BF16)
    f = o_ref.shape[1]
    for c in range(0, f, FF_CHUNK):
        gt = jnp.dot(h, wg_ref[:, c:c + FF_CHUNK], preferred_element_type=F32)
        up = jnp.dot(h, wu_ref[:, c:c + FF_CHUNK], preferred_element_type=F32)
        o_ref[:, c:c + FF_CHUNK] = (gt * jax.nn.sigmoid(gt) * up).astype(o_ref.dtype)


def _expert_up(h3, inv, wg, wu, tile_expert):
    rows = inv.shape[0]
    _, ns, _ = h3.shape
    d = wg.shape[1]
    f = wg.shape[2]
    tm = EXPERT_TILE
    nt = rows // tm
    idx3 = inv.reshape(nt, 1, tm)
    return pl.pallas_call(
        _expert_up_kernel,
        grid_spec=pltpu.PrefetchScalarGridSpec(
            num_scalar_prefetch=1,
            grid=(nt,),
            in_specs=[pl.BlockSpec((None, 1, tm), lambda i, te: (i, 0, 0), memory_space=pltpu.SMEM),
                      pl.BlockSpec((None, 1, tm), lambda i, te: (jnp.minimum(i + 1, nt - 1), 0, 0),
                                   memory_space=pltpu.SMEM),
                      pl.BlockSpec(memory_space=pl.ANY),
                      pl.BlockSpec((None, d, f), lambda i, te: (te[i], 0, 0)),
                      pl.BlockSpec((None, d, f), lambda i, te: (te[i], 0, 0))],
            out_specs=pl.BlockSpec((tm, f), lambda i, te: (i, 0)),
            scratch_shapes=[pltpu.VMEM((2, tm, ns, LANES), F32), pltpu.SemaphoreType.DMA((2,))],
        ),
        out_shape=jax.ShapeDtypeStruct((rows, f), BF16),
        compiler_params=_params("arbitrary"),
        name="expert_up",
    )(tile_expert, idx3, idx3, h3, wg, wu)


def _expert_down_kernel(te_ref, a_ref, w_ref, o_ref):
    _store_row_tiles(o_ref, jnp.dot(a_ref[...], w_ref[...], preferred_element_type=F32))


def _expert_down(a, wd, tile_expert):
    rows, f = a.shape
    d = wd.shape[2]
    tm = EXPERT_TILE
    return pl.pallas_call(
        _expert_down_kernel,
        grid_spec=pltpu.PrefetchScalarGridSpec(
            num_scalar_prefetch=1,
            grid=(rows // tm,),
            in_specs=[pl.BlockSpec((tm, f), lambda i, te: (i, 0)),
                      pl.BlockSpec((None, f, d), lambda i, te: (te[i], 0, 0))],
            out_specs=_row_tile_spec(tm, d),
        ),
        out_shape=jax.ShapeDtypeStruct((rows, d // LANES, LANES), F32),
        compiler_params=_params("arbitrary"),
        name="expert_down",
    )(tile_expert, a, wd)


def _combine_kernel(d1c_ref, d2c_ref, d1n_ref, d2n_ref, y_hbm, info_ref, x_ref, gp_ref, gate_ref, o_ref,
                    buf_ref, sem):
    i = pl.program_id(0)
    slot = i % 2
    nxt = 1 - slot
    tm, d = o_ref.shape
    ns = buf_ref.shape[3]
    per = tm // ns

    @pl.when(i == 0)
    def _():
        _start_row_gather(y_hbm, d1c_ref, buf_ref.at[0, 0], sem.at[0, 0])
        _start_row_gather(y_hbm, d2c_ref, buf_ref.at[0, 1], sem.at[0, 1])

    _wait_row_gather(y_hbm, buf_ref.at[slot, 0], sem.at[slot, 0])
    _wait_row_gather(y_hbm, buf_ref.at[slot, 1], sem.at[slot, 1])
    info = info_ref[...]
    g1 = info[:, 2:3]
    g2 = info[:, 3:4]
    ys = []
    ssq = jnp.zeros((tm, 1), F32)
    for s in range(ns):
        _issue_rows(y_hbm, d1n_ref, buf_ref.at[nxt, 0], sem.at[nxt, 0], s * per, (s + 1) * per)
        _issue_rows(y_hbm, d2n_ref, buf_ref.at[nxt, 1], sem.at[nxt, 1], s * per, (s + 1) * per)
        y_s = g1 * buf_ref[slot, 0, :, s, :] + g2 * buf_ref[slot, 1, :, s, :]
        ys.append(y_s)
        ssq = ssq + jnp.sum(y_s * y_s, axis=-1, keepdims=True)
    inv = lax.rsqrt(ssq / d + NORM_EPS)
    for s in range(ns):
        sl = slice(s * LANES, (s + 1) * LANES)
        o_ref[:, sl] = x_ref[:, sl] + gate_ref[:, sl] * (ys[s] * inv * gp_ref[:, sl])

    @pl.when(i == pl.num_programs(0) - 1)
    def _():
        _wait_row_gather(y_hbm, buf_ref.at[nxt, 0], sem.at[nxt, 0])
        _wait_row_gather(y_hbm, buf_ref.at[nxt, 1], sem.at[nxt, 1])


def _combine(ybuf3, dest1, dest2, info, x, g_post, gate, seq):
    n, d = x.shape
    tm = GATHER_TILE
    nt = n // tm
    tpb = seq // tm
    cur_spec = pl.BlockSpec((None, 1, tm), lambda i: (i, 0, 0), memory_space=pltpu.SMEM)
    nxt_spec = pl.BlockSpec((None, 1, tm), lambda i: (jnp.minimum(i + 1, nt - 1), 0, 0), memory_space=pltpu.SMEM)
    d1 = dest1.reshape(nt, 1, tm)
    d2 = dest2.reshape(nt, 1, tm)
    return pl.pallas_call(
        _combine_kernel,
        grid=(nt,),
        in_specs=[cur_spec, cur_spec, nxt_spec, nxt_spec, pl.BlockSpec(memory_space=pl.ANY),
                  _row_spec(tm, LANES), _row_spec(tm, d), _const_spec((1, d)), _batch_spec(d, tpb)],
        out_specs=_row_spec(tm, d),
        out_shape=jax.ShapeDtypeStruct((n, d), F32),
        scratch_shapes=[pltpu.VMEM((2, 2, tm, d // LANES, LANES), F32), pltpu.SemaphoreType.DMA((2, 2))],
        compiler_params=_params("arbitrary"),
        name="moe_combine",
    )(d1, d2, d1, d2, ybuf3, info, x, g_post, gate)


def _moe(x, g_pre, shift, scale, router_w, wg, wu, wd, g_post, gate, seq):
    n, d = x.shape
    wr_pad = jnp.pad(router_w, ((0, 0), (0, LANES - N_EXPERTS)))
    h3, info, infot, cnt = _route(x, g_pre, shift, scale, wr_pad, seq)
    e1 = infot[0].astype(jnp.int32)
    e2 = infot[1].astype(jnp.int32)
    r1 = infot[4].astype(jnp.int32)
    r2 = infot[5].astype(jnp.int32)
    counts = cnt[0, :N_EXPERTS].astype(jnp.int32)
    te = EXPERT_TILE
    padded = (counts + te - 1) // te * te
    pend = jnp.cumsum(padded)
    pstart = pend - padded
    dest1 = pstart[e1] + r1
    dest2 = pstart[e2] + r2
    rows = 2 * n + N_EXPERTS * te
    tok = jnp.arange(n, dtype=jnp.int32)
    inv = jnp.zeros((rows,), jnp.int32).at[jnp.concatenate([dest1, dest2])].set(
        jnp.concatenate([tok, tok]), unique_indices=True)
    tile_start = jnp.arange(rows // te, dtype=jnp.int32) * te
    tile_expert = jnp.minimum(jnp.sum((tile_start[:, None] >= pend[None, :]).astype(jnp.int32), axis=1),
                              N_EXPERTS - 1)
    a = _expert_up(h3, inv, wg, wu, tile_expert)
    ybuf3 = _expert_down(a, wd, tile_expert)
    return _combine(ybuf3, dest1, dest2, info, x, g_post, gate, seq)


def _rope_lanes(w):
    half = QK_ROPE_DIM // 2
    z = jnp.zeros(w.shape[:-1] + (LANES // 2 - half,), w.dtype)
    return jnp.concatenate([w[..., :half], z, w[..., half:], z], axis=-1)


def kernel(x, c, positions, ada_w, ada_b, norm_mix_pre, norm_mix_post, norm_ffn_pre, norm_ffn_post, ssm_w_in, ssm_log_step, ssm_a_re, ssm_a_im, ssm_b_re, ssm_b_im, ssm_c_re, ssm_c_im, ssm_d, ssm_w_out, kv_ada_w, kv_ada_b, kv_norm, mla_w_dkv, mla_kv_norm, mla_w_ukv, mla_w_dq, mla_q_norm, mla_w_uq, mla_w_o, ffn_w_gu, ffn_w_down, moe_router, moe_w_gu, moe_w_down):
    bsz, seq, d = x.shape
    depth = ada_w.shape[0]
    n_a = ssm_w_in.shape[0]
    n = bsz * seq
    d_ff = ffn_w_down.shape[1]

    c_pad = jnp.pad(c, ((0, 8 - bsz), (0, 0)))
    ada = _ada_proj(c_pad, ada_w, ada_b[:, None, :], 2048)[:, :bsz]
    kv_ada = _ada_proj(c_pad, kv_ada_w[None], kv_ada_b[None, None, :], 2048)[0, :bsz]

    def vec(a):
        return a[:, None, :]

    def gain(gv):
        return gv[None, :]

    inv_freq = ROPE_THETA ** (-jnp.arange(0, QK_ROPE_DIM, 2, dtype=F32) / QK_ROPE_DIM)
    ang = positions.astype(F32)[..., None] * inv_freq
    cos = jnp.cos(ang).reshape(n, -1)
    sin = jnp.sin(ang).reshape(n, -1)
    zpad = jnp.zeros_like(cos)
    cos_t = jnp.concatenate([cos, zpad, cos, zpad], axis=-1)
    sin_t = jnp.concatenate([-sin, zpad, sin, zpad], axis=-1)

    xs = x.reshape(n, d)
    kn = kr = v = None
    for i in range(depth):
        sh_m, sc_m, g_m, sh_f, sc_f, g_f = [vec(a) for a in jnp.split(ada[i], 6, axis=-1)]
        if i < n_a:
            u3 = _s5_in(xs, gain(norm_mix_pre[i]), sh_m, sc_m, ssm_w_in[i].T.astype(BF16), seq)
            tabs = _s5_conv_tables(ssm_log_step[i], ssm_a_re[i], ssm_a_im[i], ssm_b_re[i], ssm_b_im[i],
                                   ssm_c_re[i], ssm_c_im[i], ssm_d[i], seq // LANES)
            y3 = _s5_conv(u3, *tabs, seq)
            xs = _s5_out(y3, ssm_w_out[i].astype(BF16), xs, gain(norm_mix_post[i]), g_m, seq)
        else:
            if i == n_a:
                kv_sh, kv_sc = [vec(a) for a in jnp.split(kv_ada, 2, axis=-1)]
                w_ukv = mla_w_ukv.reshape(KV_LORA_RANK, N_HEADS, QK_NOPE_DIM + V_HEAD_DIM)
                wk = w_ukv[:, :, :QK_NOPE_DIM].reshape(KV_LORA_RANK, -1).astype(BF16)
                wv = w_ukv[:, :, QK_NOPE_DIM:].reshape(KV_LORA_RANK, -1).astype(BF16)
                wc = mla_w_dkv[:, :KV_LORA_RANK].astype(BF16)
                wr = _rope_lanes(mla_w_dkv[:, KV_LORA_RANK:]).astype(BF16)
                kn, kr, v = _mla_kv(xs, gain(kv_norm), kv_sh, kv_sc, wc, wr, gain(mla_kv_norm), wk, wv,
                                    cos_t, sin_t, seq)
                kn = kn.reshape(bsz, seq, -1)
                kr = kr.reshape(bsz, seq, -1)
                v = v.reshape(bsz, seq, -1)
            j = i - n_a
            w_uq = mla_w_uq[j].reshape(-1, N_HEADS, QK_NOPE_DIM + QK_ROPE_DIM)
            wn = w_uq[:, :, :QK_NOPE_DIM].reshape(w_uq.shape[0], -1).astype(BF16)
            wqr = _rope_lanes(w_uq[:, :, QK_NOPE_DIM:]).reshape(w_uq.shape[0], -1).astype(BF16)
            qn, qr = _mla_q(xs, gain(norm_mix_pre[i]), sh_m, sc_m, mla_w_dq[j].astype(BF16),
                            gain(mla_q_norm[j]), wn, wqr, cos_t, sin_t, seq)
            o = _attention(qn.reshape(bsz, seq, -1), qr.reshape(bsz, seq, -1), kn, kr, v, bsz, seq)
            xs = _mm_post(o.reshape(n, -1), mla_w_o[j].astype(BF16), xs, gain(norm_mix_post[i]), g_m, seq,
                          glu=False)
        if i % 2 == 0:
            w_gu = ffn_w_gu[i // 2]
            a = _ffn_up(xs, gain(norm_ffn_pre[i]), sh_f, sc_f, w_gu[:, :d_ff].astype(BF16),
                        w_gu[:, d_ff:].astype(BF16), seq)
            xs = _mm_post(a, ffn_w_down[i // 2].astype(BF16), xs, gain(norm_ffn_post[i]), g_f, seq, glu=False)
        else:
            wg, wu = _cast_experts(moe_w_gu[i // 2], 2)
            (wd,) = _cast_experts(moe_w_down[i // 2], 1)
            xs = _moe(xs, gain(norm_ffn_pre[i]), sh_f, sc_f, moe_router[i // 2], wg, wu, wd,
                      gain(norm_ffn_post[i]), g_f, seq)
    return xs.reshape(bsz, seq, d)
```

```python
import functools
import math

import jax
import jax.numpy as jnp
from jax import lax
from jax.experimental import pallas as pl
from jax.experimental.pallas import tpu as pltpu

F32 = jnp.float32
BF16 = jnp.bfloat16

NORM_EPS = 1e-6
LANES = 128
SSM_GROUP = 16
SSM_STATE = 64
N_HEADS = 8
QK_NOPE_DIM = 128
QK_ROPE_DIM = 64
V_HEAD_DIM = 128
KV_LORA_RANK = 256
ROPE_THETA = 10000.0
N_EXPERTS = 8
SOFTMAX_SCALE = (QK_NOPE_DIM + QK_ROPE_DIM) ** -0.5
Q_SCALE = SOFTMAX_SCALE * math.log2(math.e)

ROW_TILE = 512
S5_ROW_TILE = 1024
SSM_CHUNK = 128
SSM_CH_BLOCK = 128
ATTN_TILE = 1024
ROUTE_TILE = 256
EXPERT_TILE = 256
GATHER_TILE = 256
FF_CHUNK = 256
CAST_BLOCK_BYTES = 8 * 1024 * 1024


def _params(*sem):
    return pltpu.CompilerParams(dimension_semantics=sem)


def _rms(x, g):
    return x * lax.rsqrt(jnp.mean(x * x, axis=-1, keepdims=True) + NORM_EPS) * g


def _modnorm(x, g, shift, scale):
    return _rms(x, g) * (1.0 + scale) + shift


def _row_spec(tm, d):
    return pl.BlockSpec((tm, d), lambda i: (i, 0))


def _const_spec(shape):
    return pl.BlockSpec(shape, lambda i: tuple(0 for _ in shape))


def _batch_spec(d, tiles_per_batch):
    return pl.BlockSpec((None, 1, d), lambda i: (i // tiles_per_batch, 0, 0))


def _ada_kernel(c_ref, w_ref, b_ref, o_ref):
    c = c_ref[...]
    ca = c * jax.nn.sigmoid(c)
    o_ref[...] = jnp.dot(ca, w_ref[...], preferred_element_type=F32) + b_ref[...]


def _ada_proj(c_pad, w, b, tn):
    nl, d, e = w.shape
    return pl.pallas_call(
        _ada_kernel,
        grid=(nl, e // tn),
        in_specs=[
            pl.BlockSpec((8, d), lambda l, j: (0, 0)),
            pl.BlockSpec((None, d, tn), lambda l, j: (l, 0, j)),
            pl.BlockSpec((None, 1, tn), lambda l, j: (l, 0, j)),
        ],
        out_specs=pl.BlockSpec((None, 8, tn), lambda l, j: (l, 0, j)),
        out_shape=jax.ShapeDtypeStruct((nl, 8, e), F32),
        compiler_params=_params("parallel", "parallel"),
        name="ada_proj",
    )(c_pad, w, b)


def _cast_kernel(w_ref, *o_refs):
    f = o_refs[0].shape[-1]
    for k, o_ref in enumerate(o_refs):
        o_ref[...] = w_ref[:, k * f:(k + 1) * f].astype(o_ref.dtype)


def _cast_experts(w_all, layer, n_split):
    _, e, k, ftot = w_all.shape
    f = ftot // n_split
    tk = max(t for t in range(16, k + 1, 16) if k % t == 0 and t * ftot * 4 <= CAST_BLOCK_BYTES)
    return pl.pallas_call(
        _cast_kernel,
        grid=(e, k // tk),
        in_specs=[pl.BlockSpec((None, None, tk, ftot), lambda a, b: (layer, a, b, 0))],
        out_specs=[pl.BlockSpec((None, tk, f), lambda a, b: (a, b, 0)) for _ in range(n_split)],
        out_shape=[jax.ShapeDtypeStruct((e, k, f), BF16) for _ in range(n_split)],
        compiler_params=_params("parallel", "parallel"),
        name="cast_experts",
    )(w_all)


def _modmm_kernel(x_ref, g_ref, sh_ref, sc_ref, w_ref, o_ref):
    h = _modnorm(x_ref[...], g_ref[...], sh_ref[...], sc_ref[...]).astype(BF16)
    o_ref[...] = jnp.dot(h, w_ref[...], preferred_element_type=F32).astype(o_ref.dtype)


def _modmm(x, g, shift, scale, w, seq, out_dtype):
    n, d = x.shape
    e = w.shape[1]
    tm = ROW_TILE
    tpb = seq // tm
    return pl.pallas_call(
        _modmm_kernel,
        grid=(n // tm,),
        in_specs=[_row_spec(tm, d), _const_spec((1, d)), _batch_spec(d, tpb), _batch_spec(d, tpb),
                  _const_spec((d, e))],
        out_specs=_row_spec(tm, e),
        out_shape=jax.ShapeDtypeStruct((n, e), out_dtype),
        compiler_params=_params("parallel"),
        name="modnorm_matmul",
    )(x, g, shift, scale, w)


def _ffn_up_kernel(x_ref, g_ref, sh_ref, sc_ref, wg_ref, wu_ref, o_ref):
    h = _modnorm(x_ref[...], g_ref[...], sh_ref[...], sc_ref[...]).astype(BF16)
    f = o_ref.shape[1]
    for c in range(0, f, FF_CHUNK):
        gt = jnp.dot(h, wg_ref[:, c:c + FF_CHUNK], preferred_element_type=F32)
        up = jnp.dot(h, wu_ref[:, c:c + FF_CHUNK], preferred_element_type=F32)
        o_ref[:, c:c + FF_CHUNK] = (gt * jax.nn.sigmoid(gt) * up).astype(o_ref.dtype)


def _ffn_up(x, g, shift, scale, wg, wu, seq):
    n, d = x.shape
    f = wg.shape[1]
    tm = ROW_TILE
    tpb = seq // tm
    return pl.pallas_call(
        _ffn_up_kernel,
        grid=(n // tm,),
        in_specs=[_row_spec(tm, d), _const_spec((1, d)), _batch_spec(d, tpb), _batch_spec(d, tpb),
                  _const_spec((d, f)), _const_spec((d, f))],
        out_specs=_row_spec(tm, f),
        out_shape=jax.ShapeDtypeStruct((n, f), BF16),
        compiler_params=_params("parallel"),
        name="ffn_up",
    )(x, g, shift, scale, wg, wu)


def _mm_post_kernel(a_ref, w_ref, x_ref, gp_ref, gate_ref, o_ref, *, glu):
    y = jnp.dot(a_ref[...], w_ref[...], preferred_element_type=F32)
    if glu:
        d = o_ref.shape[1]
        y = y[:, :d] * jax.nn.sigmoid(y[:, d:])
    o_ref[...] = x_ref[...] + gate_ref[...] * _rms(y, gp_ref[...])


def _mm_post(a, w, x, g_post, gate, seq, glu):
    n, k = a.shape
    d = x.shape[1]
    e = w.shape[1]
    tm = ROW_TILE
    tpb = seq // tm
    return pl.pallas_call(
        functools.partial(_mm_post_kernel, glu=glu),
        grid=(n // tm,),
        in_specs=[_row_spec(tm, k), _const_spec((k, e)), _row_spec(tm, d), _const_spec((1, d)),
                  _batch_spec(d, tpb)],
        out_specs=_row_spec(tm, d),
        out_shape=jax.ShapeDtypeStruct((n, d), F32),
        compiler_params=_params("parallel"),
        name="matmul_post",
    )(a, w, x, g_post, gate)


def _s5_kernel(u_ref, b_ref, c_ref, d_ref, apow_ref, tpow_ref, o_ref, st_ref, carry_ref, *, n_steps):
    j = pl.program_id(2)
    t = u_ref.shape[0]
    half = carry_ref.shape[1] // 2
    pad = st_ref.shape[0] - t

    @pl.when(j == 0)
    def _():
        carry_ref[...] = jnp.zeros_like(carry_ref)
        st_ref[0:pad, :] = jnp.zeros((pad, 2 * half), F32)

    u = u_ref[...]
    st_ref[pad:, :] = jnp.dot(u.astype(BF16), b_ref[...], preferred_element_type=F32)
    for k in range(n_steps):
        sft = 1 << k
        ar = apow_ref[k:k + 1, :half]
        ai = apow_ref[k:k + 1, half:]
        cur_re = st_ref[pad:, :half]
        cur_im = st_ref[pad:, half:]
        sh_re = st_ref[pad - sft:pad - sft + t, :half]
        sh_im = st_ref[pad - sft:pad - sft + t, half:]
        st_ref[pad:, :half] = cur_re + (ar * sh_re - ai * sh_im)
        st_ref[pad:, half:] = cur_im + (ar * sh_im + ai * sh_re)
    cr = carry_ref[:, :half]
    ci = carry_ref[:, half:]
    pr = tpow_ref[:, :half]
    pi = tpow_ref[:, half:]
    s_re = st_ref[pad:, :half] + (pr * cr - pi * ci)
    s_im = st_ref[pad:, half:] + (pr * ci + pi * cr)
    carry_ref[:, :half] = s_re[t - 1:t, :]
    carry_ref[:, half:] = s_im[t - 1:t, :]
    st = jnp.concatenate([s_re, s_im], axis=1).astype(BF16)
    y = jnp.dot(st, c_ref[...], preferred_element_type=F32)
    o_ref[...] = jax.nn.gelu(y + d_ref[...] * u).astype(o_ref.dtype)


def _s5_scan(u, bblk, cblk, d_skip, apow, tpow, bsz, seq):
    n, width = u.shape
    cb = SSM_CH_BLOCK
    t = SSM_CHUNK
    nst = bblk.shape[2]
    n_steps = apow.shape[1]
    chunks = seq // t
    return pl.pallas_call(
        functools.partial(_s5_kernel, n_steps=n_steps),
        grid=(bsz, width // cb, chunks),
        in_specs=[
            pl.BlockSpec((t, cb), lambda b, c, j: (b * chunks + j, c)),
            pl.BlockSpec((None, cb, nst), lambda b, c, j: (c, 0, 0)),
            pl.BlockSpec((None, nst, cb), lambda b, c, j: (c, 0, 0)),
            pl.BlockSpec((1, cb), lambda b, c, j: (0, c)),
            pl.BlockSpec((None, n_steps, nst), lambda b, c, j: (c, 0, 0)),
            pl.BlockSpec((None, t, nst), lambda b, c, j: (c, 0, 0)),
        ],
        out_specs=pl.BlockSpec((t, cb), lambda b, c, j: (b * chunks + j, c)),
        out_shape=jax.ShapeDtypeStruct((n, width), BF16),
        scratch_shapes=[pltpu.VMEM((t // 2 + t, nst), F32), pltpu.VMEM((1, nst), F32)],
        compiler_params=_params("parallel", "parallel", "arbitrary"),
        name="s5_scan",
    )(u, bblk, cblk, d_skip, apow, tpow)


def _s5_tables(log_step, a_re, a_im, b_re, b_im, c_re, c_im):
    g, p = a_re.shape
    gpb = SSM_CH_BLOCK // SSM_GROUP
    nblk = g // gpb
    delta = jnp.exp(log_step)[:, None]
    mag = jnp.exp(a_re * delta)
    ang = a_im * delta
    lb_re, lb_im = mag * jnp.cos(ang), mag * jnp.sin(ang)
    den = a_re * a_re + a_im * a_im
    nr, ni = lb_re - 1.0, lb_im
    coef_re = (nr * a_re + ni * a_im) / den
    coef_im = (ni * a_re - nr * a_im) / den
    bb_re = coef_re[..., None] * b_re - coef_im[..., None] * b_im
    bb_im = coef_re[..., None] * b_im + coef_im[..., None] * b_re
    eye = jnp.eye(gpb, dtype=F32)

    def blockdiag_in(m):
        m = m.reshape(nblk, gpb, p, SSM_GROUP)
        return jnp.einsum('ngpc,gh->ngchp', m, eye).reshape(nblk, gpb * SSM_GROUP, gpb * p)

    def blockdiag_out(m):
        m = m.reshape(nblk, gpb, SSM_GROUP, p)
        return jnp.einsum('ngcp,gh->ngphc', m, eye).reshape(nblk, gpb * p, gpb * SSM_GROUP)

    bblk = jnp.concatenate([blockdiag_in(bb_re), blockdiag_in(bb_im)], axis=2).astype(BF16)
    cblk = jnp.concatenate([blockdiag_out(c_re), blockdiag_out(-c_im)], axis=1).astype(BF16)
    lr = lb_re.reshape(nblk, gpb * p)
    li = lb_im.reshape(nblk, gpb * p)
    n_steps = int(math.log2(SSM_CHUNK))
    sq = [(lr, li)]
    for _ in range(n_steps - 1):
        r, i = sq[-1]
        sq.append((r * r - i * i, 2.0 * r * i))
    apow = jnp.stack([jnp.concatenate([r, i], axis=1) for r, i in sq], axis=1)
    pr, pi = lr[:, None, :], li[:, None, :]
    for k in range(n_steps):
        r, i = sq[k]
        r, i = r[:, None, :], i[:, None, :]
        pr, pi = (jnp.concatenate([pr, pr * r - pi * i], axis=1),
                  jnp.concatenate([pi, pr * i + pi * r], axis=1))
    tpow = jnp.concatenate([pr, pi], axis=2)
    return bblk, cblk, apow, tpow


def _s5_in_kernel(x_ref, g_ref, sh_ref, sc_ref, wt_ref, o_ref):
    h = _modnorm(x_ref[...], g_ref[...], sh_ref[...], sc_ref[...]).astype(BF16)
    ut = lax.dot_general(wt_ref[...], h, (((1,), (1,)), ((), ())), preferred_element_type=F32)
    for k in range(o_ref.shape[1]):
        o_ref[:, k, :] = ut[:, k * LANES:(k + 1) * LANES]


def _s5_in(x, g, shift, scale, wt, seq):
    n, d = x.shape
    width = wt.shape[0]
    tm = S5_ROW_TILE
    tpb = seq // tm
    return pl.pallas_call(
        _s5_in_kernel,
        grid=(n // tm,),
        in_specs=[_row_spec(tm, d), _const_spec((1, d)), _batch_spec(d, tpb), _batch_spec(d, tpb),
                  _const_spec((width, d))],
        out_specs=pl.BlockSpec((width, tm // LANES, LANES), lambda i: (0, i, 0)),
        out_shape=jax.ShapeDtypeStruct((width, n // LANES, LANES), F32),
        compiler_params=_params("parallel"),
        name="s5_in",
    )(x, g, shift, scale, wt)


def _s5_conv_kernel(u_ref, k_ref, s_ref, cc_ref, a_ref, d_ref, o_ref, acc_ref, hs_ref, *, chunks_per_seq):
    gc, nc, t = u_ref.shape
    ny = gc * t
    pad = hs_ref.shape[0] - nc
    causal = lax.broadcasted_iota(jnp.int32, (t, t), 1) >= lax.broadcasted_iota(jnp.int32, (t, t), 0)

    def rhs_rows(ci):
        tiles = []
        for c in range(gc):
            lag = jnp.broadcast_to(k_ref[ci, c:c + 1, :], (t, t))
            toep = pltpu.roll(lag, 0, 1, stride=1, stride_axis=0)
            tiles.append(jnp.where(causal, toep, 0.0).astype(BF16))
        tiles.append(s_ref[ci])
        return jnp.concatenate(tiles, axis=1)

    for c0 in range(0, gc, 2):
        lhs = jnp.concatenate([u_ref[c0], u_ref[c0 + 1]], axis=1).astype(BF16)
        rhs = jnp.concatenate([rhs_rows(c0), rhs_rows(c0 + 1)], axis=0)
        part = jnp.dot(lhs, rhs, preferred_element_type=F32)
        if c0 == 0:
            acc_ref[...] = part
        else:
            acc_ref[...] += part

    hs_ref[0:pad, :] = jnp.zeros((pad, LANES), F32)
    hs_ref[pad:, :] = acc_ref[:, ny:]
    jl = lax.broadcasted_iota(jnp.int32, (nc, LANES), 0) & (chunks_per_seq - 1)
    for k in range(a_ref.shape[0] // 2):
        sft = 1 << k
        cur = hs_ref[pad:, :]
        sh = jnp.where(jl >= sft, hs_ref[pad - sft:pad - sft + nc, :], 0.0)
        hs_ref[pad:, :] = (cur + sh * a_ref[2 * k:2 * k + 1, :]
                           + pltpu.roll(sh, LANES // 2, 1) * a_ref[2 * k + 1:2 * k + 2, :])
    h_in = jnp.where(jl >= 1, hs_ref[pad - 1:pad - 1 + nc, :], 0.0).astype(BF16)
    y = acc_ref[:, :ny] + jnp.dot(h_in, cc_ref[...], preferred_element_type=F32)
    for c in range(gc):
        o_ref[c] = jax.nn.gelu(y[:, c * t:(c + 1) * t] + d_ref[c:c + 1, :] * u_ref[c])


def _s5_conv(u3, ktab, stab, cctab, atab, dtab, seq):
    width, nc, t = u3.shape
    gc = SSM_GROUP
    chunks_per_seq = seq // t
    blk = lambda *tail: pl.BlockSpec((None,) + tail, lambda g: (g,) + tuple(0 for _ in tail))
    return pl.pallas_call(
        functools.partial(_s5_conv_kernel, chunks_per_seq=chunks_per_seq),
        grid=(width // gc,),
        in_specs=[pl.BlockSpec((gc, nc, t), lambda g: (g, 0, 0)),
                  blk(gc, gc, t), blk(gc, t, LANES), blk(LANES, gc * t), blk(atab.shape[1], LANES), blk(gc, t)],
        out_specs=pl.BlockSpec((gc, nc, t), lambda g: (g, 0, 0)),
        out_shape=jax.ShapeDtypeStruct((width, nc, t), F32),
        scratch_shapes=[pltpu.VMEM((nc, gc * t + LANES), F32), pltpu.VMEM((chunks_per_seq + nc, LANES), F32)],
        compiler_params=_params("parallel"),
        name="s5_conv",
    )(u3, ktab, stab, cctab, atab, dtab)


def _s5_out_kernel(y_ref, w_ref, x_ref, gp_ref, gate_ref, o_ref, m_ref):
    d = o_ref.shape[1]
    yt = jnp.concatenate([y_ref[:, k, :] for k in range(y_ref.shape[1])], axis=1)
    y = yt.T.astype(BF16)
    for c in range(0, d, FF_CHUNK):
        za = jnp.dot(y, w_ref[:, c:c + FF_CHUNK], preferred_element_type=F32)
        zb = jnp.dot(y, w_ref[:, d + c:d + c + FF_CHUNK], preferred_element_type=F32)
        m_ref[:, c:c + FF_CHUNK] = za * jax.nn.sigmoid(zb)
    o_ref[...] = x_ref[...] + gate_ref[...] * _rms(m_ref[...], gp_ref[...])


def _s5_out(y3, w, x, g_post, gate, seq):
    n, d = x.shape
    width = y3.shape[0]
    tm = S5_ROW_TILE
    tpb = seq // tm
    return pl.pallas_call(
        _s5_out_kernel,
        grid=(n // tm,),
        in_specs=[pl.BlockSpec((width, tm // LANES, LANES), lambda i: (0, i, 0)), _const_spec(w.shape),
                  _row_spec(tm, d), _const_spec((1, d)), _batch_spec(d, tpb)],
        out_specs=_row_spec(tm, d),
        out_shape=jax.ShapeDtypeStruct((n, d), F32),
        scratch_shapes=[pltpu.VMEM((tm, d), F32)],
        compiler_params=_params("parallel"),
        name="s5_out",
    )(y3, w, x, g_post, gate)


def _s5_conv_tables(log_step, a_re, a_im, b_re, b_im, c_re, c_im, d_skip, chunks_per_seq):
    g, p = a_re.shape
    t = LANES
    gc = SSM_GROUP
    hi = lax.Precision.HIGHEST
    delta = jnp.exp(log_step)[:, None]
    mag = jnp.exp(a_re * delta)
    ang = a_im * delta
    lb_re, lb_im = mag * jnp.cos(ang), mag * jnp.sin(ang)
    den = a_re * a_re + a_im * a_im
    nr, ni = lb_re - 1.0, lb_im
    coef_re = (nr * a_re + ni * a_im) / den
    coef_im = (ni * a_re - nr * a_im) / den
    bb_re = coef_re[..., None] * b_re - coef_im[..., None] * b_im
    bb_im = coef_re[..., None] * b_im + coef_im[..., None] * b_re

    def cmul(xr, xi, yr, yi):
        return xr * yr - xi * yi, xr * yi + xi * yr

    kk = jnp.arange(t + 1, dtype=F32)[:, None, None]
    magk = jnp.exp(kk * (a_re * delta)[None])
    angk = kk * ang[None]
    pw_r, pw_i = magk * jnp.cos(angk), magk * jnp.sin(angk)
    pr, pi = pw_r[:t], pw_i[:t]
    p1r, p1i = pw_r[1:], pw_i[1:]
    lt_re, lt_im = pw_r[t], pw_i[t]

    w_re = jnp.einsum('gcp,gpd->gcdp', c_re, bb_re) - jnp.einsum('gcp,gpd->gcdp', c_im, bb_im)
    w_im = jnp.einsum('gcp,gpd->gcdp', c_re, bb_im) + jnp.einsum('gcp,gpd->gcdp', c_im, bb_re)
    ktab = (jnp.einsum('gcdp,kgp->gdck', w_re, pr, precision=hi)
            - jnp.einsum('gcdp,kgp->gdck', w_im, pi, precision=hi))
    rr = pr[::-1].transpose(1, 0, 2)[:, None]
    ri = pi[::-1].transpose(1, 0, 2)[:, None]
    br = bb_re.transpose(0, 2, 1)[:, :, None, :]
    bi = bb_im.transpose(0, 2, 1)[:, :, None, :]
    s_re, s_im = cmul(rr, ri, br, bi)
    stab = jnp.concatenate([s_re, s_im], axis=-1).astype(BF16)
    cr = c_re.transpose(0, 2, 1)[:, :, :, None]
    ci = c_im.transpose(0, 2, 1)[:, :, :, None]
    qr = p1r.transpose(1, 2, 0)[:, :, None, :]
    qi = p1i.transpose(1, 2, 0)[:, :, None, :]
    m_re, m_im = cmul(cr, ci, qr, qi)
    cctab = jnp.concatenate([m_re, -m_im], axis=1).reshape(g, 2 * p, gc * t).astype(BF16)
    rows = []
    ar, ai = lt_re, lt_im
    for _ in range(int(math.log2(chunks_per_seq))):
        rows += [jnp.concatenate([ar, ar], axis=1), jnp.concatenate([-ai, ai], axis=1)]
        ar, ai = cmul(ar, ai, ar, ai)
    atab = jnp.stack(rows, axis=1)
    dtab = jnp.broadcast_to(d_skip.reshape(g, gc, 1), (g, gc, t))
    return ktab, stab, cctab, atab, dtab


def _rope(x, cos, sin):
    return x * cos + pltpu.roll(x, LANES // 2, 1) * sin


def _kv_kernel(x_ref, g_ref, sh_ref, sc_ref, wc_ref, wr_ref, gl_ref, wk_ref, wv_ref, cos_ref, sin_ref,
               kn_ref, kr_ref, v_ref):
    hs = _modnorm(x_ref[...], g_ref[...], sh_ref[...], sc_ref[...]).astype(BF16)
    c = jnp.dot(hs, wc_ref[...], preferred_element_type=F32)
    ckv = _rms(c, gl_ref[...]).astype(BF16)
    kn_ref[...] = jnp.dot(ckv, wk_ref[...], preferred_element_type=F32).astype(BF16)
    v_ref[...] = jnp.dot(ckv, wv_ref[...], preferred_element_type=F32).astype(BF16)
    r = jnp.dot(hs, wr_ref[...], preferred_element_type=F32)
    kr_ref[...] = _rope(r, cos_ref[...], sin_ref[...]).astype(BF16)


def _mla_kv(x, g, shift, scale, wc, wr, gl, wk, wv, cos, sin, seq):
    n, d = x.shape
    tm = ROW_TILE
    tpb = seq // tm
    hk = wk.shape[1]
    hv = wv.shape[1]
    return pl.pallas_call(
        _kv_kernel,
        grid=(n // tm,),
        in_specs=[_row_spec(tm, d), _const_spec((1, d)), _batch_spec(d, tpb), _batch_spec(d, tpb),
                  _const_spec(wc.shape), _const_spec(wr.shape), _const_spec(gl.shape),
                  _const_spec(wk.shape), _const_spec(wv.shape), _row_spec(tm, LANES), _row_spec(tm, LANES)],
        out_specs=[_row_spec(tm, hk), _row_spec(tm, LANES), _row_spec(tm, hv)],
        out_shape=[jax.ShapeDtypeStruct((n, hk), BF16), jax.ShapeDtypeStruct((n, LANES), BF16),
                   jax.ShapeDtypeStruct((n, hv), BF16)],
        compiler_params=_params("parallel"),
        name="mla_kv",
    )(x, g, shift, scale, wc, wr, gl, wk, wv, cos, sin)


def _q_kernel(x_ref, g_ref, sh_ref, sc_ref, wd_ref, gq_ref, wn_ref, wr_ref, cos_ref, sin_ref,
              qn_ref, qr_ref):
    h = _modnorm(x_ref[...], g_ref[...], sh_ref[...], sc_ref[...]).astype(BF16)
    ql = jnp.dot(h, wd_ref[...], preferred_element_type=F32)
    qn = _rms(ql, gq_ref[...]).astype(BF16)
    qn_ref[...] = (jnp.dot(qn, wn_ref[...], preferred_element_type=F32) * Q_SCALE).astype(BF16)
    r = jnp.dot(qn, wr_ref[...], preferred_element_type=F32)
    cos = cos_ref[...] * Q_SCALE
    sin = sin_ref[...] * Q_SCALE
    for hd in range(qr_ref.shape[1] // LANES):
        sl = slice(hd * LANES, (hd + 1) * LANES)
        qr_ref[:, sl] = _rope(r[:, sl], cos, sin).astype(BF16)


def _mla_q(x, g, shift, scale, wd, gq, wn, wr, cos, sin, seq):
    n, d = x.shape
    tm = ROW_TILE
    tpb = seq // tm
    e = wn.shape[1]
    return pl.pallas_call(
        _q_kernel,
        grid=(n // tm,),
        in_specs=[_row_spec(tm, d), _const_spec((1, d)), _batch_spec(d, tpb), _batch_spec(d, tpb),
                  _const_spec(wd.shape), _const_spec(gq.shape), _const_spec(wn.shape), _const_spec(wr.shape),
                  _row_spec(tm, LANES), _row_spec(tm, LANES)],
        out_specs=[_row_spec(tm, e), _row_spec(tm, e)],
        out_shape=[jax.ShapeDtypeStruct((n, e), BF16), jax.ShapeDtypeStruct((n, e), BF16)],
        compiler_params=_params("parallel"),
        name="mla_q",
    )(x, g, shift, scale, wd, gq, wn, wr, cos, sin)


def _attn_kernel(qn_ref, qr_ref, kn_ref, kr_ref, v_ref, o_ref):
    i = pl.program_id(2)
    tq = qn_ref.shape[0]
    tk = tq
    q = jnp.concatenate([qn_ref[...], qr_ref[...]], axis=1)

    def step(j, carry, diagonal):
        m, l, acc = carry
        start = pl.multiple_of(j * tk, tk)
        k = jnp.concatenate([kn_ref[pl.ds(start, tk), :], kr_ref[pl.ds(start, tk), :]], axis=1)
        s = lax.dot_general(q, k, (((1,), (1,)), ((), ())), preferred_element_type=F32)
        if diagonal:
            row = lax.broadcasted_iota(jnp.int32, (tq, tk), 0)
            col = lax.broadcasted_iota(jnp.int32, (tq, tk), 1)
            s = jnp.where(col <= row, s, -jnp.inf)
        m_new = jnp.maximum(m, jnp.max(s, axis=-1, keepdims=True))
        alpha = jnp.exp2(m - m_new)
        p = jnp.exp2(s - m_new)
        l = alpha * l + jnp.sum(p, axis=-1, keepdims=True)
        acc = alpha * acc + jnp.dot(p.astype(BF16), v_ref[pl.ds(start, tk), :], preferred_element_type=F32)
        return m_new, l, acc

    init = (jnp.full((tq, 1), -jnp.inf, F32), jnp.zeros((tq, 1), F32), jnp.zeros((tq, v_ref.shape[1]), F32))
    def pair(jj, c):
        return step(2 * jj + 1, step(2 * jj, c, False), False)

    carry = lax.fori_loop(0, i // 2, pair, init)
    carry = lax.fori_loop(0, i % 2, lambda _, c: step(i - 1, c, False), carry)
    _, l, acc = step(i, carry, diagonal=True)
    o_ref[...] = (acc / l).astype(o_ref.dtype)


def _attention(qn, qr, kn, kr, v, bsz, seq):
    tq = ATTN_TILE
    hd = LANES
    return pl.pallas_call(
        _attn_kernel,
        grid=(bsz, N_HEADS, seq // tq),
        in_specs=[
            pl.BlockSpec((None, tq, hd), lambda b, h, i: (b, i, h)),
            pl.BlockSpec((None, tq, hd), lambda b, h, i: (b, i, h)),
            pl.BlockSpec((None, seq, hd), lambda b, h, i: (b, 0, h)),
            pl.BlockSpec((None, seq, hd), lambda b, h, i: (b, 0, 0)),
            pl.BlockSpec((None, seq, hd), lambda b, h, i: (b, 0, h)),
        ],
        out_specs=pl.BlockSpec((None, tq, hd), lambda b, h, i: (b, i, h)),
        out_shape=jax.ShapeDtypeStruct((bsz, seq, N_HEADS * hd), BF16),
        compiler_params=_params("parallel", "parallel", "parallel"),
        name="attention",
    )(qn, qr, kn, kr, v)


INFO_ROWS = 8


def _route_kernel(x_ref, g_ref, sh_ref, sc_ref, wr_ref, h_ref, info_ref, infot_ref, cnt_ref, carry_ref):
    @pl.when(pl.program_id(0) == 0)
    def _():
        carry_ref[...] = jnp.zeros_like(carry_ref)

    tm = x_ref.shape[0]
    h = _modnorm(x_ref[...], g_ref[...], sh_ref[...], sc_ref[...])
    _store_row_tiles(h_ref, h)
    logits = jnp.dot(h, wr_ref[...], preferred_element_type=F32, precision=lax.Precision.HIGHEST)
    lane = lax.broadcasted_iota(jnp.int32, (tm, LANES), 1).astype(F32)
    neg = -jnp.inf
    lg = jnp.where(lane < N_EXPERTS, logits, neg)
    l1 = jnp.max(lg, axis=-1, keepdims=True)
    e1 = jnp.min(jnp.where(lg == l1, lane, float(LANES)), axis=-1, keepdims=True)
    lg2 = jnp.where(lane == e1, neg, lg)
    l2 = jnp.max(lg2, axis=-1, keepdims=True)
    e2 = jnp.min(jnp.where(lg2 == l2, lane, float(LANES)), axis=-1, keepdims=True)
    tt = jnp.exp(l2 - l1)
    g1 = 1.0 / (1.0 + tt)
    g2 = tt / (1.0 + tt)
    oh1 = lane == e1
    oh2 = lane == e2
    oh = jnp.where(oh1 | oh2, 1.0, 0.0)
    tri = (lax.broadcasted_iota(jnp.int32, (tm, tm), 0) > lax.broadcasted_iota(jnp.int32, (tm, tm), 1))
    cum = jnp.dot(jnp.where(tri, 1.0, 0.0).astype(BF16), oh.astype(BF16),
                  preferred_element_type=F32) + carry_ref[...]
    r1 = jnp.sum(jnp.where(oh1, cum, 0.0), axis=-1, keepdims=True)
    r2 = jnp.sum(jnp.where(oh2, cum, 0.0), axis=-1, keepdims=True)
    carry_ref[...] = carry_ref[...] + jnp.sum(oh, axis=0, keepdims=True)
    cnt_ref[...] = carry_ref[...]
    info = jnp.where(lane == 0, e1,
           jnp.where(lane == 1, e2,
           jnp.where(lane == 2, g1,
           jnp.where(lane == 3, g2,
           jnp.where(lane == 4, r1, r2)))))
    info_ref[...] = info
    infot_ref[...] = info.T[:INFO_ROWS, :]


def _route(x, g, shift, scale, wr_pad, seq):
    n, d = x.shape
    tm = ROUTE_TILE
    tpb = seq // tm
    return pl.pallas_call(
        _route_kernel,
        grid=(n // tm,),
        in_specs=[_row_spec(tm, d), _const_spec((1, d)), _batch_spec(d, tpb), _batch_spec(d, tpb),
                  _const_spec((d, LANES))],
        out_specs=[_row_tile_spec(tm, d), _row_spec(tm, LANES), pl.BlockSpec((INFO_ROWS, tm), lambda i: (0, i)),
                   _const_spec((1, LANES))],
        out_shape=[jax.ShapeDtypeStruct((n, d // LANES, LANES), F32), jax.ShapeDtypeStruct((n, LANES), F32),
                   jax.ShapeDtypeStruct((INFO_ROWS, n), F32), jax.ShapeDtypeStruct((1, LANES), F32)],
        scratch_shapes=[pltpu.VMEM((1, LANES), F32)],
        compiler_params=_params("arbitrary"),
        name="moe_route",
    )(x, g, shift, scale, wr_pad)


def _row_tile_spec(tm, d):
    return pl.BlockSpec((tm, d // LANES, LANES), lambda i, *_: (i, 0, 0))


def _store_row_tiles(ref, val):
    for s in range(ref.shape[1]):
        ref[:, s, :] = val[:, s * LANES:(s + 1) * LANES]


def _load_row_tiles(ref):
    return jnp.concatenate([ref[:, s, :] for s in range(ref.shape[1])], axis=1)


DMA_ISSUE_UNROLL = 8


def _issue_row(src_hbm, idx_ref, dst_ref, sem, r, priority):
    pltpu.make_async_copy(src_hbm.at[pl.ds(idx_ref[0, r], 1)], dst_ref.at[pl.ds(r, 1)], sem).start(
        priority=priority)


def _issue_rows(src_hbm, idx_ref, dst_ref, sem, lo, hi):
    for r in range(lo, hi):
        _issue_row(src_hbm, idx_ref, dst_ref, sem, r, r % 2)


def _start_row_gather(src_hbm, idx_ref, dst_ref, sem):
    def issue(blk, c):
        for u in range(DMA_ISSUE_UNROLL):
            _issue_row(src_hbm, idx_ref, dst_ref, sem, blk * DMA_ISSUE_UNROLL + u, 0)
        return c

    lax.fori_loop(0, dst_ref.shape[0] // DMA_ISSUE_UNROLL, issue, 0)


def _wait_row_gather(src_hbm, dst_ref, sem):
    pltpu.make_async_copy(src_hbm.at[pl.ds(0, dst_ref.shape[0])], dst_ref, sem).wait()


IDX_ROWS = 8


def _idx_blocks(idx, tm):
    return idx.reshape(idx.shape[0] // (IDX_ROWS * tm), IDX_ROWS, tm)


def _idx_spec(tm, nt, ahead):
    def index_map(i, *_):
        return (jnp.minimum(i + ahead, nt - 1) // IDX_ROWS, 0, 0)

    return pl.BlockSpec((None, IDX_ROWS, tm), index_map, memory_space=pltpu.SMEM)


def _idx_row(idx_ref, step):
    return idx_ref.at[pl.ds(step % IDX_ROWS, 1)]


def _expert_up_kernel(te_ref, idx_cur_ref, idx_nxt_ref, h_hbm, wg_ref, wu_ref, o_ref, xbuf_ref, sem):
    i = pl.program_id(0)
    slot = i % 2
    nxt = 1 - slot

    @pl.when(i == 0)
    def _():
        _start_row_gather(h_hbm, _idx_row(idx_cur_ref, i), xbuf_ref.at[0], sem.at[0])

    @pl.when(i + 1 < pl.num_programs(0))
    def _():
        _start_row_gather(h_hbm, _idx_row(idx_nxt_ref, i + 1), xbuf_ref.at[nxt], sem.at[nxt])

    _wait_row_gather(h_hbm, xbuf_ref.at[slot], sem.at[slot])
    h = _load_row_tiles(xbuf_ref.at[slot]).astype(BF16)
    f = o_ref.shape[1]
    for c in range(0, f, FF_CHUNK):
        gt = jnp.dot(h, wg_ref[:, c:c + FF_CHUNK], preferred_element_type=F32)
        up = jnp.dot(h, wu_ref[:, c:c + FF_CHUNK], preferred_element_type=F32)
        o_ref[:, c:c + FF_CHUNK] = (gt * jax.nn.sigmoid(gt) * up).astype(o_ref.dtype)


def _expert_up(h3, inv, wg, wu, tile_expert):
    rows = inv.shape[0]
    _, ns, _ = h3.shape
    d = wg.shape[1]
    f = wg.shape[2]
    tm = EXPERT_TILE
    nt = rows // tm
    idx3 = _idx_blocks(inv, tm)
    return pl.pallas_call(
        _expert_up_kernel,
        grid_spec=pltpu.PrefetchScalarGridSpec(
            num_scalar_prefetch=1,
            grid=(nt,),
            in_specs=[_idx_spec(tm, nt, 0), _idx_spec(tm, nt, 1),
                      pl.BlockSpec(memory_space=pl.ANY),
                      pl.BlockSpec((None, d, f), lambda i, te: (te[i], 0, 0)),
                      pl.BlockSpec((None, d, f), lambda i, te: (te[i], 0, 0))],
            out_specs=pl.BlockSpec((tm, f), lambda i, te: (i, 0)),
            scratch_shapes=[pltpu.VMEM((2, tm, ns, LANES), F32), pltpu.SemaphoreType.DMA((2,))],
        ),
        out_shape=jax.ShapeDtypeStruct((rows, f), BF16),
        compiler_params=_params("arbitrary"),
        name="expert_up",
    )(tile_expert, idx3, idx3, h3, wg, wu)


def _expert_down_kernel(te_ref, a_ref, w_ref, o_ref):
    _store_row_tiles(o_ref, jnp.dot(a_ref[...], w_ref[...], preferred_element_type=F32))


def _expert_down(a, wd, tile_expert):
    rows, f = a.shape
    d = wd.shape[2]
    tm = EXPERT_TILE
    return pl.pallas_call(
        _expert_down_kernel,
        grid_spec=pltpu.PrefetchScalarGridSpec(
            num_scalar_prefetch=1,
            grid=(rows // tm,),
            in_specs=[pl.BlockSpec((tm, f), lambda i, te: (i, 0)),
                      pl.BlockSpec((None, f, d), lambda i, te: (te[i], 0, 0))],
            out_specs=_row_tile_spec(tm, d),
        ),
        out_shape=jax.ShapeDtypeStruct((rows, d // LANES, LANES), F32),
        compiler_params=_params("arbitrary"),
        name="expert_down",
    )(tile_expert, a, wd)


def _combine_kernel(d1c_ref, d2c_ref, d1n_ref, d2n_ref, y_hbm, info_ref, x_ref, gp_ref, gate_ref, o_ref,
                    buf_ref, sem):
    i = pl.program_id(0)
    slot = i % 2
    nxt = 1 - slot
    tm, d = o_ref.shape
    ns = buf_ref.shape[3]
    per = tm // ns

    j = jnp.minimum(i + 1, pl.num_programs(0) - 1)
    d1n = _idx_row(d1n_ref, j)
    d2n = _idx_row(d2n_ref, j)

    @pl.when(i == 0)
    def _():
        _start_row_gather(y_hbm, _idx_row(d1c_ref, i), buf_ref.at[0, 0], sem.at[0, 0])
        _start_row_gather(y_hbm, _idx_row(d2c_ref, i), buf_ref.at[0, 1], sem.at[0, 1])

    _wait_row_gather(y_hbm, buf_ref.at[slot, 0], sem.at[slot, 0])
    _wait_row_gather(y_hbm, buf_ref.at[slot, 1], sem.at[slot, 1])
    info = info_ref[...]
    g1 = info[:, 2:3]
    g2 = info[:, 3:4]
    ys = []
    ssq = jnp.zeros((tm, 1), F32)
    for s in range(ns):
        _issue_rows(y_hbm, d1n, buf_ref.at[nxt, 0], sem.at[nxt, 0], s * per, (s + 1) * per)
        _issue_rows(y_hbm, d2n, buf_ref.at[nxt, 1], sem.at[nxt, 1], s * per, (s + 1) * per)
        y_s = g1 * buf_ref[slot, 0, :, s, :] + g2 * buf_ref[slot, 1, :, s, :]
        ys.append(y_s)
        ssq = ssq + jnp.sum(y_s * y_s, axis=-1, keepdims=True)
    inv = lax.rsqrt(ssq / d + NORM_EPS)
    for s in range(ns):
        sl = slice(s * LANES, (s + 1) * LANES)
        o_ref[:, sl] = x_ref[:, sl] + gate_ref[:, sl] * (ys[s] * inv * gp_ref[:, sl])

    @pl.when(i == pl.num_programs(0) - 1)
    def _():
        _wait_row_gather(y_hbm, buf_ref.at[nxt, 0], sem.at[nxt, 0])
        _wait_row_gather(y_hbm, buf_ref.at[nxt, 1], sem.at[nxt, 1])


def _combine(ybuf3, dest1, dest2, info, x, g_post, gate, seq):
    n, d = x.shape
    tm = GATHER_TILE
    nt = n // tm
    tpb = seq // tm
    cur_spec = _idx_spec(tm, nt, 0)
    nxt_spec = _idx_spec(tm, nt, 1)
    d1 = _idx_blocks(dest1, tm)
    d2 = _idx_blocks(dest2, tm)
    return pl.pallas_call(
        _combine_kernel,
        grid=(nt,),
        in_specs=[cur_spec, cur_spec, nxt_spec, nxt_spec, pl.BlockSpec(memory_space=pl.ANY),
                  _row_spec(tm, LANES), _row_spec(tm, d), _const_spec((1, d)), _batch_spec(d, tpb)],
        out_specs=_row_spec(tm, d),
        out_shape=jax.ShapeDtypeStruct((n, d), F32),
        scratch_shapes=[pltpu.VMEM((2, 2, tm, d // LANES, LANES), F32), pltpu.SemaphoreType.DMA((2, 2))],
        compiler_params=_params("arbitrary"),
        name="moe_combine",
    )(d1, d2, d1, d2, ybuf3, info, x, g_post, gate)


def _moe(x, g_pre, shift, scale, router_w, wg, wu, wd, g_post, gate, seq):
    n, d = x.shape
    wr_pad = jnp.pad(router_w, ((0, 0), (0, LANES - N_EXPERTS)))
    h3, info, infot, cnt = _route(x, g_pre, shift, scale, wr_pad, seq)
    e1 = infot[0].astype(jnp.int32)
    e2 = infot[1].astype(jnp.int32)
    r1 = infot[4].astype(jnp.int32)
    r2 = infot[5].astype(jnp.int32)
    counts = cnt[0, :N_EXPERTS].astype(jnp.int32)
    te = EXPERT_TILE
    padded = (counts + te - 1) // te * te
    pend = jnp.cumsum(padded)
    pstart = pend - padded
    dest1 = pstart[e1] + r1
    dest2 = pstart[e2] + r2
    rows = 2 * n + N_EXPERTS * te
    tok = jnp.arange(n, dtype=jnp.int32)
    inv = jnp.zeros((rows,), jnp.int32).at[jnp.concatenate([dest1, dest2])].set(
        jnp.concatenate([tok, tok]), unique_indices=True)
    tile_start = jnp.arange(rows // te, dtype=jnp.int32) * te
    tile_expert = jnp.minimum(jnp.sum((tile_start[:, None] >= pend[None, :]).astype(jnp.int32), axis=1),
                              N_EXPERTS - 1)
    a = _expert_up(h3, inv, wg, wu, tile_expert)
    ybuf3 = _expert_down(a, wd, tile_expert)
    return _combine(ybuf3, dest1, dest2, info, x, g_post, gate, seq)


def _rope_lanes(w):
    half = QK_ROPE_DIM // 2
    z = jnp.zeros(w.shape[:-1] + (LANES // 2 - half,), w.dtype)
    return jnp.concatenate([w[..., :half], z, w[..., half:], z], axis=-1)


def kernel(x, c, positions, ada_w, ada_b, norm_mix_pre, norm_mix_post, norm_ffn_pre, norm_ffn_post, ssm_w_in, ssm_log_step, ssm_a_re, ssm_a_im, ssm_b_re, ssm_b_im, ssm_c_re, ssm_c_im, ssm_d, ssm_w_out, kv_ada_w, kv_ada_b, kv_norm, mla_w_dkv, mla_kv_norm, mla_w_ukv, mla_w_dq, mla_q_norm, mla_w_uq, mla_w_o, ffn_w_gu, ffn_w_down, moe_router, moe_w_gu, moe_w_down):
    bsz, seq, d = x.shape
    depth = ada_w.shape[0]
    n_a = ssm_w_in.shape[0]
    n = bsz * seq
    d_ff = ffn_w_down.shape[1]

    c_pad = jnp.pad(c, ((0, 8 - bsz), (0, 0)))
    ada = _ada_proj(c_pad, ada_w, ada_b[:, None, :], 2048)[:, :bsz]
    kv_ada = _ada_proj(c_pad, kv_ada_w[None], kv_ada_b[None, None, :], 2048)[0, :bsz]

    def vec(a):
        return a[:, None, :]

    def gain(gv):
        return gv[None, :]

    inv_freq = ROPE_THETA ** (-jnp.arange(0, QK_ROPE_DIM, 2, dtype=F32) / QK_ROPE_DIM)
    ang = positions.astype(F32)[..., None] * inv_freq
    cos = jnp.cos(ang).reshape(n, -1)
    sin = jnp.sin(ang).reshape(n, -1)
    zpad = jnp.zeros_like(cos)
    cos_t = jnp.concatenate([cos, zpad, cos, zpad], axis=-1)
    sin_t = jnp.concatenate([-sin, zpad, sin, zpad], axis=-1)

    xs = x.reshape(n, d)
    kn = kr = v = None
    for i in range(depth):
        sh_m, sc_m, g_m, sh_f, sc_f, g_f = [vec(a) for a in jnp.split(ada[i], 6, axis=-1)]
        if i < n_a:
            u3 = _s5_in(xs, gain(norm_mix_pre[i]), sh_m, sc_m, ssm_w_in[i].T.astype(BF16), seq)
            tabs = _s5_conv_tables(ssm_log_step[i], ssm_a_re[i], ssm_a_im[i], ssm_b_re[i], ssm_b_im[i],
                                   ssm_c_re[i], ssm_c_im[i], ssm_d[i], seq // LANES)
            y3 = _s5_conv(u3, *tabs, seq)
            xs = _s5_out(y3, ssm_w_out[i].astype(BF16), xs, gain(norm_mix_post[i]), g_m, seq)
        else:
            if i == n_a:
                kv_sh, kv_sc = [vec(a) for a in jnp.split(kv_ada, 2, axis=-1)]
                w_ukv = mla_w_ukv.reshape(KV_LORA_RANK, N_HEADS, QK_NOPE_DIM + V_HEAD_DIM)
                wk = w_ukv[:, :, :QK_NOPE_DIM].reshape(KV_LORA_RANK, -1).astype(BF16)
                wv = w_ukv[:, :, QK_NOPE_DIM:].reshape(KV_LORA_RANK, -1).astype(BF16)
                wc = mla_w_dkv[:, :KV_LORA_RANK].astype(BF16)
                wr = _rope_lanes(mla_w_dkv[:, KV_LORA_RANK:]).astype(BF16)
                kn, kr, v = _mla_kv(xs, gain(kv_norm), kv_sh, kv_sc, wc, wr, gain(mla_kv_norm), wk, wv,
                                    cos_t, sin_t, seq)
                kn = kn.reshape(bsz, seq, -1)
                kr = kr.reshape(bsz, seq, -1)
                v = v.reshape(bsz, seq, -1)
            j = i - n_a
            w_uq = mla_w_uq[j].reshape(-1, N_HEADS, QK_NOPE_DIM + QK_ROPE_DIM)
            wn = w_uq[:, :, :QK_NOPE_DIM].reshape(w_uq.shape[0], -1).astype(BF16)
            wqr = _rope_lanes(w_uq[:, :, QK_NOPE_DIM:]).reshape(w_uq.shape[0], -1).astype(BF16)
            qn, qr = _mla_q(xs, gain(norm_mix_pre[i]), sh_m, sc_m, mla_w_dq[j].astype(BF16),
                            gain(mla_q_norm[j]), wn, wqr, cos_t, sin_t, seq)
            o = _attention(qn.reshape(bsz, seq, -1), qr.reshape(bsz, seq, -1), kn, kr, v, bsz, seq)
            xs = _mm_post(o.reshape(n, -1), mla_w_o[j].astype(BF16), xs, gain(norm_mix_post[i]), g_m, seq,
                          glu=False)
        if i % 2 == 0:
            w_gu = ffn_w_gu[i // 2]
            a = _ffn_up(xs, gain(norm_ffn_pre[i]), sh_f, sc_f, w_gu[:, :d_ff].astype(BF16),
                        w_gu[:, d_ff:].astype(BF16), seq)
            xs = _mm_post(a, ffn_w_down[i // 2].astype(BF16), xs, gain(norm_ffn_post[i]), g_f, seq, glu=False)
        else:
            wg, wu = _cast_experts(moe_w_gu, i // 2, 2)
            (wd,) = _cast_experts(moe_w_down, i // 2, 1)
            xs = _moe(xs, gain(norm_ffn_pre[i]), sh_f, sc_f, moe_router[i // 2], wg, wu, wd,
                      gain(norm_ffn_post[i]), g_f, seq)
    return xs.reshape(bsz, seq, d)
```

```python
import functools
import math

import jax
import jax.numpy as jnp
from jax import lax
from jax.experimental import pallas as pl
from jax.experimental.pallas import tpu as pltpu

F32 = jnp.float32
BF16 = jnp.bfloat16

NORM_EPS = 1e-6
LANES = 128
SSM_GROUP = 16
N_HEADS = 8
QK_NOPE_DIM = 128
QK_ROPE_DIM = 64
V_HEAD_DIM = 128
KV_LORA_RANK = 256
ROPE_THETA = 10000.0
N_EXPERTS = 8
SOFTMAX_SCALE = (QK_NOPE_DIM + QK_ROPE_DIM) ** -0.5
Q_SCALE = SOFTMAX_SCALE * math.log2(math.e)

ROW_TILE = 512
S5_ROW_TILE = 1024
ATTN_TILE = 1024
ROUTE_TILE = 256
EXPERT_TILE = 256
GATHER_TILE = 256
FF_CHUNK = 256
CAST_BLOCK_BYTES = 8 * 1024 * 1024


def _params(*sem):
    return pltpu.CompilerParams(dimension_semantics=sem)


def _rms(x, g):
    return x * lax.rsqrt(jnp.mean(x * x, axis=-1, keepdims=True) + NORM_EPS) * g


def _modnorm(x, g, shift, scale):
    return _rms(x, g) * (1.0 + scale) + shift


def _row_spec(tm, d):
    return pl.BlockSpec((tm, d), lambda i: (i, 0))


def _const_spec(shape):
    return pl.BlockSpec(shape, lambda i: tuple(0 for _ in shape))


def _batch_spec(d, tiles_per_batch):
    return pl.BlockSpec((None, 1, d), lambda i: (i // tiles_per_batch, 0, 0))


def _ada_kernel(c_ref, w_ref, b_ref, o_ref):
    c = c_ref[...]
    ca = c * jax.nn.sigmoid(c)
    o_ref[...] = jnp.dot(ca, w_ref[...], preferred_element_type=F32) + b_ref[...]


def _ada_proj(c_pad, w, b, tn):
    nl, d, e = w.shape
    return pl.pallas_call(
        _ada_kernel,
        grid=(nl, e // tn),
        in_specs=[
            pl.BlockSpec((8, d), lambda l, j: (0, 0)),
            pl.BlockSpec((None, d, tn), lambda l, j: (l, 0, j)),
            pl.BlockSpec((None, 1, tn), lambda l, j: (l, 0, j)),
        ],
        out_specs=pl.BlockSpec((None, 8, tn), lambda l, j: (l, 0, j)),
        out_shape=jax.ShapeDtypeStruct((nl, 8, e), F32),
        compiler_params=_params("parallel", "parallel"),
        name="ada_proj",
    )(c_pad, w, b)


def _cast_kernel(w_ref, *o_refs):
    f = o_refs[0].shape[-1]
    for k, o_ref in enumerate(o_refs):
        o_ref[...] = w_ref[:, k * f:(k + 1) * f].astype(o_ref.dtype)


def _cast_experts(w_all, layer, n_split):
    _, e, k, ftot = w_all.shape
    f = ftot // n_split
    tk = max(t for t in range(16, k + 1, 16) if k % t == 0 and t * ftot * 4 <= CAST_BLOCK_BYTES)
    return pl.pallas_call(
        _cast_kernel,
        grid=(e, k // tk),
        in_specs=[pl.BlockSpec((None, None, tk, ftot), lambda a, b: (layer, a, b, 0))],
        out_specs=[pl.BlockSpec((None, tk, f), lambda a, b: (a, b, 0)) for _ in range(n_split)],
        out_shape=[jax.ShapeDtypeStruct((e, k, f), BF16) for _ in range(n_split)],
        compiler_params=_params("parallel", "parallel"),
        name="cast_experts",
    )(w_all)


def _ffn_up_kernel(x_ref, g_ref, sh_ref, sc_ref, wg_ref, wu_ref, o_ref):
    h = _modnorm(x_ref[...], g_ref[...], sh_ref[...], sc_ref[...]).astype(BF16)
    f = o_ref.shape[1]
    for c in range(0, f, FF_CHUNK):
        gt = jnp.dot(h, wg_ref[:, c:c + FF_CHUNK], preferred_element_type=F32)
        up = jnp.dot(h, wu_ref[:, c:c + FF_CHUNK], preferred_element_type=F32)
        o_ref[:, c:c + FF_CHUNK] = (gt * jax.nn.sigmoid(gt) * up).astype(o_ref.dtype)


def _ffn_up(x, g, shift, scale, wg, wu, seq):
    n, d = x.shape
    f = wg.shape[1]
    tm = ROW_TILE
    tpb = seq // tm
    return pl.pallas_call(
        _ffn_up_kernel,
        grid=(n // tm,),
        in_specs=[_row_spec(tm, d), _const_spec((1, d)), _batch_spec(d, tpb), _batch_spec(d, tpb),
                  _const_spec((d, f)), _const_spec((d, f))],
        out_specs=_row_spec(tm, f),
        out_shape=jax.ShapeDtypeStruct((n, f), BF16),
        compiler_params=_params("parallel"),
        name="ffn_up",
    )(x, g, shift, scale, wg, wu)


def _mm_post_kernel(a_ref, w_ref, x_ref, gp_ref, gate_ref, o_ref):
    y = jnp.dot(a_ref[...], w_ref[...], preferred_element_type=F32)
    o_ref[...] = x_ref[...] + gate_ref[...] * _rms(y, gp_ref[...])


def _mm_post(a, w, x, g_post, gate, seq):
    n, k = a.shape
    d = x.shape[1]
    e = w.shape[1]
    tm = ROW_TILE
    tpb = seq // tm
    return pl.pallas_call(
        _mm_post_kernel,
        grid=(n // tm,),
        in_specs=[_row_spec(tm, k), _const_spec((k, e)), _row_spec(tm, d), _const_spec((1, d)),
                  _batch_spec(d, tpb)],
        out_specs=_row_spec(tm, d),
        out_shape=jax.ShapeDtypeStruct((n, d), F32),
        compiler_params=_params("parallel"),
        name="matmul_post",
    )(a, w, x, g_post, gate)


def _s5_in_kernel(x_ref, g_ref, sh_ref, sc_ref, wt_ref, o_ref):
    h = _modnorm(x_ref[...], g_ref[...], sh_ref[...], sc_ref[...]).astype(BF16)
    ut = lax.dot_general(wt_ref[...], h, (((1,), (1,)), ((), ())), preferred_element_type=F32)
    for k in range(o_ref.shape[1]):
        o_ref[:, k, :] = ut[:, k * LANES:(k + 1) * LANES]


def _s5_in(x, g, shift, scale, wt, seq):
    n, d = x.shape
    width = wt.shape[0]
    tm = S5_ROW_TILE
    tpb = seq // tm
    return pl.pallas_call(
        _s5_in_kernel,
        grid=(n // tm,),
        in_specs=[_row_spec(tm, d), _const_spec((1, d)), _batch_spec(d, tpb), _batch_spec(d, tpb),
                  _const_spec((width, d))],
        out_specs=pl.BlockSpec((width, tm // LANES, LANES), lambda i: (0, i, 0)),
        out_shape=jax.ShapeDtypeStruct((width, n // LANES, LANES), F32),
        compiler_params=_params("parallel"),
        name="s5_in",
    )(x, g, shift, scale, wt)


def _s5_conv_kernel(u_ref, k_ref, s_ref, cc_ref, a_ref, d_ref, o_ref, acc_ref, hs_ref, *, chunks_per_seq):
    gc, nc, t = u_ref.shape
    ny = gc * t
    pad = hs_ref.shape[0] - nc
    causal = lax.broadcasted_iota(jnp.int32, (t, t), 1) >= lax.broadcasted_iota(jnp.int32, (t, t), 0)

    def rhs_rows(ci):
        tiles = []
        for c in range(gc):
            lag = jnp.broadcast_to(k_ref[ci, c:c + 1, :], (t, t))
            toep = pltpu.roll(lag, 0, 1, stride=1, stride_axis=0)
            tiles.append(jnp.where(causal, toep, 0.0).astype(BF16))
        tiles.append(s_ref[ci])
        return jnp.concatenate(tiles, axis=1)

    for c0 in range(0, gc, 2):
        lhs = jnp.concatenate([u_ref[c0], u_ref[c0 + 1]], axis=1).astype(BF16)
        rhs = jnp.concatenate([rhs_rows(c0), rhs_rows(c0 + 1)], axis=0)
        part = jnp.dot(lhs, rhs, preferred_element_type=F32)
        if c0 == 0:
            acc_ref[...] = part
        else:
            acc_ref[...] += part

    hs_ref[0:pad, :] = jnp.zeros((pad, LANES), F32)
    hs_ref[pad:, :] = acc_ref[:, ny:]
    jl = lax.broadcasted_iota(jnp.int32, (nc, LANES), 0) & (chunks_per_seq - 1)
    for k in range(a_ref.shape[0] // 2):
        sft = 1 << k
        cur = hs_ref[pad:, :]
        sh = jnp.where(jl >= sft, hs_ref[pad - sft:pad - sft + nc, :], 0.0)
        hs_ref[pad:, :] = (cur + sh * a_ref[2 * k:2 * k + 1, :]
                           + pltpu.roll(sh, LANES // 2, 1) * a_ref[2 * k + 1:2 * k + 2, :])
    h_in = jnp.where(jl >= 1, hs_ref[pad - 1:pad - 1 + nc, :], 0.0).astype(BF16)
    y = acc_ref[:, :ny] + jnp.dot(h_in, cc_ref[...], preferred_element_type=F32)
    for c in range(gc):
        o_ref[c] = jax.nn.gelu(y[:, c * t:(c + 1) * t] + d_ref[c:c + 1, :] * u_ref[c])


def _s5_conv(u3, ktab, stab, cctab, atab, dtab, seq):
    width, nc, t = u3.shape
    gc = SSM_GROUP
    chunks_per_seq = seq // t
    blk = lambda *tail: pl.BlockSpec((None,) + tail, lambda g: (g,) + tuple(0 for _ in tail))
    return pl.pallas_call(
        functools.partial(_s5_conv_kernel, chunks_per_seq=chunks_per_seq),
        grid=(width // gc,),
        in_specs=[pl.BlockSpec((gc, nc, t), lambda g: (g, 0, 0)),
                  blk(gc, gc, t), blk(gc, t, LANES), blk(LANES, gc * t), blk(atab.shape[1], LANES), blk(gc, t)],
        out_specs=pl.BlockSpec((gc, nc, t), lambda g: (g, 0, 0)),
        out_shape=jax.ShapeDtypeStruct((width, nc, t), F32),
        scratch_shapes=[pltpu.VMEM((nc, gc * t + LANES), F32), pltpu.VMEM((chunks_per_seq + nc, LANES), F32)],
        compiler_params=_params("parallel"),
        name="s5_conv",
    )(u3, ktab, stab, cctab, atab, dtab)


def _s5_out_kernel(y_ref, w_ref, x_ref, gp_ref, gate_ref, o_ref, m_ref):
    d = o_ref.shape[1]
    yt = jnp.concatenate([y_ref[:, k, :] for k in range(y_ref.shape[1])], axis=1)
    y = yt.T.astype(BF16)
    for c in range(0, d, FF_CHUNK):
        za = jnp.dot(y, w_ref[:, c:c + FF_CHUNK], preferred_element_type=F32)
        zb = jnp.dot(y, w_ref[:, d + c:d + c + FF_CHUNK], preferred_element_type=F32)
        m_ref[:, c:c + FF_CHUNK] = za * jax.nn.sigmoid(zb)
    o_ref[...] = x_ref[...] + gate_ref[...] * _rms(m_ref[...], gp_ref[...])


def _s5_out(y3, w, x, g_post, gate, seq):
    n, d = x.shape
    width = y3.shape[0]
    tm = S5_ROW_TILE
    tpb = seq // tm
    return pl.pallas_call(
        _s5_out_kernel,
        grid=(n // tm,),
        in_specs=[pl.BlockSpec((width, tm // LANES, LANES), lambda i: (0, i, 0)), _const_spec(w.shape),
                  _row_spec(tm, d), _const_spec((1, d)), _batch_spec(d, tpb)],
        out_specs=_row_spec(tm, d),
        out_shape=jax.ShapeDtypeStruct((n, d), F32),
        scratch_shapes=[pltpu.VMEM((tm, d), F32)],
        compiler_params=_params("parallel"),
        name="s5_out",
    )(y3, w, x, g_post, gate)


def _s5_conv_tables(log_step, a_re, a_im, b_re, b_im, c_re, c_im, d_skip, chunks_per_seq):
    g, p = a_re.shape
    t = LANES
    gc = SSM_GROUP
    hi = lax.Precision.HIGHEST
    delta = jnp.exp(log_step)[:, None]
    mag = jnp.exp(a_re * delta)
    ang = a_im * delta
    lb_re, lb_im = mag * jnp.cos(ang), mag * jnp.sin(ang)
    den = a_re * a_re + a_im * a_im
    nr, ni = lb_re - 1.0, lb_im
    coef_re = (nr * a_re + ni * a_im) / den
    coef_im = (ni * a_re - nr * a_im) / den
    bb_re = coef_re[..., None] * b_re - coef_im[..., None] * b_im
    bb_im = coef_re[..., None] * b_im + coef_im[..., None] * b_re

    def cmul(xr, xi, yr, yi):
        return xr * yr - xi * yi, xr * yi + xi * yr

    kk = jnp.arange(t + 1, dtype=F32)[:, None, None]
    magk = jnp.exp(kk * (a_re * delta)[None])
    angk = kk * ang[None]
    pw_r, pw_i = magk * jnp.cos(angk), magk * jnp.sin(angk)
    pr, pi = pw_r[:t], pw_i[:t]
    p1r, p1i = pw_r[1:], pw_i[1:]
    lt_re, lt_im = pw_r[t], pw_i[t]

    w_re = jnp.einsum('gcp,gpd->gcdp', c_re, bb_re) - jnp.einsum('gcp,gpd->gcdp', c_im, bb_im)
    w_im = jnp.einsum('gcp,gpd->gcdp', c_re, bb_im) + jnp.einsum('gcp,gpd->gcdp', c_im, bb_re)
    ktab = (jnp.einsum('gcdp,kgp->gdck', w_re, pr, precision=hi)
            - jnp.einsum('gcdp,kgp->gdck', w_im, pi, precision=hi))
    rr = pr[::-1].transpose(1, 0, 2)[:, None]
    ri = pi[::-1].transpose(1, 0, 2)[:, None]
    br = bb_re.transpose(0, 2, 1)[:, :, None, :]
    bi = bb_im.transpose(0, 2, 1)[:, :, None, :]
    s_re, s_im = cmul(rr, ri, br, bi)
    stab = jnp.concatenate([s_re, s_im], axis=-1).astype(BF16)
    cr = c_re.transpose(0, 2, 1)[:, :, :, None]
    ci = c_im.transpose(0, 2, 1)[:, :, :, None]
    qr = p1r.transpose(1, 2, 0)[:, :, None, :]
    qi = p1i.transpose(1, 2, 0)[:, :, None, :]
    m_re, m_im = cmul(cr, ci, qr, qi)
    cctab = jnp.concatenate([m_re, -m_im], axis=1).reshape(g, 2 * p, gc * t).astype(BF16)
    rows = []
    ar, ai = lt_re, lt_im
    for _ in range(int(math.log2(chunks_per_seq))):
        rows += [jnp.concatenate([ar, ar], axis=1), jnp.concatenate([-ai, ai], axis=1)]
        ar, ai = cmul(ar, ai, ar, ai)
    atab = jnp.stack(rows, axis=1)
    dtab = jnp.broadcast_to(d_skip.reshape(g, gc, 1), (g, gc, t))
    return ktab, stab, cctab, atab, dtab


def _rope(x, cos, sin):
    return x * cos + pltpu.roll(x, LANES // 2, 1) * sin


def _kv_kernel(x_ref, g_ref, sh_ref, sc_ref, wc_ref, wr_ref, gl_ref, wk_ref, wv_ref, cos_ref, sin_ref,
               kn_ref, kr_ref, v_ref):
    hs = _modnorm(x_ref[...], g_ref[...], sh_ref[...], sc_ref[...]).astype(BF16)
    c = jnp.dot(hs, wc_ref[...], preferred_element_type=F32)
    ckv = _rms(c, gl_ref[...]).astype(BF16)
    kn_ref[...] = jnp.dot(ckv, wk_ref[...], preferred_element_type=F32).astype(BF16)
    v_ref[...] = jnp.dot(ckv, wv_ref[...], preferred_element_type=F32).astype(BF16)
    r = jnp.dot(hs, wr_ref[...], preferred_element_type=F32)
    kr_ref[...] = _rope(r, cos_ref[...], sin_ref[...]).astype(BF16)


def _mla_kv(x, g, shift, scale, wc, wr, gl, wk, wv, cos, sin, seq):
    n, d = x.shape
    tm = ROW_TILE
    tpb = seq // tm
    hk = wk.shape[1]
    hv = wv.shape[1]
    return pl.pallas_call(
        _kv_kernel,
        grid=(n // tm,),
        in_specs=[_row_spec(tm, d), _const_spec((1, d)), _batch_spec(d, tpb), _batch_spec(d, tpb),
                  _const_spec(wc.shape), _const_spec(wr.shape), _const_spec(gl.shape),
                  _const_spec(wk.shape), _const_spec(wv.shape), _row_spec(tm, LANES), _row_spec(tm, LANES)],
        out_specs=[_row_spec(tm, hk), _row_spec(tm, LANES), _row_spec(tm, hv)],
        out_shape=[jax.ShapeDtypeStruct((n, hk), BF16), jax.ShapeDtypeStruct((n, LANES), BF16),
                   jax.ShapeDtypeStruct((n, hv), BF16)],
        compiler_params=_params("parallel"),
        name="mla_kv",
    )(x, g, shift, scale, wc, wr, gl, wk, wv, cos, sin)


def _q_kernel(x_ref, g_ref, sh_ref, sc_ref, wd_ref, gq_ref, wn_ref, wr_ref, cos_ref, sin_ref,
              qn_ref, qr_ref):
    h = _modnorm(x_ref[...], g_ref[...], sh_ref[...], sc_ref[...]).astype(BF16)
    ql = jnp.dot(h, wd_ref[...], preferred_element_type=F32)
    qn = _rms(ql, gq_ref[...]).astype(BF16)
    qn_ref[...] = (jnp.dot(qn, wn_ref[...], preferred_element_type=F32) * Q_SCALE).astype(BF16)
    r = jnp.dot(qn, wr_ref[...], preferred_element_type=F32)
    cos = cos_ref[...] * Q_SCALE
    sin = sin_ref[...] * Q_SCALE
    for hd in range(qr_ref.shape[1] // LANES):
        sl = slice(hd * LANES, (hd + 1) * LANES)
        qr_ref[:, sl] = _rope(r[:, sl], cos, sin).astype(BF16)


def _mla_q(x, g, shift, scale, wd, gq, wn, wr, cos, sin, seq):
    n, d = x.shape
    tm = ROW_TILE
    tpb = seq // tm
    e = wn.shape[1]
    return pl.pallas_call(
        _q_kernel,
        grid=(n // tm,),
        in_specs=[_row_spec(tm, d), _const_spec((1, d)), _batch_spec(d, tpb), _batch_spec(d, tpb),
                  _const_spec(wd.shape), _const_spec(gq.shape), _const_spec(wn.shape), _const_spec(wr.shape),
                  _row_spec(tm, LANES), _row_spec(tm, LANES)],
        out_specs=[_row_spec(tm, e), _row_spec(tm, e)],
        out_shape=[jax.ShapeDtypeStruct((n, e), BF16), jax.ShapeDtypeStruct((n, e), BF16)],
        compiler_params=_params("parallel"),
        name="mla_q",
    )(x, g, shift, scale, wd, gq, wn, wr, cos, sin)


def _attn_kernel(qn_ref, qr_ref, kn_ref, kr_ref, v_ref, o_ref):
    i = pl.program_id(2)
    tq = qn_ref.shape[0]
    tk = tq
    q = jnp.concatenate([qn_ref[...], qr_ref[...]], axis=1)

    def step(j, carry, diagonal):
        m, l, acc = carry
        start = pl.multiple_of(j * tk, tk)
        k = jnp.concatenate([kn_ref[pl.ds(start, tk), :], kr_ref[pl.ds(start, tk), :]], axis=1)
        s = lax.dot_general(q, k, (((1,), (1,)), ((), ())), preferred_element_type=F32)
        if diagonal:
            row = lax.broadcasted_iota(jnp.int32, (tq, tk), 0)
            col = lax.broadcasted_iota(jnp.int32, (tq, tk), 1)
            s = jnp.where(col <= row, s, -jnp.inf)
        m_new = jnp.maximum(m, jnp.max(s, axis=-1, keepdims=True))
        alpha = jnp.exp2(m - m_new)
        p = jnp.exp2(s - m_new)
        l = alpha * l + jnp.sum(p, axis=-1, keepdims=True)
        acc = alpha * acc + jnp.dot(p.astype(BF16), v_ref[pl.ds(start, tk), :], preferred_element_type=F32)
        return m_new, l, acc

    init = (jnp.full((tq, 1), -jnp.inf, F32), jnp.zeros((tq, 1), F32), jnp.zeros((tq, v_ref.shape[1]), F32))
    def pair(jj, c):
        return step(2 * jj + 1, step(2 * jj, c, False), False)

    carry = lax.fori_loop(0, i // 2, pair, init)
    carry = lax.fori_loop(0, i % 2, lambda _, c: step(i - 1, c, False), carry)
    _, l, acc = step(i, carry, diagonal=True)
    o_ref[...] = (acc / l).astype(o_ref.dtype)


def _attention(qn, qr, kn, kr, v, bsz, seq):
    tq = ATTN_TILE
    hd = LANES
    return pl.pallas_call(
        _attn_kernel,
        grid=(bsz, N_HEADS, seq // tq),
        in_specs=[
            pl.BlockSpec((None, tq, hd), lambda b, h, i: (b, i, h)),
            pl.BlockSpec((None, tq, hd), lambda b, h, i: (b, i, h)),
            pl.BlockSpec((None, seq, hd), lambda b, h, i: (b, 0, h)),
            pl.BlockSpec((None, seq, hd), lambda b, h, i: (b, 0, 0)),
            pl.BlockSpec((None, seq, hd), lambda b, h, i: (b, 0, h)),
        ],
        out_specs=pl.BlockSpec((None, tq, hd), lambda b, h, i: (b, i, h)),
        out_shape=jax.ShapeDtypeStruct((bsz, seq, N_HEADS * hd), BF16),
        compiler_params=_params("parallel", "parallel", "parallel"),
        name="attention",
    )(qn, qr, kn, kr, v)


IDX_ROWS = 8
ROUTE_FIELD_LANES = (0, 1, 4, 5)


def _route_kernel(x_ref, g_ref, sh_ref, sc_ref, wr_ref, h_ref, info_ref, fld_ref, cnt_ref, carry_ref):
    @pl.when(pl.program_id(0) == 0)
    def _():
        carry_ref[...] = jnp.zeros_like(carry_ref)

    tm = x_ref.shape[0]
    h = _modnorm(x_ref[...], g_ref[...], sh_ref[...], sc_ref[...])
    _store_row_tiles(h_ref, h)
    logits = jnp.dot(h, wr_ref[...], preferred_element_type=F32, precision=lax.Precision.HIGHEST)
    lane = lax.broadcasted_iota(jnp.int32, (tm, LANES), 1).astype(F32)
    neg = -jnp.inf
    lg = jnp.where(lane < N_EXPERTS, logits, neg)
    l1 = jnp.max(lg, axis=-1, keepdims=True)
    e1 = jnp.min(jnp.where(lg == l1, lane, float(LANES)), axis=-1, keepdims=True)
    lg2 = jnp.where(lane == e1, neg, lg)
    l2 = jnp.max(lg2, axis=-1, keepdims=True)
    e2 = jnp.min(jnp.where(lg2 == l2, lane, float(LANES)), axis=-1, keepdims=True)
    tt = jnp.exp(l2 - l1)
    g1 = 1.0 / (1.0 + tt)
    g2 = tt / (1.0 + tt)
    oh1 = lane == e1
    oh2 = lane == e2
    oh = jnp.where(oh1 | oh2, 1.0, 0.0)
    tri = (lax.broadcasted_iota(jnp.int32, (tm, tm), 0) > lax.broadcasted_iota(jnp.int32, (tm, tm), 1))
    cum = jnp.dot(jnp.where(tri, 1.0, 0.0).astype(BF16), oh.astype(BF16),
                  preferred_element_type=F32) + carry_ref[...]
    r1 = jnp.sum(jnp.where(oh1, cum, 0.0), axis=-1, keepdims=True)
    r2 = jnp.sum(jnp.where(oh2, cum, 0.0), axis=-1, keepdims=True)
    carry_ref[...] = carry_ref[...] + jnp.sum(oh, axis=0, keepdims=True)
    cnt_ref[...] = carry_ref[...]
    info = jnp.where(lane == 0, e1,
           jnp.where(lane == 1, e2,
           jnp.where(lane == 2, g1,
           jnp.where(lane == 3, g2,
           jnp.where(lane == 4, r1, r2)))))
    info_ref[...] = info
    info_t = info.T
    row = pl.program_id(0) % IDX_ROWS
    for k, lane_k in enumerate(ROUTE_FIELD_LANES):
        fld_ref[k, pl.ds(row, 1), :] = info_t[lane_k:lane_k + 1, :]


def _route(x, g, shift, scale, wr_pad, seq):
    n, d = x.shape
    tm = ROUTE_TILE
    tpb = seq // tm
    return pl.pallas_call(
        _route_kernel,
        grid=(n // tm,),
        in_specs=[_row_spec(tm, d), _const_spec((1, d)), _batch_spec(d, tpb), _batch_spec(d, tpb),
                  _const_spec((d, LANES))],
        out_specs=[_row_tile_spec(tm, d), _row_spec(tm, LANES),
                   pl.BlockSpec((len(ROUTE_FIELD_LANES), IDX_ROWS, tm), lambda i: (0, i // IDX_ROWS, 0)),
                   _const_spec((1, LANES))],
        out_shape=[jax.ShapeDtypeStruct((n, d // LANES, LANES), F32), jax.ShapeDtypeStruct((n, LANES), F32),
                   jax.ShapeDtypeStruct((len(ROUTE_FIELD_LANES), n // tm, tm), F32),
                   jax.ShapeDtypeStruct((1, LANES), F32)],
        scratch_shapes=[pltpu.VMEM((1, LANES), F32)],
        compiler_params=_params("arbitrary"),
        name="moe_route",
    )(x, g, shift, scale, wr_pad)


def _row_tile_spec(tm, d):
    return pl.BlockSpec((tm, d // LANES, LANES), lambda i, *_: (i, 0, 0))


def _store_row_tiles(ref, val):
    for s in range(ref.shape[1]):
        ref[:, s, :] = val[:, s * LANES:(s + 1) * LANES]


def _load_row_tiles(ref):
    return jnp.concatenate([ref[:, s, :] for s in range(ref.shape[1])], axis=1)


DMA_ISSUE_UNROLL = 8


def _issue_row(src_hbm, idx_ref, dst_ref, sem, r, priority):
    pltpu.make_async_copy(src_hbm.at[pl.ds(idx_ref[0, r], 1)], dst_ref.at[pl.ds(r, 1)], sem).start(
        priority=priority)


def _issue_rows(src_hbm, idx_ref, dst_ref, sem, lo, hi):
    for r in range(lo, hi):
        _issue_row(src_hbm, idx_ref, dst_ref, sem, r, r % 2)


def _start_row_gather(src_hbm, idx_ref, dst_ref, sem):
    def issue(blk, c):
        for u in range(DMA_ISSUE_UNROLL):
            _issue_row(src_hbm, idx_ref, dst_ref, sem, blk * DMA_ISSUE_UNROLL + u, 0)
        return c

    lax.fori_loop(0, dst_ref.shape[0] // DMA_ISSUE_UNROLL, issue, 0)


def _wait_row_gather(src_hbm, dst_ref, sem):
    pltpu.make_async_copy(src_hbm.at[pl.ds(0, dst_ref.shape[0])], dst_ref, sem).wait()


def _idx_spec(tm, nt, ahead):
    def index_map(i, *_):
        return (jnp.minimum(i + ahead, nt - 1) // IDX_ROWS, 0)

    return pl.BlockSpec((IDX_ROWS, tm), index_map, memory_space=pltpu.SMEM)


def _idx_row(idx_ref, step):
    return idx_ref.at[pl.ds(step % IDX_ROWS, 1)]


def _expert_up_kernel(te_ref, idx_cur_ref, idx_nxt_ref, h_hbm, wg_ref, wu_ref, o_ref, xbuf_ref, sem):
    i = pl.program_id(0)
    slot = i % 2
    nxt = 1 - slot

    @pl.when(i == 0)
    def _():
        _start_row_gather(h_hbm, _idx_row(idx_cur_ref, i), xbuf_ref.at[0], sem.at[0])

    @pl.when(i + 1 < pl.num_programs(0))
    def _():
        _start_row_gather(h_hbm, _idx_row(idx_nxt_ref, i + 1), xbuf_ref.at[nxt], sem.at[nxt])

    _wait_row_gather(h_hbm, xbuf_ref.at[slot], sem.at[slot])
    h = _load_row_tiles(xbuf_ref.at[slot]).astype(BF16)
    f = o_ref.shape[1]
    for c in range(0, f, FF_CHUNK):
        gt = jnp.dot(h, wg_ref[:, c:c + FF_CHUNK], preferred_element_type=F32)
        up = jnp.dot(h, wu_ref[:, c:c + FF_CHUNK], preferred_element_type=F32)
        o_ref[:, c:c + FF_CHUNK] = (gt * jax.nn.sigmoid(gt) * up).astype(o_ref.dtype)


def _expert_up(h3, inv, wg, wu, tile_expert):
    nt, tm = inv.shape
    rows = nt * tm
    _, ns, _ = h3.shape
    d = wg.shape[1]
    f = wg.shape[2]
    idx3 = inv
    return pl.pallas_call(
        _expert_up_kernel,
        grid_spec=pltpu.PrefetchScalarGridSpec(
            num_scalar_prefetch=1,
            grid=(nt,),
            in_specs=[_idx_spec(tm, nt, 0), _idx_spec(tm, nt, 1),
                      pl.BlockSpec(memory_space=pl.ANY),
                      pl.BlockSpec((None, d, f), lambda i, te: (te[i], 0, 0)),
                      pl.BlockSpec((None, d, f), lambda i, te: (te[i], 0, 0))],
            out_specs=pl.BlockSpec((tm, f), lambda i, te: (i, 0)),
            scratch_shapes=[pltpu.VMEM((2, tm, ns, LANES), F32), pltpu.SemaphoreType.DMA((2,))],
        ),
        out_shape=jax.ShapeDtypeStruct((rows, f), BF16),
        compiler_params=_params("arbitrary"),
        name="expert_up",
    )(tile_expert, idx3, idx3, h3, wg, wu)


def _expert_down_kernel(te_ref, a_ref, w_ref, o_ref):
    _store_row_tiles(o_ref, jnp.dot(a_ref[...], w_ref[...], preferred_element_type=F32))


def _expert_down(a, wd, tile_expert):
    rows, f = a.shape
    d = wd.shape[2]
    tm = EXPERT_TILE
    return pl.pallas_call(
        _expert_down_kernel,
        grid_spec=pltpu.PrefetchScalarGridSpec(
            num_scalar_prefetch=1,
            grid=(rows // tm,),
            in_specs=[pl.BlockSpec((tm, f), lambda i, te: (i, 0)),
                      pl.BlockSpec((None, f, d), lambda i, te: (te[i], 0, 0))],
            out_specs=_row_tile_spec(tm, d),
        ),
        out_shape=jax.ShapeDtypeStruct((rows, d // LANES, LANES), F32),
        compiler_params=_params("arbitrary"),
        name="expert_down",
    )(tile_expert, a, wd)


def _combine_kernel(d1c_ref, d2c_ref, d1n_ref, d2n_ref, y_hbm, info_ref, x_ref, gp_ref, gate_ref, o_ref,
                    buf_ref, sem):
    i = pl.program_id(0)
    slot = i % 2
    nxt = 1 - slot
    tm, d = o_ref.shape
    ns = buf_ref.shape[3]
    per = tm // ns

    j = jnp.minimum(i + 1, pl.num_programs(0) - 1)
    d1n = _idx_row(d1n_ref, j)
    d2n = _idx_row(d2n_ref, j)

    @pl.when(i == 0)
    def _():
        _start_row_gather(y_hbm, _idx_row(d1c_ref, i), buf_ref.at[0, 0], sem.at[0, 0])
        _start_row_gather(y_hbm, _idx_row(d2c_ref, i), buf_ref.at[0, 1], sem.at[0, 1])

    _wait_row_gather(y_hbm, buf_ref.at[slot, 0], sem.at[slot, 0])
    _wait_row_gather(y_hbm, buf_ref.at[slot, 1], sem.at[slot, 1])
    info = info_ref[...]
    g1 = info[:, 2:3]
    g2 = info[:, 3:4]
    ys = []
    ssq = jnp.zeros((tm, 1), F32)
    for s in range(ns):
        _issue_rows(y_hbm, d1n, buf_ref.at[nxt, 0], sem.at[nxt, 0], s * per, (s + 1) * per)
        _issue_rows(y_hbm, d2n, buf_ref.at[nxt, 1], sem.at[nxt, 1], s * per, (s + 1) * per)
        y_s = g1 * buf_ref[slot, 0, :, s, :] + g2 * buf_ref[slot, 1, :, s, :]
        ys.append(y_s)
        ssq = ssq + jnp.sum(y_s * y_s, axis=-1, keepdims=True)
    inv = lax.rsqrt(ssq / d + NORM_EPS)
    for s in range(ns):
        sl = slice(s * LANES, (s + 1) * LANES)
        o_ref[:, sl] = x_ref[:, sl] + gate_ref[:, sl] * (ys[s] * inv * gp_ref[:, sl])

    @pl.when(i == pl.num_programs(0) - 1)
    def _():
        _wait_row_gather(y_hbm, buf_ref.at[nxt, 0], sem.at[nxt, 0])
        _wait_row_gather(y_hbm, buf_ref.at[nxt, 1], sem.at[nxt, 1])


def _combine(ybuf3, dest1, dest2, info, x, g_post, gate, seq):
    n, d = x.shape
    tm = GATHER_TILE
    nt = n // tm
    tpb = seq // tm
    cur_spec = _idx_spec(tm, nt, 0)
    nxt_spec = _idx_spec(tm, nt, 1)
    assert dest1.shape == (nt, tm) and dest2.shape == (nt, tm)
    d1, d2 = dest1, dest2
    return pl.pallas_call(
        _combine_kernel,
        grid=(nt,),
        in_specs=[cur_spec, cur_spec, nxt_spec, nxt_spec, pl.BlockSpec(memory_space=pl.ANY),
                  _row_spec(tm, LANES), _row_spec(tm, d), _const_spec((1, d)), _batch_spec(d, tpb)],
        out_specs=_row_spec(tm, d),
        out_shape=jax.ShapeDtypeStruct((n, d), F32),
        scratch_shapes=[pltpu.VMEM((2, 2, tm, d // LANES, LANES), F32), pltpu.SemaphoreType.DMA((2, 2))],
        compiler_params=_params("arbitrary"),
        name="moe_combine",
    )(d1, d2, d1, d2, ybuf3, info, x, g_post, gate)


def _moe(x, g_pre, shift, scale, router_w, wg, wu, wd, g_post, gate, seq):
    n, d = x.shape
    wr_pad = jnp.pad(router_w, ((0, 0), (0, LANES - N_EXPERTS)))
    h3, info, fld, cnt = _route(x, g_pre, shift, scale, wr_pad, seq)
    e1, e2, r1, r2 = [fld[k].astype(jnp.int32) for k in range(len(ROUTE_FIELD_LANES))]
    counts = cnt[0, :N_EXPERTS].astype(jnp.int32)
    te = EXPERT_TILE
    padded = (counts + te - 1) // te * te
    pend = jnp.cumsum(padded)
    pstart = pend - padded
    dest1 = pstart[e1] + r1
    dest2 = pstart[e2] + r2
    rows = 2 * n + N_EXPERTS * te
    tok = (lax.broadcasted_iota(jnp.int32, e1.shape, 0) * e1.shape[1]
           + lax.broadcasted_iota(jnp.int32, e1.shape, 1))
    dest = jnp.concatenate([dest1, dest2], axis=0)
    inv = jnp.zeros((rows // te, te), jnp.int32).at[dest // te, dest % te].set(
        jnp.concatenate([tok, tok], axis=0), unique_indices=True)
    tile_start = jnp.arange(rows // te, dtype=jnp.int32) * te
    tile_expert = jnp.minimum(jnp.sum((tile_start[:, None] >= pend[None, :]).astype(jnp.int32), axis=1),
                              N_EXPERTS - 1)
    a = _expert_up(h3, inv, wg, wu, tile_expert)
    ybuf3 = _expert_down(a, wd, tile_expert)
    return _combine(ybuf3, dest1, dest2, info, x, g_post, gate, seq)


def _rope_lanes(w):
    half = QK_ROPE_DIM // 2
    z = jnp.zeros(w.shape[:-1] + (LANES // 2 - half,), w.dtype)
    return jnp.concatenate([w[..., :half], z, w[..., half:], z], axis=-1)


def kernel(x, c, positions, ada_w, ada_b, norm_mix_pre, norm_mix_post, norm_ffn_pre, norm_ffn_post, ssm_w_in, ssm_log_step, ssm_a_re, ssm_a_im, ssm_b_re, ssm_b_im, ssm_c_re, ssm_c_im, ssm_d, ssm_w_out, kv_ada_w, kv_ada_b, kv_norm, mla_w_dkv, mla_kv_norm, mla_w_ukv, mla_w_dq, mla_q_norm, mla_w_uq, mla_w_o, ffn_w_gu, ffn_w_down, moe_router, moe_w_gu, moe_w_down):
    bsz, seq, d = x.shape
    depth = ada_w.shape[0]
    n_a = ssm_w_in.shape[0]
    n = bsz * seq
    d_ff = ffn_w_down.shape[1]

    c_pad = jnp.pad(c, ((0, 8 - bsz), (0, 0)))
    ada = _ada_proj(c_pad, ada_w, ada_b[:, None, :], 2048)[:, :bsz]
    kv_ada = _ada_proj(c_pad, kv_ada_w[None], kv_ada_b[None, None, :], 2048)[0, :bsz]

    def vec(a):
        return a[:, None, :]

    def gain(gv):
        return gv[None, :]

    inv_freq = ROPE_THETA ** (-jnp.arange(0, QK_ROPE_DIM, 2, dtype=F32) / QK_ROPE_DIM)
    ang = positions.astype(F32)[..., None] * inv_freq
    cos = jnp.cos(ang).reshape(n, -1)
    sin = jnp.sin(ang).reshape(n, -1)
    zpad = jnp.zeros_like(cos)
    cos_t = jnp.concatenate([cos, zpad, cos, zpad], axis=-1)
    sin_t = jnp.concatenate([-sin, zpad, sin, zpad], axis=-1)

    xs = x.reshape(n, d)
    kn = kr = v = None
    for i in range(depth):
        sh_m, sc_m, g_m, sh_f, sc_f, g_f = [vec(a) for a in jnp.split(ada[i], 6, axis=-1)]
        if i < n_a:
            u3 = _s5_in(xs, gain(norm_mix_pre[i]), sh_m, sc_m, ssm_w_in[i].T.astype(BF16), seq)
            tabs = _s5_conv_tables(ssm_log_step[i], ssm_a_re[i], ssm_a_im[i], ssm_b_re[i], ssm_b_im[i],
                                   ssm_c_re[i], ssm_c_im[i], ssm_d[i], seq // LANES)
            y3 = _s5_conv(u3, *tabs, seq)
            xs = _s5_out(y3, ssm_w_out[i].astype(BF16), xs, gain(norm_mix_post[i]), g_m, seq)
        else:
            if i == n_a:
                kv_sh, kv_sc = [vec(a) for a in jnp.split(kv_ada, 2, axis=-1)]
                w_ukv = mla_w_ukv.reshape(KV_LORA_RANK, N_HEADS, QK_NOPE_DIM + V_HEAD_DIM)
                wk = w_ukv[:, :, :QK_NOPE_DIM].reshape(KV_LORA_RANK, -1).astype(BF16)
                wv = w_ukv[:, :, QK_NOPE_DIM:].reshape(KV_LORA_RANK, -1).astype(BF16)
                wc = mla_w_dkv[:, :KV_LORA_RANK].astype(BF16)
                wr = _rope_lanes(mla_w_dkv[:, KV_LORA_RANK:]).astype(BF16)
                kn, kr, v = _mla_kv(xs, gain(kv_norm), kv_sh, kv_sc, wc, wr, gain(mla_kv_norm), wk, wv,
                                    cos_t, sin_t, seq)
                kn = kn.reshape(bsz, seq, -1)
                kr = kr.reshape(bsz, seq, -1)
                v = v.reshape(bsz, seq, -1)
            j = i - n_a
            w_uq = mla_w_uq[j].reshape(-1, N_HEADS, QK_NOPE_DIM + QK_ROPE_DIM)
            wn = w_uq[:, :, :QK_NOPE_DIM].reshape(w_uq.shape[0], -1).astype(BF16)
            wqr = _rope_lanes(w_uq[:, :, QK_NOPE_DIM:]).reshape(w_uq.shape[0], -1).astype(BF16)
            qn, qr = _mla_q(xs, gain(norm_mix_pre[i]), sh_m, sc_m, mla_w_dq[j].astype(BF16),
                            gain(mla_q_norm[j]), wn, wqr, cos_t, sin_t, seq)
            o = _attention(qn.reshape(bsz, seq, -1), qr.reshape(bsz, seq, -1), kn, kr, v, bsz, seq)
            xs = _mm_post(o.reshape(n, -1), mla_w_o[j].astype(BF16), xs, gain(norm_mix_post[i]), g_m, seq)
        if i % 2 == 0:
            w_gu = ffn_w_gu[i // 2]
            a = _ffn_up(xs, gain(norm_ffn_pre[i]), sh_f, sc_f, w_gu[:, :d_ff].astype(BF16),
                        w_gu[:, d_ff:].astype(BF16), seq)
            xs = _mm_post(a, ffn_w_down[i // 2].astype(BF16), xs, gain(norm_ffn_post[i]), g_f, seq)
        else:
            wg, wu = _cast_experts(moe_w_gu, i // 2, 2)
            (wd,) = _cast_experts(moe_w_down, i // 2, 1)
            xs = _moe(xs, gain(norm_ffn_pre[i]), sh_f, sc_f, moe_router[i // 2], wg, wu, wd,
                      gain(norm_ffn_post[i]), g_f, seq)
    return xs.reshape(bsz, seq, d)
```

```python
import functools
import math

import jax
import jax.numpy as jnp
from jax import lax
from jax.experimental import pallas as pl
from jax.experimental.pallas import tpu as pltpu

F32 = jnp.float32
BF16 = jnp.bfloat16

NORM_EPS = 1e-6
LANES = 128
SSM_GROUP = 16
N_HEADS = 8
QK_NOPE_DIM = 128
QK_ROPE_DIM = 64
V_HEAD_DIM = 128
KV_LORA_RANK = 256
ROPE_THETA = 10000.0
N_EXPERTS = 8
SOFTMAX_SCALE = (QK_NOPE_DIM + QK_ROPE_DIM) ** -0.5
Q_SCALE = SOFTMAX_SCALE * math.log2(math.e)

ROW_TILE = 512
S5_ROW_TILE = 1024
ATTN_TILE = 1024
ROUTE_TILE = 256
EXPERT_TILE = 256
GATHER_TILE = 256
FF_CHUNK = 256
CAST_BLOCK_BYTES = 8 * 1024 * 1024


def _params(*sem):
    return pltpu.CompilerParams(dimension_semantics=sem)


def _rms(x, g):
    return x * lax.rsqrt(jnp.mean(x * x, axis=-1, keepdims=True) + NORM_EPS) * g


def _modnorm(x, g, shift, scale):
    return _rms(x, g) * (1.0 + scale) + shift


def _row_spec(tm, d):
    return pl.BlockSpec((tm, d), lambda i: (i, 0))


def _const_spec(shape):
    return pl.BlockSpec(shape, lambda i: tuple(0 for _ in shape))


def _batch_spec(d, tiles_per_batch):
    return pl.BlockSpec((None, 1, d), lambda i: (i // tiles_per_batch, 0, 0))


def _ada_kernel(c_ref, w_ref, b_ref, o_ref):
    c = c_ref[...]
    ca = c * jax.nn.sigmoid(c)
    o_ref[...] = jnp.dot(ca, w_ref[...], preferred_element_type=F32) + b_ref[...]


def _ada_proj(c_pad, w, b, tn):
    nl, d, e = w.shape
    return pl.pallas_call(
        _ada_kernel,
        grid=(nl, e // tn),
        in_specs=[
            pl.BlockSpec((8, d), lambda l, j: (0, 0)),
            pl.BlockSpec((None, d, tn), lambda l, j: (l, 0, j)),
            pl.BlockSpec((None, 1, tn), lambda l, j: (l, 0, j)),
        ],
        out_specs=pl.BlockSpec((None, 8, tn), lambda l, j: (l, 0, j)),
        out_shape=jax.ShapeDtypeStruct((nl, 8, e), F32),
        compiler_params=_params("parallel", "parallel"),
        name="ada_proj",
    )(c_pad, w, b)


def _cast_kernel(w_ref, *o_refs):
    f = o_refs[0].shape[-1]
    for k, o_ref in enumerate(o_refs):
        o_ref[...] = w_ref[:, k * f:(k + 1) * f].astype(o_ref.dtype)


def _cast_experts(w_all, layer, n_split):
    _, e, k, ftot = w_all.shape
    f = ftot // n_split
    tk = max(t for t in range(16, k + 1, 16) if k % t == 0 and t * ftot * 4 <= CAST_BLOCK_BYTES)
    return pl.pallas_call(
        _cast_kernel,
        grid=(e, k // tk),
        in_specs=[pl.BlockSpec((None, None, tk, ftot), lambda a, b: (layer, a, b, 0))],
        out_specs=[pl.BlockSpec((None, tk, f), lambda a, b: (a, b, 0)) for _ in range(n_split)],
        out_shape=[jax.ShapeDtypeStruct((e, k, f), BF16) for _ in range(n_split)],
        compiler_params=_params("parallel", "parallel"),
        name="cast_experts",
    )(w_all)


def _ffn_up_kernel(x_ref, g_ref, sh_ref, sc_ref, wg_ref, wu_ref, o_ref):
    h = _modnorm(x_ref[...], g_ref[...], sh_ref[...], sc_ref[...]).astype(BF16)
    f = o_ref.shape[1]
    for c in range(0, f, FF_CHUNK):
        gt = jnp.dot(h, wg_ref[:, c:c + FF_CHUNK], preferred_element_type=F32)
        up = jnp.dot(h, wu_ref[:, c:c + FF_CHUNK], preferred_element_type=F32)
        o_ref[:, c:c + FF_CHUNK] = (gt * jax.nn.sigmoid(gt) * up).astype(o_ref.dtype)


def _ffn_up(x, g, shift, scale, wg, wu, seq):
    n, d = x.shape
    f = wg.shape[1]
    tm = ROW_TILE
    tpb = seq // tm
    return pl.pallas_call(
        _ffn_up_kernel,
        grid=(n // tm,),
        in_specs=[_row_spec(tm, d), _const_spec((1, d)), _batch_spec(d, tpb), _batch_spec(d, tpb),
                  _const_spec((d, f)), _const_spec((d, f))],
        out_specs=_row_spec(tm, f),
        out_shape=jax.ShapeDtypeStruct((n, f), BF16),
        compiler_params=_params("parallel"),
        name="ffn_up",
    )(x, g, shift, scale, wg, wu)


def _mm_post_kernel(a_ref, w_ref, x_ref, gp_ref, gate_ref, o_ref):
    y = jnp.dot(a_ref[...], w_ref[...], preferred_element_type=F32)
    o_ref[...] = x_ref[...] + gate_ref[...] * _rms(y, gp_ref[...])


def _mm_post(a, w, x, g_post, gate, seq):
    n, k = a.shape
    d = x.shape[1]
    e = w.shape[1]
    tm = ROW_TILE
    tpb = seq // tm
    return pl.pallas_call(
        _mm_post_kernel,
        grid=(n // tm,),
        in_specs=[_row_spec(tm, k), _const_spec((k, e)), _row_spec(tm, d), _const_spec((1, d)),
                  _batch_spec(d, tpb)],
        out_specs=_row_spec(tm, d),
        out_shape=jax.ShapeDtypeStruct((n, d), F32),
        compiler_params=_params("parallel"),
        name="matmul_post",
    )(a, w, x, g_post, gate)


def _s5_in_kernel(x_ref, g_ref, sh_ref, sc_ref, wt_ref, o_ref):
    h = _modnorm(x_ref[...], g_ref[...], sh_ref[...], sc_ref[...]).astype(BF16)
    ut = lax.dot_general(wt_ref[...], h, (((1,), (1,)), ((), ())), preferred_element_type=F32)
    for k in range(o_ref.shape[1]):
        o_ref[:, k, :] = ut[:, k * LANES:(k + 1) * LANES]


def _s5_in(x, g, shift, scale, wt, seq):
    n, d = x.shape
    width = wt.shape[0]
    tm = S5_ROW_TILE
    tpb = seq // tm
    return pl.pallas_call(
        _s5_in_kernel,
        grid=(n // tm,),
        in_specs=[_row_spec(tm, d), _const_spec((1, d)), _batch_spec(d, tpb), _batch_spec(d, tpb),
                  _const_spec((width, d))],
        out_specs=pl.BlockSpec((width, tm // LANES, LANES), lambda i: (0, i, 0)),
        out_shape=jax.ShapeDtypeStruct((width, n // LANES, LANES), F32),
        compiler_params=_params("parallel"),
        name="s5_in",
    )(x, g, shift, scale, wt)


def _s5_conv_kernel(u_ref, k_ref, s_ref, cc_ref, a_ref, d_ref, o_ref, acc_ref, hs_ref, *, chunks_per_seq):
    gc, nc, t = u_ref.shape
    ny = gc * t
    pad = hs_ref.shape[0] - nc
    causal = lax.broadcasted_iota(jnp.int32, (t, t), 1) >= lax.broadcasted_iota(jnp.int32, (t, t), 0)

    def rhs_rows(ci):
        tiles = []
        for c in range(gc):
            lag = jnp.broadcast_to(k_ref[ci, c:c + 1, :], (t, t))
            toep = pltpu.roll(lag, 0, 1, stride=1, stride_axis=0)
            tiles.append(jnp.where(causal, toep, 0.0).astype(BF16))
        tiles.append(s_ref[ci])
        return jnp.concatenate(tiles, axis=1)

    for c0 in range(0, gc, 2):
        lhs = jnp.concatenate([u_ref[c0], u_ref[c0 + 1]], axis=1).astype(BF16)
        rhs = jnp.concatenate([rhs_rows(c0), rhs_rows(c0 + 1)], axis=0)
        part = jnp.dot(lhs, rhs, preferred_element_type=F32)
        if c0 == 0:
            acc_ref[...] = part
        else:
            acc_ref[...] += part

    hs_ref[0:pad, :] = jnp.zeros((pad, LANES), F32)
    hs_ref[pad:, :] = acc_ref[:, ny:]
    jl = lax.broadcasted_iota(jnp.int32, (nc, LANES), 0) & (chunks_per_seq - 1)
    for k in range(a_ref.shape[0] // 2):
        sft = 1 << k
        cur = hs_ref[pad:, :]
        sh = jnp.where(jl >= sft, hs_ref[pad - sft:pad - sft + nc, :], 0.0)
        hs_ref[pad:, :] = (cur + sh * a_ref[2 * k:2 * k + 1, :]
                           + pltpu.roll(sh, LANES // 2, 1) * a_ref[2 * k + 1:2 * k + 2, :])
    h_in = jnp.where(jl >= 1, hs_ref[pad - 1:pad - 1 + nc, :], 0.0).astype(BF16)
    for c in range(gc):
        y = acc_ref[:, c * t:(c + 1) * t] + jnp.dot(h_in, cc_ref[c], preferred_element_type=F32)
        o_ref[c] = jax.nn.gelu(y + d_ref[c:c + 1, :] * u_ref[c])


def _s5_conv(u3, ktab, stab, cctab, atab, dtab, seq):
    width, nc, t = u3.shape
    gc = SSM_GROUP
    chunks_per_seq = seq // t
    blk = lambda *tail: pl.BlockSpec((None,) + tail, lambda g: (g,) + tuple(0 for _ in tail))
    return pl.pallas_call(
        functools.partial(_s5_conv_kernel, chunks_per_seq=chunks_per_seq),
        grid=(width // gc,),
        in_specs=[pl.BlockSpec((gc, nc, t), lambda g: (g, 0, 0)),
                  blk(gc, gc, t), blk(gc, t, LANES), blk(gc, LANES, t), blk(atab.shape[1], LANES), blk(gc, t)],
        out_specs=pl.BlockSpec((gc, nc, t), lambda g: (g, 0, 0)),
        out_shape=jax.ShapeDtypeStruct((width, nc, t), F32),
        scratch_shapes=[pltpu.VMEM((nc, gc * t + LANES), F32), pltpu.VMEM((chunks_per_seq + nc, LANES), F32)],
        compiler_params=_params("parallel"),
        name="s5_conv",
    )(u3, ktab, stab, cctab, atab, dtab)


def _s5_out_kernel(y_ref, w_ref, x_ref, gp_ref, gate_ref, o_ref, m_ref):
    d = o_ref.shape[1]
    yt = jnp.concatenate([y_ref[:, k, :] for k in range(y_ref.shape[1])], axis=1)
    y = yt.T.astype(BF16)
    for c in range(0, d, FF_CHUNK):
        za = jnp.dot(y, w_ref[:, c:c + FF_CHUNK], preferred_element_type=F32)
        zb = jnp.dot(y, w_ref[:, d + c:d + c + FF_CHUNK], preferred_element_type=F32)
        m_ref[:, c:c + FF_CHUNK] = za * jax.nn.sigmoid(zb)
    o_ref[...] = x_ref[...] + gate_ref[...] * _rms(m_ref[...], gp_ref[...])


def _s5_out(y3, w, x, g_post, gate, seq):
    n, d = x.shape
    width = y3.shape[0]
    tm = S5_ROW_TILE
    tpb = seq // tm
    return pl.pallas_call(
        _s5_out_kernel,
        grid=(n // tm,),
        in_specs=[pl.BlockSpec((width, tm // LANES, LANES), lambda i: (0, i, 0)), _const_spec(w.shape),
                  _row_spec(tm, d), _const_spec((1, d)), _batch_spec(d, tpb)],
        out_specs=_row_spec(tm, d),
        out_shape=jax.ShapeDtypeStruct((n, d), F32),
        scratch_shapes=[pltpu.VMEM((tm, d), F32)],
        compiler_params=_params("parallel"),
        name="s5_out",
    )(y3, w, x, g_post, gate)


def _s5_conv_tables(log_step, a_re, a_im, b_re, b_im, c_re, c_im, d_skip, chunks_per_seq):
    g, p = a_re.shape
    t = LANES
    gc = SSM_GROUP
    hi = lax.Precision.HIGHEST
    delta = jnp.exp(log_step)[:, None]
    mag = jnp.exp(a_re * delta)
    ang = a_im * delta
    lb_re, lb_im = mag * jnp.cos(ang), mag * jnp.sin(ang)
    den = a_re * a_re + a_im * a_im
    nr, ni = lb_re - 1.0, lb_im
    coef_re = (nr * a_re + ni * a_im) / den
    coef_im = (ni * a_re - nr * a_im) / den
    bb_re = coef_re[..., None] * b_re - coef_im[..., None] * b_im
    bb_im = coef_re[..., None] * b_im + coef_im[..., None] * b_re

    def cmul(xr, xi, yr, yi):
        return xr * yr - xi * yi, xr * yi + xi * yr

    kk = jnp.arange(t + 1, dtype=F32)[:, None, None]
    magk = jnp.exp(kk * (a_re * delta)[None])
    angk = kk * ang[None]
    pw_r, pw_i = magk * jnp.cos(angk), magk * jnp.sin(angk)
    pr, pi = pw_r[:t], pw_i[:t]
    p1r, p1i = pw_r[1:], pw_i[1:]
    lt_re, lt_im = pw_r[t], pw_i[t]

    w_re = jnp.einsum('gcp,gpd->gcdp', c_re, bb_re) - jnp.einsum('gcp,gpd->gcdp', c_im, bb_im)
    w_im = jnp.einsum('gcp,gpd->gcdp', c_re, bb_im) + jnp.einsum('gcp,gpd->gcdp', c_im, bb_re)
    ktab = (jnp.einsum('gcdp,kgp->gdck', w_re, pr, precision=hi)
            - jnp.einsum('gcdp,kgp->gdck', w_im, pi, precision=hi))
    rr = pr[::-1].transpose(1, 0, 2)[:, None]
    ri = pi[::-1].transpose(1, 0, 2)[:, None]
    br = bb_re.transpose(0, 2, 1)[:, :, None, :]
    bi = bb_im.transpose(0, 2, 1)[:, :, None, :]
    stab = (jnp.concatenate([rr, rr], axis=-1) * jnp.concatenate([br, bi], axis=-1)
            + jnp.concatenate([ri, ri], axis=-1) * jnp.concatenate([-bi, br], axis=-1)).astype(BF16)
    cr = c_re[:, :, :, None]
    ci = c_im[:, :, :, None]
    qr = p1r.transpose(1, 2, 0)[:, None]
    qi = p1i.transpose(1, 2, 0)[:, None]
    cctab = (jnp.concatenate([cr, cr], axis=2) * jnp.concatenate([qr, -qi], axis=2)
             + jnp.concatenate([ci, ci], axis=2) * jnp.concatenate([-qi, -qr], axis=2)).astype(BF16)
    rows = []
    ar, ai = lt_re, lt_im
    for _ in range(int(math.log2(chunks_per_seq))):
        rows += [jnp.concatenate([ar, ar], axis=1), jnp.concatenate([-ai, ai], axis=1)]
        ar, ai = cmul(ar, ai, ar, ai)
    atab = jnp.stack(rows, axis=1)
    dtab = jnp.broadcast_to(d_skip.reshape(g, gc, 1), (g, gc, t))
    return ktab, stab, cctab, atab, dtab


def _rope(x, cos, sin):
    return x * cos + pltpu.roll(x, LANES // 2, 1) * sin


def _kv_kernel(x_ref, g_ref, sh_ref, sc_ref, wc_ref, wr_ref, gl_ref, wk_ref, wv_ref, cos_ref, sin_ref,
               kn_ref, kr_ref, v_ref):
    hs = _modnorm(x_ref[...], g_ref[...], sh_ref[...], sc_ref[...]).astype(BF16)
    c = jnp.dot(hs, wc_ref[...], preferred_element_type=F32)
    ckv = _rms(c, gl_ref[...]).astype(BF16)
    kn_ref[...] = jnp.dot(ckv, wk_ref[...], preferred_element_type=F32).astype(BF16)
    v_ref[...] = jnp.dot(ckv, wv_ref[...], preferred_element_type=F32).astype(BF16)
    r = jnp.dot(hs, wr_ref[...], preferred_element_type=F32)
    kr_ref[...] = _rope(r, cos_ref[...], sin_ref[...]).astype(BF16)


def _mla_kv(x, g, shift, scale, wc, wr, gl, wk, wv, cos, sin, seq):
    n, d = x.shape
    tm = ROW_TILE
    tpb = seq // tm
    hk = wk.shape[1]
    hv = wv.shape[1]
    return pl.pallas_call(
        _kv_kernel,
        grid=(n // tm,),
        in_specs=[_row_spec(tm, d), _const_spec((1, d)), _batch_spec(d, tpb), _batch_spec(d, tpb),
                  _const_spec(wc.shape), _const_spec(wr.shape), _const_spec(gl.shape),
                  _const_spec(wk.shape), _const_spec(wv.shape), _row_spec(tm, LANES), _row_spec(tm, LANES)],
        out_specs=[_row_spec(tm, hk), _row_spec(tm, LANES), _row_spec(tm, hv)],
        out_shape=[jax.ShapeDtypeStruct((n, hk), BF16), jax.ShapeDtypeStruct((n, LANES), BF16),
                   jax.ShapeDtypeStruct((n, hv), BF16)],
        compiler_params=_params("parallel"),
        name="mla_kv",
    )(x, g, shift, scale, wc, wr, gl, wk, wv, cos, sin)


def _q_kernel(x_ref, g_ref, sh_ref, sc_ref, wd_ref, gq_ref, wn_ref, wr_ref, cos_ref, sin_ref,
              qn_ref, qr_ref):
    h = _modnorm(x_ref[...], g_ref[...], sh_ref[...], sc_ref[...]).astype(BF16)
    ql = jnp.dot(h, wd_ref[...], preferred_element_type=F32)
    qn = _rms(ql, gq_ref[...]).astype(BF16)
    qn_ref[...] = (jnp.dot(qn, wn_ref[...], preferred_element_type=F32) * Q_SCALE).astype(BF16)
    r = jnp.dot(qn, wr_ref[...], preferred_element_type=F32)
    cos = cos_ref[...] * Q_SCALE
    sin = sin_ref[...] * Q_SCALE
    for hd in range(qr_ref.shape[1] // LANES):
        sl = slice(hd * LANES, (hd + 1) * LANES)
        qr_ref[:, sl] = _rope(r[:, sl], cos, sin).astype(BF16)


def _mla_q(x, g, shift, scale, wd, gq, wn, wr, cos, sin, seq):
    n, d = x.shape
    tm = ROW_TILE
    tpb = seq // tm
    e = wn.shape[1]
    return pl.pallas_call(
        _q_kernel,
        grid=(n // tm,),
        in_specs=[_row_spec(tm, d), _const_spec((1, d)), _batch_spec(d, tpb), _batch_spec(d, tpb),
                  _const_spec(wd.shape), _const_spec(gq.shape), _const_spec(wn.shape), _const_spec(wr.shape),
                  _row_spec(tm, LANES), _row_spec(tm, LANES)],
        out_specs=[_row_spec(tm, e), _row_spec(tm, e)],
        out_shape=[jax.ShapeDtypeStruct((n, e), BF16), jax.ShapeDtypeStruct((n, e), BF16)],
        compiler_params=_params("parallel"),
        name="mla_q",
    )(x, g, shift, scale, wd, gq, wn, wr, cos, sin)


def _attn_kernel(qn_ref, qr_ref, kn_ref, kr_ref, v_ref, o_ref):
    i = pl.program_id(2)
    tq = qn_ref.shape[0]
    tk = tq
    q = jnp.concatenate([qn_ref[...], qr_ref[...]], axis=1)

    def step(j, carry, diagonal):
        m, l, acc = carry
        start = pl.multiple_of(j * tk, tk)
        k = jnp.concatenate([kn_ref[pl.ds(start, tk), :], kr_ref[pl.ds(start, tk), :]], axis=1)
        s = lax.dot_general(q, k, (((1,), (1,)), ((), ())), preferred_element_type=F32)
        if diagonal:
            row = lax.broadcasted_iota(jnp.int32, (tq, tk), 0)
            col = lax.broadcasted_iota(jnp.int32, (tq, tk), 1)
            s = jnp.where(col <= row, s, -jnp.inf)
        m_new = jnp.maximum(m, jnp.max(s, axis=-1, keepdims=True))
        alpha = jnp.exp2(m - m_new)
        p = jnp.exp2(s - m_new)
        l = alpha * l + jnp.sum(p, axis=-1, keepdims=True)
        acc = alpha * acc + jnp.dot(p.astype(BF16), v_ref[pl.ds(start, tk), :], preferred_element_type=F32)
        return m_new, l, acc

    init = (jnp.full((tq, 1), -jnp.inf, F32), jnp.zeros((tq, 1), F32), jnp.zeros((tq, v_ref.shape[1]), F32))
    def pair(jj, c):
        return step(2 * jj + 1, step(2 * jj, c, False), False)

    carry = lax.fori_loop(0, i // 2, pair, init)
    carry = lax.fori_loop(0, i % 2, lambda _, c: step(i - 1, c, False), carry)
    _, l, acc = step(i, carry, diagonal=True)
    o_ref[...] = (acc / l).astype(o_ref.dtype)


def _attention(qn, qr, kn, kr, v, bsz, seq):
    tq = ATTN_TILE
    hd = LANES
    nq = seq // tq
    return pl.pallas_call(
        _attn_kernel,
        grid=(bsz, N_HEADS, nq),
        in_specs=[
            pl.BlockSpec((tq, hd), lambda b, h, i: (b * nq + i, h)),
            pl.BlockSpec((tq, hd), lambda b, h, i: (b * nq + i, h)),
            pl.BlockSpec((seq, hd), lambda b, h, i: (b, h)),
            pl.BlockSpec((seq, hd), lambda b, h, i: (b, 0)),
            pl.BlockSpec((seq, hd), lambda b, h, i: (b, h)),
        ],
        out_specs=pl.BlockSpec((tq, hd), lambda b, h, i: (b * nq + i, h)),
        out_shape=jax.ShapeDtypeStruct((bsz * seq, N_HEADS * hd), BF16),
        compiler_params=_params("parallel", "parallel", "parallel"),
        name="attention",
    )(qn, qr, kn, kr, v)


IDX_ROWS = 8
ROUTE_FIELD_LANES = (0, 1, 4, 5)


def _route_kernel(x_ref, g_ref, sh_ref, sc_ref, wr_ref, h_ref, info_ref, fld_ref, cnt_ref, carry_ref):
    @pl.when(pl.program_id(0) == 0)
    def _():
        carry_ref[...] = jnp.zeros_like(carry_ref)

    tm = x_ref.shape[0]
    h = _modnorm(x_ref[...], g_ref[...], sh_ref[...], sc_ref[...])
    _store_row_tiles(h_ref, h)
    logits = jnp.dot(h, wr_ref[...], preferred_element_type=F32, precision=lax.Precision.HIGHEST)
    lane = lax.broadcasted_iota(jnp.int32, (tm, LANES), 1).astype(F32)
    neg = -jnp.inf
    lg = jnp.where(lane < N_EXPERTS, logits, neg)
    l1 = jnp.max(lg, axis=-1, keepdims=True)
    e1 = jnp.min(jnp.where(lg == l1, lane, float(LANES)), axis=-1, keepdims=True)
    lg2 = jnp.where(lane == e1, neg, lg)
    l2 = jnp.max(lg2, axis=-1, keepdims=True)
    e2 = jnp.min(jnp.where(lg2 == l2, lane, float(LANES)), axis=-1, keepdims=True)
    tt = jnp.exp(l2 - l1)
    g1 = 1.0 / (1.0 + tt)
    g2 = tt / (1.0 + tt)
    oh1 = lane == e1
    oh2 = lane == e2
    oh = jnp.where(oh1 | oh2, 1.0, 0.0)
    tri = (lax.broadcasted_iota(jnp.int32, (tm, tm), 0) > lax.broadcasted_iota(jnp.int32, (tm, tm), 1))
    cum = jnp.dot(jnp.where(tri, 1.0, 0.0).astype(BF16), oh.astype(BF16),
                  preferred_element_type=F32) + carry_ref[...]
    r1 = jnp.sum(jnp.where(oh1, cum, 0.0), axis=-1, keepdims=True)
    r2 = jnp.sum(jnp.where(oh2, cum, 0.0), axis=-1, keepdims=True)
    carry_ref[...] = carry_ref[...] + jnp.sum(oh, axis=0, keepdims=True)
    cnt_ref[...] = carry_ref[...]
    info = jnp.where(lane == 0, e1,
           jnp.where(lane == 1, e2,
           jnp.where(lane == 2, g1,
           jnp.where(lane == 3, g2,
           jnp.where(lane == 4, r1, r2)))))
    info_ref[...] = info
    info_t = info.T
    row = pl.program_id(0) % IDX_ROWS
    for k, lane_k in enumerate(ROUTE_FIELD_LANES):
        fld_ref[k, pl.ds(row, 1), :] = info_t[lane_k:lane_k + 1, :]


def _route(x, g, shift, scale, wr_pad, seq):
    n, d = x.shape
    tm = ROUTE_TILE
    tpb = seq // tm
    return pl.pallas_call(
        _route_kernel,
        grid=(n // tm,),
        in_specs=[_row_spec(tm, d), _const_spec((1, d)), _batch_spec(d, tpb), _batch_spec(d, tpb),
                  _const_spec((d, LANES))],
        out_specs=[_row_tile_spec(tm, d), _row_spec(tm, LANES),
                   pl.BlockSpec((len(ROUTE_FIELD_LANES), IDX_ROWS, tm), lambda i: (0, i // IDX_ROWS, 0)),
                   _const_spec((1, LANES))],
        out_shape=[jax.ShapeDtypeStruct((n, d // LANES, LANES), F32), jax.ShapeDtypeStruct((n, LANES), F32),
                   jax.ShapeDtypeStruct((len(ROUTE_FIELD_LANES), n // tm, tm), F32),
                   jax.ShapeDtypeStruct((1, LANES), F32)],
        scratch_shapes=[pltpu.VMEM((1, LANES), F32)],
        compiler_params=_params("arbitrary"),
        name="moe_route",
    )(x, g, shift, scale, wr_pad)


def _row_tile_spec(tm, d):
    return pl.BlockSpec((tm, d // LANES, LANES), lambda i, *_: (i, 0, 0))


def _store_row_tiles(ref, val):
    for s in range(ref.shape[1]):
        ref[:, s, :] = val[:, s * LANES:(s + 1) * LANES]


def _load_row_tiles(ref):
    return jnp.concatenate([ref[:, s, :] for s in range(ref.shape[1])], axis=1)


DMA_ISSUE_UNROLL = 8


def _issue_row(src_hbm, idx_ref, dst_ref, sem, r, priority):
    pltpu.make_async_copy(src_hbm.at[pl.ds(idx_ref[0, r], 1)], dst_ref.at[pl.ds(r, 1)], sem).start(
        priority=priority)


def _issue_rows(src_hbm, idx_ref, dst_ref, sem, lo, hi):
    for r in range(lo, hi):
        _issue_row(src_hbm, idx_ref, dst_ref, sem, r, r % 2)


def _start_row_gather(src_hbm, idx_ref, dst_ref, sem):
    def issue(blk, c):
        for u in range(DMA_ISSUE_UNROLL):
            _issue_row(src_hbm, idx_ref, dst_ref, sem, blk * DMA_ISSUE_UNROLL + u, 0)
        return c

    lax.fori_loop(0, dst_ref.shape[0] // DMA_ISSUE_UNROLL, issue, 0)


def _wait_row_gather(src_hbm, dst_ref, sem):
    pltpu.make_async_copy(src_hbm.at[pl.ds(0, dst_ref.shape[0])], dst_ref, sem).wait()


def _idx_spec(tm, nt, ahead):
    def index_map(i, *_):
        return (jnp.minimum(i + ahead, nt - 1) // IDX_ROWS, 0)

    return pl.BlockSpec((IDX_ROWS, tm), index_map, memory_space=pltpu.SMEM)


def _idx_row(idx_ref, step):
    return idx_ref.at[pl.ds(step % IDX_ROWS, 1)]


def _expert_up_kernel(te_ref, idx_cur_ref, idx_nxt_ref, h_hbm, wg_ref, wu_ref, o_ref, xbuf_ref, sem):
    i = pl.program_id(0)
    slot = i % 2
    nxt = 1 - slot

    @pl.when(i == 0)
    def _():
        _start_row_gather(h_hbm, _idx_row(idx_cur_ref, i), xbuf_ref.at[0], sem.at[0])

    @pl.when(i + 1 < pl.num_programs(0))
    def _():
        _start_row_gather(h_hbm, _idx_row(idx_nxt_ref, i + 1), xbuf_ref.at[nxt], sem.at[nxt])

    _wait_row_gather(h_hbm, xbuf_ref.at[slot], sem.at[slot])
    h = _load_row_tiles(xbuf_ref.at[slot]).astype(BF16)
    f = o_ref.shape[1]
    for c in range(0, f, FF_CHUNK):
        gt = jnp.dot(h, wg_ref[:, c:c + FF_CHUNK], preferred_element_type=F32)
        up = jnp.dot(h, wu_ref[:, c:c + FF_CHUNK], preferred_element_type=F32)
        o_ref[:, c:c + FF_CHUNK] = (gt * jax.nn.sigmoid(gt) * up).astype(o_ref.dtype)


def _expert_up(h3, inv, wg, wu, tile_expert):
    nt, tm = inv.shape
    rows = nt * tm
    _, ns, _ = h3.shape
    d = wg.shape[1]
    f = wg.shape[2]
    idx3 = inv
    return pl.pallas_call(
        _expert_up_kernel,
        grid_spec=pltpu.PrefetchScalarGridSpec(
            num_scalar_prefetch=1,
            grid=(nt,),
            in_specs=[_idx_spec(tm, nt, 0), _idx_spec(tm, nt, 1),
                      pl.BlockSpec(memory_space=pl.ANY),
                      pl.BlockSpec((None, d, f), lambda i, te: (te[i], 0, 0)),
                      pl.BlockSpec((None, d, f), lambda i, te: (te[i], 0, 0))],
            out_specs=pl.BlockSpec((tm, f), lambda i, te: (i, 0)),
            scratch_shapes=[pltpu.VMEM((2, tm, ns, LANES), F32), pltpu.SemaphoreType.DMA((2,))],
        ),
        out_shape=jax.ShapeDtypeStruct((rows, f), BF16),
        compiler_params=_params("arbitrary"),
        name="expert_up",
    )(tile_expert, idx3, idx3, h3, wg, wu)


def _expert_down_kernel(te_ref, a_ref, w_ref, o_ref):
    _store_row_tiles(o_ref, jnp.dot(a_ref[...], w_ref[...], preferred_element_type=F32))


def _expert_down(a, wd, tile_expert):
    rows, f = a.shape
    d = wd.shape[2]
    tm = EXPERT_TILE
    return pl.pallas_call(
        _expert_down_kernel,
        grid_spec=pltpu.PrefetchScalarGridSpec(
            num_scalar_prefetch=1,
            grid=(rows // tm,),
            in_specs=[pl.BlockSpec((tm, f), lambda i, te: (i, 0)),
                      pl.BlockSpec((None, f, d), lambda i, te: (te[i], 0, 0))],
            out_specs=_row_tile_spec(tm, d),
        ),
        out_shape=jax.ShapeDtypeStruct((rows, d // LANES, LANES), F32),
        compiler_params=_params("arbitrary"),
        name="expert_down",
    )(tile_expert, a, wd)


def _combine_kernel(d1c_ref, d2c_ref, d1n_ref, d2n_ref, y_hbm, info_ref, x_ref, gp_ref, gate_ref, o_ref,
                    buf_ref, sem):
    i = pl.program_id(0)
    slot = i % 2
    nxt = 1 - slot
    tm, d = o_ref.shape
    ns = buf_ref.shape[3]
    per = tm // ns

    j = jnp.minimum(i + 1, pl.num_programs(0) - 1)
    d1n = _idx_row(d1n_ref, j)
    d2n = _idx_row(d2n_ref, j)

    @pl.when(i == 0)
    def _():
        _start_row_gather(y_hbm, _idx_row(d1c_ref, i), buf_ref.at[0, 0], sem.at[0, 0])
        _start_row_gather(y_hbm, _idx_row(d2c_ref, i), buf_ref.at[0, 1], sem.at[0, 1])

    _wait_row_gather(y_hbm, buf_ref.at[slot, 0], sem.at[slot, 0])
    _wait_row_gather(y_hbm, buf_ref.at[slot, 1], sem.at[slot, 1])
    info = info_ref[...]
    g1 = info[:, 2:3]
    g2 = info[:, 3:4]
    ys = []
    ssq = jnp.zeros((tm, 1), F32)
    for s in range(ns):
        _issue_rows(y_hbm, d1n, buf_ref.at[nxt, 0], sem.at[nxt, 0], s * per, (s + 1) * per)
        _issue_rows(y_hbm, d2n, buf_ref.at[nxt, 1], sem.at[nxt, 1], s * per, (s + 1) * per)
        y_s = g1 * buf_ref[slot, 0, :, s, :] + g2 * buf_ref[slot, 1, :, s, :]
        ys.append(y_s)
        ssq = ssq + jnp.sum(y_s * y_s, axis=-1, keepdims=True)
    inv = lax.rsqrt(ssq / d + NORM_EPS)
    for s in range(ns):
        sl = slice(s * LANES, (s + 1) * LANES)
        o_ref[:, sl] = x_ref[:, sl] + gate_ref[:, sl] * (ys[s] * inv * gp_ref[:, sl])

    @pl.when(i == pl.num_programs(0) - 1)
    def _():
        _wait_row_gather(y_hbm, buf_ref.at[nxt, 0], sem.at[nxt, 0])
        _wait_row_gather(y_hbm, buf_ref.at[nxt, 1], sem.at[nxt, 1])


def _combine(ybuf3, dest1, dest2, info, x, g_post, gate, seq):
    n, d = x.shape
    tm = GATHER_TILE
    nt = n // tm
    tpb = seq // tm
    cur_spec = _idx_spec(tm, nt, 0)
    nxt_spec = _idx_spec(tm, nt, 1)
    assert dest1.shape == (nt, tm) and dest2.shape == (nt, tm)
    d1, d2 = dest1, dest2
    return pl.pallas_call(
        _combine_kernel,
        grid=(nt,),
        in_specs=[cur_spec, cur_spec, nxt_spec, nxt_spec, pl.BlockSpec(memory_space=pl.ANY),
                  _row_spec(tm, LANES), _row_spec(tm, d), _const_spec((1, d)), _batch_spec(d, tpb)],
        out_specs=_row_spec(tm, d),
        out_shape=jax.ShapeDtypeStruct((n, d), F32),
        scratch_shapes=[pltpu.VMEM((2, 2, tm, d // LANES, LANES), F32), pltpu.SemaphoreType.DMA((2, 2))],
        compiler_params=_params("arbitrary"),
        name="moe_combine",
    )(d1, d2, d1, d2, ybuf3, info, x, g_post, gate)


def _moe(x, g_pre, shift, scale, router_w, wg, wu, wd, g_post, gate, seq):
    n, d = x.shape
    wr_pad = jnp.pad(router_w, ((0, 0), (0, LANES - N_EXPERTS)))
    h3, info, fld, cnt = _route(x, g_pre, shift, scale, wr_pad, seq)
    e1, e2, r1, r2 = [fld[k].astype(jnp.int32) for k in range(len(ROUTE_FIELD_LANES))]
    counts = cnt[0, :N_EXPERTS].astype(jnp.int32)
    te = EXPERT_TILE
    padded = (counts + te - 1) // te * te
    pend = jnp.cumsum(padded)
    pstart = pend - padded
    dest1 = pstart[e1] + r1
    dest2 = pstart[e2] + r2
    rows = 2 * n + N_EXPERTS * te
    tok = (lax.broadcasted_iota(jnp.int32, e1.shape, 0) * e1.shape[1]
           + lax.broadcasted_iota(jnp.int32, e1.shape, 1))
    dest = jnp.concatenate([dest1, dest2], axis=0)
    inv = jnp.zeros((rows // te, te), jnp.int32).at[dest // te, dest % te].set(
        jnp.concatenate([tok, tok], axis=0), unique_indices=True)
    tile_start = jnp.arange(rows // te, dtype=jnp.int32) * te
    tile_expert = jnp.minimum(jnp.sum((tile_start[:, None] >= pend[None, :]).astype(jnp.int32), axis=1),
                              N_EXPERTS - 1)
    a = _expert_up(h3, inv, wg, wu, tile_expert)
    ybuf3 = _expert_down(a, wd, tile_expert)
    return _combine(ybuf3, dest1, dest2, info, x, g_post, gate, seq)


def _rope_lanes(w):
    half = QK_ROPE_DIM // 2
    z = jnp.zeros(w.shape[:-1] + (LANES // 2 - half,), w.dtype)
    return jnp.concatenate([w[..., :half], z, w[..., half:], z], axis=-1)


def kernel(x, c, positions, ada_w, ada_b, norm_mix_pre, norm_mix_post, norm_ffn_pre, norm_ffn_post, ssm_w_in, ssm_log_step, ssm_a_re, ssm_a_im, ssm_b_re, ssm_b_im, ssm_c_re, ssm_c_im, ssm_d, ssm_w_out, kv_ada_w, kv_ada_b, kv_norm, mla_w_dkv, mla_kv_norm, mla_w_ukv, mla_w_dq, mla_q_norm, mla_w_uq, mla_w_o, ffn_w_gu, ffn_w_down, moe_router, moe_w_gu, moe_w_down):
    bsz, seq, d = x.shape
    depth = ada_w.shape[0]
    n_a = ssm_w_in.shape[0]
    n = bsz * seq
    d_ff = ffn_w_down.shape[1]

    c_pad = jnp.pad(c, ((0, 8 - bsz), (0, 0)))
    ada = _ada_proj(c_pad, ada_w, ada_b[:, None, :], 2048)[:, :bsz]
    kv_ada = _ada_proj(c_pad, kv_ada_w[None], kv_ada_b[None, None, :], 2048)[0, :bsz]

    def vec(a):
        return a[:, None, :]

    def gain(gv):
        return gv[None, :]

    inv_freq = ROPE_THETA ** (-jnp.arange(0, QK_ROPE_DIM, 2, dtype=F32) / QK_ROPE_DIM)
    ang = positions.astype(F32)[..., None] * inv_freq
    cos = jnp.cos(ang).reshape(n, -1)
    sin = jnp.sin(ang).reshape(n, -1)
    zpad = jnp.zeros_like(cos)
    cos_t = jnp.concatenate([cos, zpad, cos, zpad], axis=-1)
    sin_t = jnp.concatenate([-sin, zpad, sin, zpad], axis=-1)

    xs = x.reshape(n, d)
    kn = kr = v = None
    for i in range(depth):
        sh_m, sc_m, g_m, sh_f, sc_f, g_f = [vec(a) for a in jnp.split(ada[i], 6, axis=-1)]
        if i < n_a:
            u3 = _s5_in(xs, gain(norm_mix_pre[i]), sh_m, sc_m, ssm_w_in[i].T.astype(BF16), seq)
            tabs = _s5_conv_tables(ssm_log_step[i], ssm_a_re[i], ssm_a_im[i], ssm_b_re[i], ssm_b_im[i],
                                   ssm_c_re[i], ssm_c_im[i], ssm_d[i], seq // LANES)
            y3 = _s5_conv(u3, *tabs, seq)
            xs = _s5_out(y3, ssm_w_out[i].astype(BF16), xs, gain(norm_mix_post[i]), g_m, seq)
        else:
            if i == n_a:
                kv_sh, kv_sc = [vec(a) for a in jnp.split(kv_ada, 2, axis=-1)]
                w_ukv = mla_w_ukv.reshape(KV_LORA_RANK, N_HEADS, QK_NOPE_DIM + V_HEAD_DIM)
                wk = w_ukv[:, :, :QK_NOPE_DIM].reshape(KV_LORA_RANK, -1).astype(BF16)
                wv = w_ukv[:, :, QK_NOPE_DIM:].reshape(KV_LORA_RANK, -1).astype(BF16)
                wc = mla_w_dkv[:, :KV_LORA_RANK].astype(BF16)
                wr = _rope_lanes(mla_w_dkv[:, KV_LORA_RANK:]).astype(BF16)
                kn, kr, v = _mla_kv(xs, gain(kv_norm), kv_sh, kv_sc, wc, wr, gain(mla_kv_norm), wk, wv,
                                    cos_t, sin_t, seq)
            j = i - n_a
            w_uq = mla_w_uq[j].reshape(-1, N_HEADS, QK_NOPE_DIM + QK_ROPE_DIM)
            wn = w_uq[:, :, :QK_NOPE_DIM].reshape(w_uq.shape[0], -1).astype(BF16)
            wqr = _rope_lanes(w_uq[:, :, QK_NOPE_DIM:]).reshape(w_uq.shape[0], -1).astype(BF16)
            qn, qr = _mla_q(xs, gain(norm_mix_pre[i]), sh_m, sc_m, mla_w_dq[j].astype(BF16),
                            gain(mla_q_norm[j]), wn, wqr, cos_t, sin_t, seq)
            o = _attention(qn, qr, kn, kr, v, bsz, seq)
            xs = _mm_post(o, mla_w_o[j].astype(BF16), xs, gain(norm_mix_post[i]), g_m, seq)
        if i % 2 == 0:
            w_gu = ffn_w_gu[i // 2]
            a = _ffn_up(xs, gain(norm_ffn_pre[i]), sh_f, sc_f, w_gu[:, :d_ff].astype(BF16),
                        w_gu[:, d_ff:].astype(BF16), seq)
            xs = _mm_post(a, ffn_w_down[i // 2].astype(BF16), xs, gain(norm_ffn_post[i]), g_f, seq)
        else:
            wg, wu = _cast_experts(moe_w_gu, i // 2, 2)
            (wd,) = _cast_experts(moe_w_down, i // 2, 1)
            xs = _moe(xs, gain(norm_ffn_pre[i]), sh_f, sc_f, moe_router[i // 2], wg, wu, wd,
                      gain(norm_ffn_post[i]), g_f, seq)
    return xs.reshape(bsz, seq, d)
```

```python
import functools
import math

import jax
import jax.numpy as jnp
from jax import lax
from jax.experimental import pallas as pl
from jax.experimental.pallas import tpu as pltpu

F32 = jnp.float32
BF16 = jnp.bfloat16

NORM_EPS = 1e-6
LANES = 128
SSM_GROUP = 16
N_HEADS = 8
QK_NOPE_DIM = 128
QK_ROPE_DIM = 64
V_HEAD_DIM = 128
KV_LORA_RANK = 256
ROPE_THETA = 10000.0
N_EXPERTS = 8
SOFTMAX_SCALE = (QK_NOPE_DIM + QK_ROPE_DIM) ** -0.5
Q_SCALE = SOFTMAX_SCALE * math.log2(math.e)

ADA_COL_TILE = 2048
ROW_TILE = 512
S5_ROW_TILE = 1024
ATTN_TILE = 1024
ROUTE_TILE = 256
EXPERT_TILE = 256
GATHER_TILE = 256
FF_CHUNK = 256
CAST_BLOCK_BYTES = 8 * 1024 * 1024


def _params(*sem):
    return pltpu.CompilerParams(dimension_semantics=sem)


def _rms(x, g):
    return x * lax.rsqrt(jnp.mean(x * x, axis=-1, keepdims=True) + NORM_EPS) * g


def _modnorm(x, g, shift, scale):
    return _rms(x, g) * (1.0 + scale) + shift


def _row_spec(tm, d):
    return pl.BlockSpec((tm, d), lambda i: (i, 0))


def _const_spec(shape):
    return pl.BlockSpec(shape, lambda i: tuple(0 for _ in shape))


def _batch_spec(d, tiles_per_batch):
    return pl.BlockSpec((None, 1, d), lambda i: (i // tiles_per_batch, 0, 0))


def _ada_kernel(c_ref, w_ref, b_ref, o_ref):
    c = c_ref[...]
    ca = c * jax.nn.sigmoid(c)
    o_ref[...] = jnp.dot(ca, w_ref[...], preferred_element_type=F32) + b_ref[...]


def _ada_proj(c_pad, w, b, tn):
    nl, d, e = w.shape
    return pl.pallas_call(
        _ada_kernel,
        grid=(nl, e // tn),
        in_specs=[
            pl.BlockSpec((8, d), lambda l, j: (0, 0)),
            pl.BlockSpec((None, d, tn), lambda l, j: (l, 0, j)),
            pl.BlockSpec((None, 1, tn), lambda l, j: (l, 0, j)),
        ],
        out_specs=pl.BlockSpec((None, 8, tn), lambda l, j: (l, 0, j)),
        out_shape=jax.ShapeDtypeStruct((nl, 8, e), F32),
        compiler_params=_params("parallel", "parallel"),
        name="ada_proj",
    )(c_pad, w, b)


def _cast_kernel(w_ref, *o_refs):
    f = o_refs[0].shape[-1]
    for k, o_ref in enumerate(o_refs):
        o_ref[...] = w_ref[:, k * f:(k + 1) * f].astype(o_ref.dtype)


def _cast_experts(w_all, layer, n_split):
    _, e, k, ftot = w_all.shape
    f = ftot // n_split
    tk = max(t for t in range(16, k + 1, 16) if k % t == 0 and t * ftot * 4 <= CAST_BLOCK_BYTES)
    return pl.pallas_call(
        _cast_kernel,
        grid=(e, k // tk),
        in_specs=[pl.BlockSpec((None, None, tk, ftot), lambda a, b: (layer, a, b, 0))],
        out_specs=[pl.BlockSpec((None, tk, f), lambda a, b: (a, b, 0)) for _ in range(n_split)],
        out_shape=[jax.ShapeDtypeStruct((e, k, f), BF16) for _ in range(n_split)],
        compiler_params=_params("parallel", "parallel"),
        name="cast_experts",
    )(w_all)


def _ffn_up_kernel(x_ref, g_ref, sh_ref, sc_ref, wg_ref, wu_ref, o_ref):
    h = _modnorm(x_ref[...], g_ref[...], sh_ref[...], sc_ref[...]).astype(BF16)
    f = o_ref.shape[1]
    for c in range(0, f, FF_CHUNK):
        gt = jnp.dot(h, wg_ref[:, c:c + FF_CHUNK], preferred_element_type=F32)
        up = jnp.dot(h, wu_ref[:, c:c + FF_CHUNK], preferred_element_type=F32)
        o_ref[:, c:c + FF_CHUNK] = (gt * jax.nn.sigmoid(gt) * up).astype(o_ref.dtype)


def _ffn_up(x, g, shift, scale, wg, wu, seq):
    n, d = x.shape
    f = wg.shape[1]
    tm = ROW_TILE
    tpb = seq // tm
    return pl.pallas_call(
        _ffn_up_kernel,
        grid=(n // tm,),
        in_specs=[_row_spec(tm, d), _const_spec((1, d)), _batch_spec(d, tpb), _batch_spec(d, tpb),
                  _const_spec((d, f)), _const_spec((d, f))],
        out_specs=_row_spec(tm, f),
        out_shape=jax.ShapeDtypeStruct((n, f), BF16),
        compiler_params=_params("parallel"),
        name="ffn_up",
    )(x, g, shift, scale, wg, wu)


def _mm_post_kernel(a_ref, w_ref, x_ref, gp_ref, gate_ref, o_ref):
    y = jnp.dot(a_ref[...], w_ref[...], preferred_element_type=F32)
    o_ref[...] = x_ref[...] + gate_ref[...] * _rms(y, gp_ref[...])


def _mm_post(a, w, x, g_post, gate, seq):
    n, k = a.shape
    d = x.shape[1]
    e = w.shape[1]
    tm = ROW_TILE
    tpb = seq // tm
    return pl.pallas_call(
        _mm_post_kernel,
        grid=(n // tm,),
        in_specs=[_row_spec(tm, k), _const_spec((k, e)), _row_spec(tm, d), _const_spec((1, d)),
                  _batch_spec(d, tpb)],
        out_specs=_row_spec(tm, d),
        out_shape=jax.ShapeDtypeStruct((n, d), F32),
        compiler_params=_params("parallel"),
        name="matmul_post",
    )(a, w, x, g_post, gate)


def _s5_in_kernel(x_ref, g_ref, sh_ref, sc_ref, wt_ref, o_ref):
    h = _modnorm(x_ref[...], g_ref[...], sh_ref[...], sc_ref[...]).astype(BF16)
    ut = lax.dot_general(wt_ref[...], h, (((1,), (1,)), ((), ())), preferred_element_type=F32)
    for k in range(o_ref.shape[1]):
        o_ref[:, k, :] = ut[:, k * LANES:(k + 1) * LANES]


def _s5_in(x, g, shift, scale, wt, seq):
    n, d = x.shape
    width = wt.shape[0]
    tm = S5_ROW_TILE
    tpb = seq // tm
    return pl.pallas_call(
        _s5_in_kernel,
        grid=(n // tm,),
        in_specs=[_row_spec(tm, d), _const_spec((1, d)), _batch_spec(d, tpb), _batch_spec(d, tpb),
                  _const_spec((width, d))],
        out_specs=pl.BlockSpec((width, tm // LANES, LANES), lambda i: (0, i, 0)),
        out_shape=jax.ShapeDtypeStruct((width, n // LANES, LANES), F32),
        compiler_params=_params("parallel"),
        name="s5_in",
    )(x, g, shift, scale, wt)


def _s5_conv_kernel(u_ref, k_ref, s_ref, cc_ref, a_ref, d_ref, o_ref, acc_ref, hs_ref, *, chunks_per_seq):
    gc, nc, t = u_ref.shape
    ny = gc * t
    pad = hs_ref.shape[0] - nc
    causal = lax.broadcasted_iota(jnp.int32, (t, t), 1) >= lax.broadcasted_iota(jnp.int32, (t, t), 0)

    def rhs_rows(ci):
        tiles = []
        for c in range(gc):
            lag = jnp.broadcast_to(k_ref[ci, c:c + 1, :], (t, t))
            toep = pltpu.roll(lag, 0, 1, stride=1, stride_axis=0)
            tiles.append(jnp.where(causal, toep, 0.0).astype(BF16))
        tiles.append(s_ref[ci])
        return jnp.concatenate(tiles, axis=1)

    for c0 in range(0, gc, 2):
        lhs = jnp.concatenate([u_ref[c0], u_ref[c0 + 1]], axis=1).astype(BF16)
        rhs = jnp.concatenate([rhs_rows(c0), rhs_rows(c0 + 1)], axis=0)
        part = jnp.dot(lhs, rhs, preferred_element_type=F32)
        if c0 == 0:
            acc_ref[...] = part
        else:
            acc_ref[...] += part

    hs_ref[0:pad, :] = jnp.zeros((pad, LANES), F32)
    hs_ref[pad:, :] = acc_ref[:, ny:]
    jl = lax.broadcasted_iota(jnp.int32, (nc, LANES), 0) & (chunks_per_seq - 1)
    for k in range(a_ref.shape[0] // 2):
        sft = 1 << k
        cur = hs_ref[pad:, :]
        sh = jnp.where(jl >= sft, hs_ref[pad - sft:pad - sft + nc, :], 0.0)
        hs_ref[pad:, :] = (cur + sh * a_ref[2 * k:2 * k + 1, :]
                           + pltpu.roll(sh, LANES // 2, 1) * a_ref[2 * k + 1:2 * k + 2, :])
    h_in = jnp.where(jl >= 1, hs_ref[pad - 1:pad - 1 + nc, :], 0.0).astype(BF16)
    for c in range(gc):
        y = acc_ref[:, c * t:(c + 1) * t] + jnp.dot(h_in, cc_ref[c], preferred_element_type=F32)
        o_ref[c] = jax.nn.gelu(y + d_ref[c:c + 1, :] * u_ref[c])


def _s5_conv(u3, ktab, stab, cctab, atab, dtab, seq):
    width, nc, t = u3.shape
    gc = SSM_GROUP
    chunks_per_seq = seq // t
    blk = lambda *tail: pl.BlockSpec((None,) + tail, lambda g: (g,) + tuple(0 for _ in tail))
    return pl.pallas_call(
        functools.partial(_s5_conv_kernel, chunks_per_seq=chunks_per_seq),
        grid=(width // gc,),
        in_specs=[pl.BlockSpec((gc, nc, t), lambda g: (g, 0, 0)),
                  blk(gc, gc, t), blk(gc, t, LANES), blk(gc, LANES, t), blk(atab.shape[1], LANES), blk(gc, t)],
        out_specs=pl.BlockSpec((gc, nc, t), lambda g: (g, 0, 0)),
        out_shape=jax.ShapeDtypeStruct((width, nc, t), F32),
        scratch_shapes=[pltpu.VMEM((nc, gc * t + LANES), F32), pltpu.VMEM((chunks_per_seq + nc, LANES), F32)],
        compiler_params=_params("parallel"),
        name="s5_conv",
    )(u3, ktab, stab, cctab, atab, dtab)


def _s5_out_kernel(y_ref, w_ref, x_ref, gp_ref, gate_ref, o_ref, m_ref):
    d = o_ref.shape[1]
    yt = jnp.concatenate([y_ref[:, k, :] for k in range(y_ref.shape[1])], axis=1)
    y = yt.T.astype(BF16)
    for c in range(0, d, FF_CHUNK):
        za = jnp.dot(y, w_ref[:, c:c + FF_CHUNK], preferred_element_type=F32)
        zb = jnp.dot(y, w_ref[:, d + c:d + c + FF_CHUNK], preferred_element_type=F32)
        m_ref[:, c:c + FF_CHUNK] = za * jax.nn.sigmoid(zb)
    o_ref[...] = x_ref[...] + gate_ref[...] * _rms(m_ref[...], gp_ref[...])


def _s5_out(y3, w, x, g_post, gate, seq):
    n, d = x.shape
    width = y3.shape[0]
    tm = S5_ROW_TILE
    tpb = seq // tm
    return pl.pallas_call(
        _s5_out_kernel,
        grid=(n // tm,),
        in_specs=[pl.BlockSpec((width, tm // LANES, LANES), lambda i: (0, i, 0)), _const_spec(w.shape),
                  _row_spec(tm, d), _const_spec((1, d)), _batch_spec(d, tpb)],
        out_specs=_row_spec(tm, d),
        out_shape=jax.ShapeDtypeStruct((n, d), F32),
        scratch_shapes=[pltpu.VMEM((tm, d), F32)],
        compiler_params=_params("parallel"),
        name="s5_out",
    )(y3, w, x, g_post, gate)


def _s5_conv_tables(log_step, a_re, a_im, b_re, b_im, c_re, c_im, d_skip, chunks_per_seq):
    g, p = a_re.shape
    t = LANES
    gc = SSM_GROUP
    hi = lax.Precision.HIGHEST
    delta = jnp.exp(log_step)[:, None]
    mag = jnp.exp(a_re * delta)
    ang = a_im * delta
    lb_re, lb_im = mag * jnp.cos(ang), mag * jnp.sin(ang)
    den = a_re * a_re + a_im * a_im
    nr, ni = lb_re - 1.0, lb_im
    coef_re = (nr * a_re + ni * a_im) / den
    coef_im = (ni * a_re - nr * a_im) / den
    bb_re = coef_re[..., None] * b_re - coef_im[..., None] * b_im
    bb_im = coef_re[..., None] * b_im + coef_im[..., None] * b_re

    def cmul(xr, xi, yr, yi):
        return xr * yr - xi * yi, xr * yi + xi * yr

    kk = jnp.arange(t + 1, dtype=F32)[:, None, None]
    magk = jnp.exp(kk * (a_re * delta)[None])
    angk = kk * ang[None]
    pw_r, pw_i = magk * jnp.cos(angk), magk * jnp.sin(angk)
    pr, pi = pw_r[:t], pw_i[:t]
    p1r, p1i = pw_r[1:], pw_i[1:]
    lt_re, lt_im = pw_r[t], pw_i[t]

    w_re = jnp.einsum('gcp,gpd->gcdp', c_re, bb_re) - jnp.einsum('gcp,gpd->gcdp', c_im, bb_im)
    w_im = jnp.einsum('gcp,gpd->gcdp', c_re, bb_im) + jnp.einsum('gcp,gpd->gcdp', c_im, bb_re)
    ktab = (jnp.einsum('gcdp,kgp->gdck', w_re, pr, precision=hi)
            - jnp.einsum('gcdp,kgp->gdck', w_im, pi, precision=hi))
    rr = pr[::-1].transpose(1, 0, 2)[:, None]
    ri = pi[::-1].transpose(1, 0, 2)[:, None]
    br = bb_re.transpose(0, 2, 1)[:, :, None, :]
    bi = bb_im.transpose(0, 2, 1)[:, :, None, :]
    stab = (jnp.concatenate([rr, rr], axis=-1) * jnp.concatenate([br, bi], axis=-1)
            + jnp.concatenate([ri, ri], axis=-1) * jnp.concatenate([-bi, br], axis=-1)).astype(BF16)
    cr = c_re[:, :, :, None]
    ci = c_im[:, :, :, None]
    qr = p1r.transpose(1, 2, 0)[:, None]
    qi = p1i.transpose(1, 2, 0)[:, None]
    cctab = (jnp.concatenate([cr, cr], axis=2) * jnp.concatenate([qr, -qi], axis=2)
             + jnp.concatenate([ci, ci], axis=2) * jnp.concatenate([-qi, -qr], axis=2)).astype(BF16)
    rows = []
    ar, ai = lt_re, lt_im
    for _ in range(int(math.log2(chunks_per_seq))):
        rows += [jnp.concatenate([ar, ar], axis=1), jnp.concatenate([-ai, ai], axis=1)]
        ar, ai = cmul(ar, ai, ar, ai)
    atab = jnp.stack(rows, axis=1)
    dtab = jnp.broadcast_to(d_skip.reshape(g, gc, 1), (g, gc, t))
    return ktab, stab, cctab, atab, dtab


def _rope(x, cos, sin):
    return x * cos + pltpu.roll(x, LANES // 2, 1) * sin


def _kv_kernel(x_ref, g_ref, sh_ref, sc_ref, wc_ref, wr_ref, gl_ref, wk_ref, wv_ref, cos_ref, sin_ref,
               kn_ref, kr_ref, v_ref):
    hs = _modnorm(x_ref[...], g_ref[...], sh_ref[...], sc_ref[...]).astype(BF16)
    c = jnp.dot(hs, wc_ref[...], preferred_element_type=F32)
    ckv = _rms(c, gl_ref[...]).astype(BF16)
    kn_ref[...] = jnp.dot(ckv, wk_ref[...], preferred_element_type=F32).astype(BF16)
    v_ref[...] = jnp.dot(ckv, wv_ref[...], preferred_element_type=F32).astype(BF16)
    r = jnp.dot(hs, wr_ref[...], preferred_element_type=F32)
    kr_ref[...] = _rope(r, cos_ref[...], sin_ref[...]).astype(BF16)


def _mla_kv(x, g, shift, scale, wc, wr, gl, wk, wv, cos, sin, seq):
    n, d = x.shape
    tm = ROW_TILE
    tpb = seq // tm
    hk = wk.shape[1]
    hv = wv.shape[1]
    return pl.pallas_call(
        _kv_kernel,
        grid=(n // tm,),
        in_specs=[_row_spec(tm, d), _const_spec((1, d)), _batch_spec(d, tpb), _batch_spec(d, tpb),
                  _const_spec(wc.shape), _const_spec(wr.shape), _const_spec(gl.shape),
                  _const_spec(wk.shape), _const_spec(wv.shape), _row_spec(tm, LANES), _row_spec(tm, LANES)],
        out_specs=[_row_spec(tm, hk), _row_spec(tm, LANES), _row_spec(tm, hv)],
        out_shape=[jax.ShapeDtypeStruct((n, hk), BF16), jax.ShapeDtypeStruct((n, LANES), BF16),
                   jax.ShapeDtypeStruct((n, hv), BF16)],
        compiler_params=_params("parallel"),
        name="mla_kv",
    )(x, g, shift, scale, wc, wr, gl, wk, wv, cos, sin)


def _q_kernel(x_ref, g_ref, sh_ref, sc_ref, wd_ref, gq_ref, wn_ref, wr_ref, cos_ref, sin_ref,
              qn_ref, qr_ref):
    h = _modnorm(x_ref[...], g_ref[...], sh_ref[...], sc_ref[...]).astype(BF16)
    ql = jnp.dot(h, wd_ref[...], preferred_element_type=F32)
    qn = _rms(ql, gq_ref[...]).astype(BF16)
    qn_ref[...] = (jnp.dot(qn, wn_ref[...], preferred_element_type=F32) * Q_SCALE).astype(BF16)
    r = jnp.dot(qn, wr_ref[...], preferred_element_type=F32)
    cos = cos_ref[...] * Q_SCALE
    sin = sin_ref[...] * Q_SCALE
    for hd in range(qr_ref.shape[1] // LANES):
        sl = slice(hd * LANES, (hd + 1) * LANES)
        qr_ref[:, sl] = _rope(r[:, sl], cos, sin).astype(BF16)


def _mla_q(x, g, shift, scale, wd, gq, wn, wr, cos, sin, seq):
    n, d = x.shape
    tm = ROW_TILE
    tpb = seq // tm
    e = wn.shape[1]
    return pl.pallas_call(
        _q_kernel,
        grid=(n // tm,),
        in_specs=[_row_spec(tm, d), _const_spec((1, d)), _batch_spec(d, tpb), _batch_spec(d, tpb),
                  _const_spec(wd.shape), _const_spec(gq.shape), _const_spec(wn.shape), _const_spec(wr.shape),
                  _row_spec(tm, LANES), _row_spec(tm, LANES)],
        out_specs=[_row_spec(tm, e), _row_spec(tm, e)],
        out_shape=[jax.ShapeDtypeStruct((n, e), BF16), jax.ShapeDtypeStruct((n, e), BF16)],
        compiler_params=_params("parallel"),
        name="mla_q",
    )(x, g, shift, scale, wd, gq, wn, wr, cos, sin)


def _attn_kernel(qn_ref, qr_ref, kn_ref, kr_ref, v_ref, o_ref):
    i = pl.program_id(2)
    tq = qn_ref.shape[0]
    tk = tq
    q = jnp.concatenate([qn_ref[...], qr_ref[...]], axis=1)

    def step(j, carry, diagonal):
        m, l, acc = carry
        start = pl.multiple_of(j * tk, tk)
        k = jnp.concatenate([kn_ref[pl.ds(start, tk), :], kr_ref[pl.ds(start, tk), :]], axis=1)
        s = lax.dot_general(q, k, (((1,), (1,)), ((), ())), preferred_element_type=F32)
        if diagonal:
            row = lax.broadcasted_iota(jnp.int32, (tq, tk), 0)
            col = lax.broadcasted_iota(jnp.int32, (tq, tk), 1)
            s = jnp.where(col <= row, s, -jnp.inf)
        m_new = jnp.maximum(m, jnp.max(s, axis=-1, keepdims=True))
        alpha = jnp.exp2(m - m_new)
        p = jnp.exp2(s - m_new)
        l = alpha * l + jnp.sum(p, axis=-1, keepdims=True)
        acc = alpha * acc + jnp.dot(p.astype(BF16), v_ref[pl.ds(start, tk), :], preferred_element_type=F32)
        return m_new, l, acc

    init = (jnp.full((tq, 1), -jnp.inf, F32), jnp.zeros((tq, 1), F32), jnp.zeros((tq, v_ref.shape[1]), F32))
    def pair(jj, c):
        return step(2 * jj + 1, step(2 * jj, c, False), False)

    carry = lax.fori_loop(0, i // 2, pair, init)
    carry = lax.fori_loop(0, i % 2, lambda _, c: step(i - 1, c, False), carry)
    _, l, acc = step(i, carry, diagonal=True)
    o_ref[...] = (acc / l).astype(o_ref.dtype)


def _attention(qn, qr, kn, kr, v, bsz, seq):
    tq = ATTN_TILE
    hd = LANES
    nq = seq // tq
    return pl.pallas_call(
        _attn_kernel,
        grid=(bsz, N_HEADS, nq),
        in_specs=[
            pl.BlockSpec((tq, hd), lambda b, h, i: (b * nq + i, h)),
            pl.BlockSpec((tq, hd), lambda b, h, i: (b * nq + i, h)),
            pl.BlockSpec((seq, hd), lambda b, h, i: (b, h)),
            pl.BlockSpec((seq, hd), lambda b, h, i: (b, 0)),
            pl.BlockSpec((seq, hd), lambda b, h, i: (b, h)),
        ],
        out_specs=pl.BlockSpec((tq, hd), lambda b, h, i: (b * nq + i, h)),
        out_shape=jax.ShapeDtypeStruct((bsz * seq, N_HEADS * hd), BF16),
        compiler_params=_params("parallel", "parallel", "parallel"),
        name="attention",
    )(qn, qr, kn, kr, v)


IDX_ROWS = 8
ROUTE_FIELD_LANES = (0, 1, 4, 5)


def _route_kernel(x_ref, g_ref, sh_ref, sc_ref, wr_ref, h_ref, info_ref, fld_ref, cnt_ref, carry_ref):
    @pl.when(pl.program_id(0) == 0)
    def _():
        carry_ref[...] = jnp.zeros_like(carry_ref)

    tm = x_ref.shape[0]
    h = _modnorm(x_ref[...], g_ref[...], sh_ref[...], sc_ref[...])
    _store_row_tiles(h_ref, h)
    w = wr_ref[...]
    h_hi = h.astype(BF16)
    h_lo = (h - h_hi.astype(F32)).astype(BF16)
    w_hi = w.astype(BF16)
    w_lo = (w - w_hi.astype(F32)).astype(BF16)
    logits = (jnp.dot(h_hi, w_hi, preferred_element_type=F32) + jnp.dot(h_hi, w_lo, preferred_element_type=F32)
              + jnp.dot(h_lo, w_hi, preferred_element_type=F32))
    lane = lax.broadcasted_iota(jnp.int32, (tm, LANES), 1).astype(F32)
    neg = -jnp.inf
    lg = jnp.where(lane < N_EXPERTS, logits, neg)
    l1 = jnp.max(lg, axis=-1, keepdims=True)
    e1 = jnp.min(jnp.where(lg == l1, lane, float(LANES)), axis=-1, keepdims=True)
    lg2 = jnp.where(lane == e1, neg, lg)
    l2 = jnp.max(lg2, axis=-1, keepdims=True)
    e2 = jnp.min(jnp.where(lg2 == l2, lane, float(LANES)), axis=-1, keepdims=True)
    tt = jnp.exp(l2 - l1)
    g1 = 1.0 / (1.0 + tt)
    g2 = tt / (1.0 + tt)
    oh1 = lane == e1
    oh2 = lane == e2
    oh = jnp.where(oh1 | oh2, 1.0, 0.0)
    tri = (lax.broadcasted_iota(jnp.int32, (tm, tm), 0) > lax.broadcasted_iota(jnp.int32, (tm, tm), 1))
    cum = jnp.dot(jnp.where(tri, 1.0, 0.0).astype(BF16), oh.astype(BF16),
                  preferred_element_type=F32) + carry_ref[...]
    r1 = jnp.sum(jnp.where(oh1, cum, 0.0), axis=-1, keepdims=True)
    r2 = jnp.sum(jnp.where(oh2, cum, 0.0), axis=-1, keepdims=True)
    carry_ref[...] = carry_ref[...] + jnp.sum(oh, axis=0, keepdims=True)
    cnt_ref[...] = carry_ref[...]
    info = jnp.where(lane == 0, e1,
           jnp.where(lane == 1, e2,
           jnp.where(lane == 2, g1,
           jnp.where(lane == 3, g2,
           jnp.where(lane == 4, r1, r2)))))
    info_ref[...] = info
    info_t = info.T
    row = pl.program_id(0) % IDX_ROWS
    for k, lane_k in enumerate(ROUTE_FIELD_LANES):
        fld_ref[k, pl.ds(row, 1), :] = info_t[lane_k:lane_k + 1, :]


def _route(x, g, shift, scale, wr_pad, seq):
    n, d = x.shape
    tm = ROUTE_TILE
    tpb = seq // tm
    return pl.pallas_call(
        _route_kernel,
        grid=(n // tm,),
        in_specs=[_row_spec(tm, d), _const_spec((1, d)), _batch_spec(d, tpb), _batch_spec(d, tpb),
                  _const_spec((d, LANES))],
        out_specs=[_row_tile_spec(tm, d), _row_spec(tm, LANES),
                   pl.BlockSpec((len(ROUTE_FIELD_LANES), IDX_ROWS, tm), lambda i: (0, i // IDX_ROWS, 0)),
                   _const_spec((1, LANES))],
        out_shape=[jax.ShapeDtypeStruct((n, d // LANES, LANES), F32), jax.ShapeDtypeStruct((n, LANES), F32),
                   jax.ShapeDtypeStruct((len(ROUTE_FIELD_LANES), n // tm, tm), F32),
                   jax.ShapeDtypeStruct((1, LANES), F32)],
        scratch_shapes=[pltpu.VMEM((1, LANES), F32)],
        compiler_params=_params("arbitrary"),
        name="moe_route",
    )(x, g, shift, scale, wr_pad)


def _row_tile_spec(tm, d):
    return pl.BlockSpec((tm, d // LANES, LANES), lambda i, *_: (i, 0, 0))


def _store_row_tiles(ref, val):
    for s in range(ref.shape[1]):
        ref[:, s, :] = val[:, s * LANES:(s + 1) * LANES]


def _load_row_tiles(ref):
    return jnp.concatenate([ref[:, s, :] for s in range(ref.shape[1])], axis=1)


DMA_ISSUE_UNROLL = 8


def _issue_row(src_hbm, idx_ref, dst_ref, sem, r, priority):
    pltpu.make_async_copy(src_hbm.at[pl.ds(idx_ref[0, r], 1)], dst_ref.at[pl.ds(r, 1)], sem).start(
        priority=priority)


def _issue_rows(src_hbm, idx_ref, dst_ref, sem, lo, hi):
    for r in range(lo, hi):
        _issue_row(src_hbm, idx_ref, dst_ref, sem, r, r % 2)


def _start_row_gather(src_hbm, idx_ref, dst_ref, sem):
    def issue(blk, c):
        for u in range(DMA_ISSUE_UNROLL):
            _issue_row(src_hbm, idx_ref, dst_ref, sem, blk * DMA_ISSUE_UNROLL + u, 0)
        return c

    lax.fori_loop(0, dst_ref.shape[0] // DMA_ISSUE_UNROLL, issue, 0)


def _wait_row_gather(src_hbm, dst_ref, sem):
    pltpu.make_async_copy(src_hbm.at[pl.ds(0, dst_ref.shape[0])], dst_ref, sem).wait()


def _idx_spec(tm, nt, ahead):
    def index_map(i, *_):
        return (jnp.minimum(i + ahead, nt - 1) // IDX_ROWS, 0)

    return pl.BlockSpec((IDX_ROWS, tm), index_map, memory_space=pltpu.SMEM)


def _idx_row(idx_ref, step):
    return idx_ref.at[pl.ds(step % IDX_ROWS, 1)]


def _expert_up_kernel(te_ref, idx_cur_ref, idx_nxt_ref, h_hbm, wg_ref, wu_ref, o_ref, xbuf_ref, sem):
    i = pl.program_id(0)
    slot = i % 2
    nxt = 1 - slot

    @pl.when(i == 0)
    def _():
        _start_row_gather(h_hbm, _idx_row(idx_cur_ref, i), xbuf_ref.at[0], sem.at[0])

    @pl.when(i + 1 < pl.num_programs(0))
    def _():
        _start_row_gather(h_hbm, _idx_row(idx_nxt_ref, i + 1), xbuf_ref.at[nxt], sem.at[nxt])

    _wait_row_gather(h_hbm, xbuf_ref.at[slot], sem.at[slot])
    h = _load_row_tiles(xbuf_ref.at[slot]).astype(BF16)
    f = o_ref.shape[1]
    for c in range(0, f, FF_CHUNK):
        gt = jnp.dot(h, wg_ref[:, c:c + FF_CHUNK], preferred_element_type=F32)
        up = jnp.dot(h, wu_ref[:, c:c + FF_CHUNK], preferred_element_type=F32)
        o_ref[:, c:c + FF_CHUNK] = (gt * jax.nn.sigmoid(gt) * up).astype(o_ref.dtype)


def _expert_up(h3, inv, wg, wu, tile_expert):
    nt, tm = inv.shape
    rows = nt * tm
    _, ns, _ = h3.shape
    d = wg.shape[1]
    f = wg.shape[2]
    idx3 = inv
    return pl.pallas_call(
        _expert_up_kernel,
        grid_spec=pltpu.PrefetchScalarGridSpec(
            num_scalar_prefetch=1,
            grid=(nt,),
            in_specs=[_idx_spec(tm, nt, 0), _idx_spec(tm, nt, 1),
                      pl.BlockSpec(memory_space=pl.ANY),
                      pl.BlockSpec((None, d, f), lambda i, te: (te[i], 0, 0)),
                      pl.BlockSpec((None, d, f), lambda i, te: (te[i], 0, 0))],
            out_specs=pl.BlockSpec((tm, f), lambda i, te: (i, 0)),
            scratch_shapes=[pltpu.VMEM((2, tm, ns, LANES), F32), pltpu.SemaphoreType.DMA((2,))],
        ),
        out_shape=jax.ShapeDtypeStruct((rows, f), BF16),
        compiler_params=_params("arbitrary"),
        name="expert_up",
    )(tile_expert, idx3, idx3, h3, wg, wu)


def _expert_down_kernel(te_ref, a_ref, w_ref, o_ref):
    _store_row_tiles(o_ref, jnp.dot(a_ref[...], w_ref[...], preferred_element_type=F32))


def _expert_down(a, wd, tile_expert):
    rows, f = a.shape
    d = wd.shape[2]
    tm = EXPERT_TILE
    return pl.pallas_call(
        _expert_down_kernel,
        grid_spec=pltpu.PrefetchScalarGridSpec(
            num_scalar_prefetch=1,
            grid=(rows // tm,),
            in_specs=[pl.BlockSpec((tm, f), lambda i, te: (i, 0)),
                      pl.BlockSpec((None, f, d), lambda i, te: (te[i], 0, 0))],
            out_specs=_row_tile_spec(tm, d),
        ),
        out_shape=jax.ShapeDtypeStruct((rows, d // LANES, LANES), F32),
        compiler_params=_params("arbitrary"),
        name="expert_down",
    )(tile_expert, a, wd)


def _combine_kernel(d1c_ref, d2c_ref, d1n_ref, d2n_ref, y_hbm, info_ref, x_ref, gp_ref, gate_ref, o_ref,
                    buf_ref, sem):
    i = pl.program_id(0)
    slot = i % 2
    nxt = 1 - slot
    tm, d = o_ref.shape
    ns = buf_ref.shape[3]
    per = tm // ns

    j = jnp.minimum(i + 1, pl.num_programs(0) - 1)
    d1n = _idx_row(d1n_ref, j)
    d2n = _idx_row(d2n_ref, j)

    @pl.when(i == 0)
    def _():
        _start_row_gather(y_hbm, _idx_row(d1c_ref, i), buf_ref.at[0, 0], sem.at[0, 0])
        _start_row_gather(y_hbm, _idx_row(d2c_ref, i), buf_ref.at[0, 1], sem.at[0, 1])

    _wait_row_gather(y_hbm, buf_ref.at[slot, 0], sem.at[slot, 0])
    _wait_row_gather(y_hbm, buf_ref.at[slot, 1], sem.at[slot, 1])
    info = info_ref[...]
    g1 = info[:, 2:3]
    g2 = info[:, 3:4]
    ys = []
    ssq = jnp.zeros((tm, 1), F32)
    for s in range(ns):
        _issue_rows(y_hbm, d1n, buf_ref.at[nxt, 0], sem.at[nxt, 0], s * per, (s + 1) * per)
        _issue_rows(y_hbm, d2n, buf_ref.at[nxt, 1], sem.at[nxt, 1], s * per, (s + 1) * per)
        y_s = g1 * buf_ref[slot, 0, :, s, :] + g2 * buf_ref[slot, 1, :, s, :]
        ys.append(y_s)
        ssq = ssq + jnp.sum(y_s * y_s, axis=-1, keepdims=True)
    inv = lax.rsqrt(ssq / d + NORM_EPS)
    for s in range(ns):
        sl = slice(s * LANES, (s + 1) * LANES)
        o_ref[:, sl] = x_ref[:, sl] + gate_ref[:, sl] * (ys[s] * inv * gp_ref[:, sl])

    @pl.when(i == pl.num_programs(0) - 1)
    def _():
        _wait_row_gather(y_hbm, buf_ref.at[nxt, 0], sem.at[nxt, 0])
        _wait_row_gather(y_hbm, buf_ref.at[nxt, 1], sem.at[nxt, 1])


def _combine(ybuf3, dest1, dest2, info, x, g_post, gate, seq):
    n, d = x.shape
    tm = GATHER_TILE
    nt = n // tm
    tpb = seq // tm
    cur_spec = _idx_spec(tm, nt, 0)
    nxt_spec = _idx_spec(tm, nt, 1)
    assert dest1.shape == (nt, tm) and dest2.shape == (nt, tm)
    d1, d2 = dest1, dest2
    return pl.pallas_call(
        _combine_kernel,
        grid=(nt,),
        in_specs=[cur_spec, cur_spec, nxt_spec, nxt_spec, pl.BlockSpec(memory_space=pl.ANY),
                  _row_spec(tm, LANES), _row_spec(tm, d), _const_spec((1, d)), _batch_spec(d, tpb)],
        out_specs=_row_spec(tm, d),
        out_shape=jax.ShapeDtypeStruct((n, d), F32),
        scratch_shapes=[pltpu.VMEM((2, 2, tm, d // LANES, LANES), F32), pltpu.SemaphoreType.DMA((2, 2))],
        compiler_params=_params("arbitrary"),
        name="moe_combine",
    )(d1, d2, d1, d2, ybuf3, info, x, g_post, gate)


def _moe(x, g_pre, shift, scale, router_w, wg, wu, wd, g_post, gate, seq):
    n, d = x.shape
    wr_pad = jnp.pad(router_w, ((0, 0), (0, LANES - N_EXPERTS)))
    h3, info, fld, cnt = _route(x, g_pre, shift, scale, wr_pad, seq)
    e1, e2, r1, r2 = [fld[k].astype(jnp.int32) for k in range(len(ROUTE_FIELD_LANES))]
    counts = cnt[0, :N_EXPERTS].astype(jnp.int32)
    te = EXPERT_TILE
    padded = (counts + te - 1) // te * te
    pend = jnp.cumsum(padded)
    pstart = pend - padded
    dest1 = pstart[e1] + r1
    dest2 = pstart[e2] + r2
    rows = 2 * n + N_EXPERTS * te
    tok = (lax.broadcasted_iota(jnp.int32, e1.shape, 0) * e1.shape[1]
           + lax.broadcasted_iota(jnp.int32, e1.shape, 1))
    dest = jnp.concatenate([dest1, dest2], axis=0)
    inv = jnp.zeros((rows // te, te), jnp.int32).at[dest // te, dest % te].set(
        jnp.concatenate([tok, tok], axis=0), unique_indices=True)
    tile_start = jnp.arange(rows // te, dtype=jnp.int32) * te
    tile_expert = jnp.minimum(jnp.sum((tile_start[:, None] >= pend[None, :]).astype(jnp.int32), axis=1),
                              N_EXPERTS - 1)
    a = _expert_up(h3, inv, wg, wu, tile_expert)
    ybuf3 = _expert_down(a, wd, tile_expert)
    return _combine(ybuf3, dest1, dest2, info, x, g_post, gate, seq)


def _rope_lanes(w):
    half = QK_ROPE_DIM // 2
    z = jnp.zeros(w.shape[:-1] + (LANES // 2 - half,), w.dtype)
    return jnp.concatenate([w[..., :half], z, w[..., half:], z], axis=-1)


def kernel(x, c, positions, ada_w, ada_b, norm_mix_pre, norm_mix_post, norm_ffn_pre, norm_ffn_post, ssm_w_in, ssm_log_step, ssm_a_re, ssm_a_im, ssm_b_re, ssm_b_im, ssm_c_re, ssm_c_im, ssm_d, ssm_w_out, kv_ada_w, kv_ada_b, kv_norm, mla_w_dkv, mla_kv_norm, mla_w_ukv, mla_w_dq, mla_q_norm, mla_w_uq, mla_w_o, ffn_w_gu, ffn_w_down, moe_router, moe_w_gu, moe_w_down):
    bsz, seq, d = x.shape
    depth = ada_w.shape[0]
    n_a = ssm_w_in.shape[0]
    n = bsz * seq
    d_ff = ffn_w_down.shape[1]

    c_pad = jnp.pad(c, ((0, 8 - bsz), (0, 0)))
    ada = _ada_proj(c_pad, ada_w, ada_b[:, None, :], ADA_COL_TILE)[:, :bsz]
    kv_ada = _ada_proj(c_pad, kv_ada_w[None], kv_ada_b[None, None, :], ADA_COL_TILE)[0, :bsz]

    def vec(a):
        return a[:, None, :]

    def gain(gv):
        return gv[None, :]

    inv_freq = ROPE_THETA ** (-jnp.arange(0, QK_ROPE_DIM, 2, dtype=F32) / QK_ROPE_DIM)
    ang = positions.astype(F32)[..., None] * inv_freq
    cos = jnp.cos(ang).reshape(n, -1)
    sin = jnp.sin(ang).reshape(n, -1)
    zpad = jnp.zeros_like(cos)
    cos_t = jnp.concatenate([cos, zpad, cos, zpad], axis=-1)
    sin_t = jnp.concatenate([-sin, zpad, sin, zpad], axis=-1)

    xs = x.reshape(n, d)
    kn = kr = v = None
    for i in range(depth):
        sh_m, sc_m, g_m, sh_f, sc_f, g_f = [vec(a) for a in jnp.split(ada[i], 6, axis=-1)]
        if i < n_a:
            u3 = _s5_in(xs, gain(norm_mix_pre[i]), sh_m, sc_m, ssm_w_in[i].T.astype(BF16), seq)
            tabs = _s5_conv_tables(ssm_log_step[i], ssm_a_re[i], ssm_a_im[i], ssm_b_re[i], ssm_b_im[i],
                                   ssm_c_re[i], ssm_c_im[i], ssm_d[i], seq // LANES)
            y3 = _s5_conv(u3, *tabs, seq)
            xs = _s5_out(y3, ssm_w_out[i].astype(BF16), xs, gain(norm_mix_post[i]), g_m, seq)
        else:
            if i == n_a:
                kv_sh, kv_sc = [vec(a) for a in jnp.split(kv_ada, 2, axis=-1)]
                w_ukv = mla_w_ukv.reshape(KV_LORA_RANK, N_HEADS, QK_NOPE_DIM + V_HEAD_DIM)
                wk = w_ukv[:, :, :QK_NOPE_DIM].reshape(KV_LORA_RANK, -1).astype(BF16)
                wv = w_ukv[:, :, QK_NOPE_DIM:].reshape(KV_LORA_RANK, -1).astype(BF16)
                wc = mla_w_dkv[:, :KV_LORA_RANK].astype(BF16)
                wr = _rope_lanes(mla_w_dkv[:, KV_LORA_RANK:]).astype(BF16)
                kn, kr, v = _mla_kv(xs, gain(kv_norm), kv_sh, kv_sc, wc, wr, gain(mla_kv_norm), wk, wv,
                                    cos_t, sin_t, seq)
            j = i - n_a
            w_uq = mla_w_uq[j].reshape(-1, N_HEADS, QK_NOPE_DIM + QK_ROPE_DIM)
            wn = w_uq[:, :, :QK_NOPE_DIM].reshape(w_uq.shape[0], -1).astype(BF16)
            wqr = _rope_lanes(w_uq[:, :, QK_NOPE_DIM:]).reshape(w_uq.shape[0], -1).astype(BF16)
            qn, qr = _mla_q(xs, gain(norm_mix_pre[i]), sh_m, sc_m, mla_w_dq[j].astype(BF16),
                            gain(mla_q_norm[j]), wn, wqr, cos_t, sin_t, seq)
            o = _attention(qn, qr, kn, kr, v, bsz, seq)
            xs = _mm_post(o, mla_w_o[j].astype(BF16), xs, gain(norm_mix_post[i]), g_m, seq)
        if i % 2 == 0:
            w_gu = ffn_w_gu[i // 2]
            a = _ffn_up(xs, gain(norm_ffn_pre[i]), sh_f, sc_f, w_gu[:, :d_ff].astype(BF16),
                        w_gu[:, d_ff:].astype(BF16), seq)
            xs = _mm_post(a, ffn_w_down[i // 2].astype(BF16), xs, gain(norm_ffn_post[i]), g_f, seq)
        else:
            wg, wu = _cast_experts(moe_w_gu, i // 2, 2)
            (wd,) = _cast_experts(moe_w_down, i // 2, 1)
            xs = _moe(xs, gain(norm_ffn_pre[i]), sh_f, sc_f, moe_router[i // 2], wg, wu, wd,
                      gain(norm_ffn_post[i]), g_f, seq)
    return xs.reshape(bsz, seq, d)
```
